```python
import math
import jax
import jax.numpy as jnp
from jax import lax
import numpy as np

D_MODEL = 2048
BATCH = 2
SEQ = 8192
DEPTH = 4

CTX_LEN = 256
GRID_W = 64
N_MIXERS = 2
N_HGRN = (DEPTH + N_MIXERS - 1) // N_MIXERS
N_HYENA = DEPTH // N_MIXERS
N_MOD = 6
EPS = 1e-6

HG_EXPAND = 128
HG_HEADS = D_MODEL // HG_EXPAND
HG_DK = HG_EXPAND
HG_DV = D_MODEL // HG_HEADS
HG_CHUNK = 64

HY_SHORT = 3
HY_BANDS = 16
HY_EMB = 2 * HY_BANDS + 1
HY_FILTER_WIDTH = 64
HY_DECAY_PCT_MIN = 0.3
HY_DECAY_PCT_MAX = 1.5
HY_DECAY_TARGET = 1e-2

N_EXPERTS = 32
TOP_K = 4
D_EXPERT = 3 * D_MODEL // 8
SWIGLU_ALPHA = 1.702
SWIGLU_LIMIT = 7.0

kernel_name = 'hybrid_hgrn2_hyena_moe_prefix_dit'


def rms_norm(x, g):
    xf = x.astype(jnp.float32)
    y = xf * lax.rsqrt(jnp.mean(xf * xf, axis=-1, keepdims=True) + EPS)
    return (y * g.astype(jnp.float32)).astype(x.dtype)


def ada_mod(cond, w, b):
    m = jax.nn.silu(cond) @ w + b
    return jnp.split(m[..., None, :], N_MOD, axis=-1)


def modulate(h, shift, scale):
    return h * (1 + scale) + shift


def gla_scan(q, k, v, log_f, s0):
    bsz, nh, length, _ = q.shape
    dv = v.shape[-1]
    n_chunks = length // HG_CHUNK

    def chunks(t):
        return jnp.moveaxis(t.reshape(bsz, nh, n_chunks, HG_CHUNK, t.shape[-1]), 2, 0)

    lower = jnp.tril(jnp.ones((HG_CHUNK, HG_CHUNK), dtype=bool))[:, :, None]

    def step(s, inp):
        qc, kc, vc, gc = inp
        b = jnp.cumsum(gc, axis=2)
        b_last = b[:, :, -1:, :]
        o_inter = jnp.einsum('bhik,bhkv->bhiv', qc * jnp.exp(b), s)
        rel = jnp.where(lower, b[:, :, :, None, :] - b[:, :, None, :, :], -jnp.inf)
        att = jnp.einsum('bhik,bhjk,bhijk->bhij', qc, kc, jnp.exp(rel))
        o_intra = jnp.einsum('bhij,bhjv->bhiv', att, vc)
        s_new = jnp.exp(b_last)[:, :, 0, :, None] * s + jnp.einsum('bhjk,bhjv->bhkv', kc * jnp.exp(b_last - b), vc)
        return s_new, o_inter + o_intra

    s_fin, o = lax.scan(step, s0, (chunks(q), chunks(k), chunks(v), chunks(log_f)))
    return jnp.moveaxis(o, 0, 2).reshape(bsz, nh, length, dv), s_fin


def hgrn2_mixer(h_ctx, h_lat, w_in, lower_bound, onorm_g, w_out, ctx_out):
    def heads(t):
        b_, l_, _ = t.shape
        return t.reshape(b_, l_, HG_HEADS, -1).transpose(0, 2, 1, 3).astype(jnp.float32)

    def project(h):
        q, f_fw, f_bw, i, g = jnp.split(h @ w_in, 5, axis=-1)
        q = heads(jax.nn.silu(q)) * HG_DK ** -0.5

        def forget(f_raw):
            f = lower_bound + (1.0 - lower_bound) * jax.nn.sigmoid(f_raw.astype(jnp.float32))
            return heads(1.0 - f), heads(jnp.log(f))
        return q, forget(f_fw), forget(f_bw), heads(i), g

    def flip(t):
        return jnp.flip(t, axis=2)

    def scan_both(proj, s_fw, s_bw):
        q, (k_fw, lf_fw), (k_bw, lf_bw), v, _ = proj
        o_fw, s_fw = gla_scan(q, k_fw, v, lf_fw, s_fw)
        o_bw, s_bw = gla_scan(flip(q), flip(k_bw), flip(v), flip(lf_bw), s_bw)
        return o_fw + flip(o_bw), s_fw, s_bw

    def readout(o, g):
        b_, _, l_, _ = o.shape
        o = rms_norm(o, onorm_g).transpose(0, 2, 1, 3).reshape(b_, l_, -1)
        return (o * jax.nn.silu(g.astype(jnp.float32))).astype(g.dtype) @ w_out

    p_ctx = project(h_ctx)
    p_lat = project(h_lat)
    s0 = jnp.zeros((h_ctx.shape[0], HG_HEADS, HG_DK, HG_DV), jnp.float32)
    o_ctx, s_fw, s_bw = scan_both(p_ctx, s0, s0)
    o_lat, _, _ = scan_both(p_lat, s_fw, s_bw)
    y_ctx = readout(o_ctx, p_ctx[4]) if ctx_out else None
    return y_ctx, readout(o_lat, p_lat[4])


def short_conv(z, w, b, rows, row_len):
    bsz, _, ch = z.shape
    pad = HY_SHORT // 2
    zp = jnp.pad(z.reshape(bsz, rows, row_len, ch), ((0, 0), (0, 0), (pad, pad), (0, 0)))
    y = sum(zp[:, :, j:j + row_len] * w[j] for j in range(HY_SHORT)) + b
    return y.reshape(bsz, rows * row_len, ch)


def hyena_filter(length, w1, b1, freq, w2, b2, w3, b3, w4):
    t = (jnp.arange(length, dtype=jnp.float32) / length)[:, None]
    bands = jnp.linspace(1e-4, HY_BANDS - 1, HY_BANDS, dtype=jnp.float32)
    ang = 2.0 * math.pi * t * bands
    z = jnp.concatenate([t, jnp.cos(ang), -jnp.sin(ang)], axis=-1)
    a = jnp.sin(freq * (z @ w1 + b1))
    a = jnp.sin(freq * (a @ w2 + b2))
    a = jnp.sin(freq * (a @ w3 + b3))
    h = (a @ w4).astype(jnp.float32)
    h_fw, h_bw = jnp.split(h, 2, axis=-1)
    deltas = jnp.abs(jnp.linspace(math.log(HY_DECAY_PCT_MIN) / HY_DECAY_TARGET,
                                  math.log(HY_DECAY_PCT_MAX) / HY_DECAY_TARGET,
                                  h_fw.shape[-1], dtype=jnp.float32))
    decay = jnp.exp(-t * deltas)
    return h_fw * decay, h_bw * decay


def bidir_long_conv(u, h_fw, h_bw, skip):
    length, ch = h_fw.shape
    filt = jnp.concatenate([h_fw, jnp.zeros((1, ch), h_fw.dtype), h_bw[:0:-1]], axis=0)
    filt = filt / jnp.sum(jnp.abs(filt), axis=0, keepdims=True)
    uf = u.astype(jnp.float32)
    spec = jnp.fft.rfft(uf, n=2 * length, axis=1) * jnp.fft.rfft(filt, axis=0)[None]
    y = jnp.fft.irfft(spec, n=2 * length, axis=1)[:, :length]
    return (y + uf * skip.astype(jnp.float32)).astype(u.dtype)


def hyena_mixer(h, rows, row_len, w_in, b_in, short_w, short_b, f_w1, f_b1, f_freq,
                f_w2, f_b2, f_w3, f_b3, f_w4, skip, w_out, b_out):
    z = short_conv(h @ w_in + b_in, short_w, short_b, rows, row_len)
    x0, x1, v = jnp.split(z, 3, axis=-1)
    h_fw, h_bw = hyena_filter(h.shape[1], f_w1, f_b1, f_freq, f_w2, f_b2, f_w3, f_b3, f_w4)
    v = bidir_long_conv(v * x1, h_fw, h_bw, skip)
    return (v * x0) @ w_out + b_out


def moe_ffn(h, w_router, b_router, w_gu, b_gu, w_down, b_down):
    logits = (h @ w_router + b_router).astype(jnp.float32)
    top_logit, top_idx = lax.top_k(logits, TOP_K)
    top_w = jax.nn.softmax(top_logit, axis=-1)
    combine = jnp.sum(jax.nn.one_hot(top_idx, N_EXPERTS, dtype=jnp.float32) * top_w[..., None], axis=1)
    out = jnp.zeros(h.shape, jnp.float32)
    for e in range(N_EXPERTS):
        gu = h @ w_gu[e] + b_gu[e]
        gate = jnp.minimum(gu[:, :D_EXPERT], SWIGLU_LIMIT)
        up = jnp.clip(gu[:, D_EXPERT:], -SWIGLU_LIMIT, SWIGLU_LIMIT)
        act = (up + 1) * gate * jax.nn.sigmoid(SWIGLU_ALPHA * gate)
        out = out + combine[:, e:e + 1] * (act @ w_down[e] + b_down[e])
    return out.astype(h.dtype)


def setup_inputs(seed: int = 0) -> dict:
    key = jax.random.key(seed)
    keys = iter(jax.random.split(key, 40))

    def nrm(shape, scale):
        return jax.random.normal(next(keys), shape, jnp.float32) * scale

    d, f, fw = D_MODEL, D_EXPERT, HY_FILTER_WIDTH
    return {
        'x': nrm((BATCH, SEQ, d), 1.0),
        'c': nrm((BATCH, d), 1.0),
        'ctx': nrm((BATCH, CTX_LEN, d), 1.0),
        'c_ctx': nrm((d,), 1.0),
        'ada_w': nrm((DEPTH, d, N_MOD * d), 0.5 * d ** -0.5),
        'ada_b': nrm((DEPTH, N_MOD * d), 0.02),
        'norm_mix_g': 1.0 + nrm((DEPTH, d), 0.02),
        'norm_ffn_g': 1.0 + nrm((DEPTH, d), 0.02),
        'final_norm_g': 1.0 + nrm((d,), 0.02),
        'hg_w_in': nrm((N_HGRN, d, 5 * d), d ** -0.5),
        'hg_lb': 1.0 + nrm((N_HGRN, d), 0.1),
        'hg_onorm_g': 1.0 + nrm((N_HGRN, HG_DV), 0.02),
        'hg_w_out': nrm((N_HGRN, d, d), d ** -0.5),
        'hy_w_in': nrm((N_HYENA, d, 3 * d), d ** -0.5),
        'hy_b_in': nrm((N_HYENA, 3 * d), 0.02),
        'hy_short_w': nrm((N_HYENA, HY_SHORT, 3 * d), HY_SHORT ** -0.5),
        'hy_short_b': nrm((N_HYENA, 3 * d), 0.02),
        'hy_f_w1': nrm((N_HYENA, HY_EMB, fw), HY_EMB ** -0.5),
        'hy_f_b1': nrm((N_HYENA, fw), 0.02),
        'hy_f_freq': 1.0 + nrm((N_HYENA, fw), 0.02),
        'hy_f_w2': nrm((N_HYENA, fw, fw), fw ** -0.5),
        'hy_f_b2': nrm((N_HYENA, fw), 0.02),
        'hy_f_w3': nrm((N_HYENA, fw, fw), fw ** -0.5),
        'hy_f_b3': nrm((N_HYENA, fw), 0.02),
        'hy_f_w4': nrm((N_HYENA, fw, 2 * d), fw ** -0.5),
        'hy_skip': nrm((N_HYENA, d), 0.5),
        'hy_w_out': nrm((N_HYENA, d, d), d ** -0.5),
        'hy_b_out': nrm((N_HYENA, d), 0.02),
        'moe_w_router': nrm((DEPTH, d, N_EXPERTS), d ** -0.5),
        'moe_b_router': nrm((DEPTH, N_EXPERTS), 0.01),
        'moe_w_gu': nrm((DEPTH, N_EXPERTS, d, 2 * f), d ** -0.5),
        'moe_b_gu': nrm((DEPTH, N_EXPERTS, 2 * f), 0.02),
        'moe_w_down': nrm((DEPTH, N_EXPERTS, f, d), f ** -0.5),
        'moe_b_down': nrm((DEPTH, N_EXPERTS, d), 0.02),
    }


def reference(x, c, ctx, c_ctx, ada_w, ada_b, norm_mix_g, norm_ffn_g, final_norm_g,
              hg_w_in, hg_lb, hg_onorm_g, hg_w_out,
              hy_w_in, hy_b_in, hy_short_w, hy_short_b, hy_f_w1, hy_f_b1, hy_f_freq,
              hy_f_w2, hy_f_b2, hy_f_w3, hy_f_b3, hy_f_w4, hy_skip, hy_w_out, hy_b_out,
              moe_w_router, moe_b_router, moe_w_gu, moe_b_gu, moe_w_down, moe_b_down):
    bsz, seq, d = x.shape
    rows = seq // GRID_W
    ctx_len = ctx.shape[1]
    n_lat = bsz * seq
    lb_soft = jax.nn.softmax(hg_lb.astype(jnp.float32), axis=0)
    lower_bounds = jnp.cumsum(lb_soft, axis=0) - lb_soft[0]
    ctx_needed = [any(l % N_MIXERS == 0 for l in range(layer + 1, DEPTH)) for layer in range(DEPTH)]
    xc = ctx
    for layer in range(DEPTH):
        j = layer // N_MIXERS
        keep_ctx = ctx_needed[layer]
        is_hgrn = layer % N_MIXERS == 0
        sh1, sc1, g1, sh2, sc2, g2 = ada_mod(c, ada_w[layer], ada_b[layer])
        h_lat = modulate(rms_norm(x, norm_mix_g[layer]), sh1, sc1)
        if is_hgrn or keep_ctx:
            csh1, csc1, cg1, csh2, csc2, cg2 = ada_mod(c_ctx, ada_w[layer], ada_b[layer])
            h_ctx = modulate(rms_norm(xc, norm_mix_g[layer]), csh1, csc1)
        if is_hgrn:
            y_ctx, y_lat = hgrn2_mixer(h_ctx, h_lat, hg_w_in[j], lower_bounds[j], hg_onorm_g[j], hg_w_out[j], keep_ctx)
        else:
            hy = (hy_w_in[j], hy_b_in[j], hy_short_w[j], hy_short_b[j], hy_f_w1[j], hy_f_b1[j], hy_f_freq[j],
                  hy_f_w2[j], hy_f_b2[j], hy_f_w3[j], hy_f_b3[j], hy_f_w4[j], hy_skip[j], hy_w_out[j], hy_b_out[j])
            y_lat = hyena_mixer(h_lat, rows, GRID_W, *hy)
            y_ctx = hyena_mixer(h_ctx, 1, ctx_len, *hy) if keep_ctx else None
        x = x + g1 * y_lat
        moe_p = (moe_w_router[layer], moe_b_router[layer], moe_w_gu[layer], moe_b_gu[layer],
                 moe_w_down[layer], moe_b_down[layer])
        h2 = modulate(rms_norm(x, norm_ffn_g[layer]), sh2, sc2)
        if keep_ctx:
            xc = xc + cg1 * y_ctx
            h2c = modulate(rms_norm(xc, norm_ffn_g[layer]), csh2, csc2)
            out = moe_ffn(jnp.concatenate([h2.reshape(n_lat, d), h2c.reshape(-1, d)], axis=0), *moe_p)
            x = x + g2 * out[:n_lat].reshape(x.shape)
            xc = xc + cg2 * out[n_lat:].reshape(xc.shape)
        else:
            x = x + g2 * moe_ffn(h2.reshape(n_lat, d), *moe_p).reshape(x.shape)
    return rms_norm(x, final_norm_g)
```

```python
import functools
import math

import numpy as np
import jax
import jax.numpy as jnp
from jax import lax
from jax.experimental import pallas as pl
from jax.experimental.pallas import tpu as pltpu

F32 = jnp.float32
BF16 = jnp.bfloat16

EPS = 1e-6
N_MOD = 6
LANES = 128
GRID_W = 64
HG_DK = 128
GLA_CHUNK = 128
GLA_ROWS = 256
GLA_HEADS = 2
HY_BANDS = 16
HY_DECAY_PCT_MIN = 0.3
HY_DECAY_PCT_MAX = 1.5
HY_DECAY_TARGET = 1e-2
N_EXPERTS = 32
TOP_K = 4
SWIGLU_ALPHA = 1.702
SWIGLU_LIMIT = 7.0
MOE_TILE = 512
VMEM_LIMIT = 56 * 1024 * 1024


def _params(sem, vmem=VMEM_LIMIT):
    return pltpu.CompilerParams(dimension_semantics=sem, vmem_limit_bytes=vmem)


def _sigmoid(x):
    return 1.0 / (1.0 + jnp.exp(-x))


def _dot(a, b):
    return jnp.dot(a, b, preferred_element_type=F32)


def _dot_nt(a, b):
    return lax.dot_general(a, b, (((1,), (1,)), ((), ())), preferred_element_type=F32)


def _split(a):
    hi = a.astype(BF16)
    lo = (a - hi.astype(F32)).astype(BF16)
    return hi, lo


def _dot3(a, b):
    (ah, al), (bh, bl) = a, b
    return _dot(ah, bh) + _dot(ah, bl) + _dot(al, bh)


def _rms_mod(x, g, shift, scale):
    ms = jnp.mean(x * x, axis=-1, keepdims=True)
    return x * lax.rsqrt(ms + EPS) * g * (1.0 + scale) + shift


def _ada_kernel(c_ref, w_ref, b_ref, o_ref):
    c = c_ref[...]
    o_ref[0] = _dot(c * _sigmoid(c), w_ref[0]) + b_ref[0]


def _ada_all(cond8, ada_w, ada_b):
    depth, d, nd = ada_w.shape
    tn = 1024
    return pl.pallas_call(
        _ada_kernel,
        grid=(depth, nd // tn),
        in_specs=[pl.BlockSpec((8, d), lambda l, j: (0, 0)),
                  pl.BlockSpec((1, d, tn), lambda l, j: (l, 0, j)),
                  pl.BlockSpec((1, 1, tn), lambda l, j: (l, 0, j))],
        out_specs=pl.BlockSpec((1, 8, tn), lambda l, j: (l, 0, j)),
        out_shape=jax.ShapeDtypeStruct((depth, 8, nd), F32),
        compiler_params=_params(("arbitrary", "arbitrary")),
        name="ada_mod",
    )(cond8, ada_w, ada_b.reshape(depth, 1, nd))


def _hg_in_kernel(x_ref, mod_ref, ng_ref, lb_ref, wq, wf, wb, wi, wg,
                  q_o, kf_o, lff_o, kb_o, lfb_o, v_o, g_o, h_scr):
    @pl.when(pl.program_id(1) == 0)
    def _():
        m = mod_ref[0]
        h_scr[...] = _rms_mod(x_ref[...], ng_ref[...], m[0:1], m[1:2]).astype(BF16)

    h = h_scr[...]
    q = _dot(h, wq[...])
    q_o[...] = q * _sigmoid(q) * (HG_DK ** -0.5)
    lb = lb_ref[...]
    for w, k_o, lf_o in ((wf, kf_o, lff_o), (wb, kb_o, lfb_o)):
        f = lb + (1.0 - lb) * _sigmoid(_dot(h, w[...]))
        k_o[...] = 1.0 - f
        lf_o[...] = jnp.log(f)
    v_o[...] = _dot(h, wi[...])
    g_o[...] = _dot(h, wg[...])


def _hg_in(x, mod, ng, lb, w_bf, seq, n_rows):
    d = x.shape[1]
    tm, tn = 512, 256
    nj = d // tn
    grp = lambda i, j: (jnp.minimum((i * tm) // seq, 2), 0, 0)
    wspec = lambda s: pl.BlockSpec((d, tn), lambda i, j, s=s: (0, s * nj + j))
    ospec = pl.BlockSpec((tm, tn), lambda i, j: (i, j))
    osh = jax.ShapeDtypeStruct((n_rows, d), F32)
    return pl.pallas_call(
        _hg_in_kernel,
        grid=(n_rows // tm, nj),
        in_specs=[pl.BlockSpec((tm, d), lambda i, j: (i, 0)),
                  pl.BlockSpec((1, 8, d), grp),
                  pl.BlockSpec((1, d), lambda i, j: (0, 0)),
                  pl.BlockSpec((1, tn), lambda i, j: (0, j))] + [wspec(s) for s in range(5)],
        out_specs=[ospec] * 7,
        out_shape=[osh] * 7,
        scratch_shapes=[pltpu.VMEM((tm, d), BF16)],
        compiler_params=_params(("arbitrary", "arbitrary")),
        name="hgrn_in_proj",
    )(x, mod, ng, lb, w_bf, w_bf, w_bf, w_bf, w_bf)


def _level_map(c, reverse):
    i = np.arange(c)[:, None]
    j = np.arange(c)[None, :]
    x = i ^ j
    lvl = np.where(x > 0, np.floor(np.log2(np.maximum(x, 1))), -1).astype(np.int32)
    side = (i < j) if reverse else (i > j)
    return np.where(side, lvl, -1).astype(np.int32)


def _cumsum_rows(x, row, c, reverse):
    sh = 1
    while sh < c:
        if reverse:
            x = x + jnp.where(row < c - sh, pltpu.roll(x, c - sh, axis=0), 0.0)
        else:
            x = x + jnp.where(row >= sh, pltpu.roll(x, sh, axis=0), 0.0)
        sh *= 2
    return x


def _gla_chunk(q, k, v, lf, st, lvl, row, reverse):
    c, wd = q.shape
    nh = wd // HG_DK
    b = _cumsum_rows(lf, row, c, reverse)
    lf_up = pltpu.roll(lf, c - 1, axis=0)
    lf_dn = pltpu.roll(lf, 1, axis=0)
    att = [jnp.zeros((c, c), F32) for _ in range(nh)]
    p = 0
    while (1 << p) < c:
        s = 1 << p
        is_query = ((row & s) == 0) if reverse else ((row & s) != 0)
        if s == 1:
            a = jnp.where(is_query, lf, 0.0)
        elif s == 2:
            m = row & 3
            if reverse:
                a = jnp.where(m == 0, lf + lf_up, jnp.where(m == 1, lf, jnp.where(m == 2, 0.0, lf_dn)))
            else:
                a = jnp.where(m == 0, lf_up, jnp.where(m == 1, 0.0, jnp.where(m == 2, lf, lf + lf_dn)))
        else:
            nb = c // (2 * s)
            r = s if reverse else s - 1
            b3 = b.reshape(nb, 2 * s, wd)
            br = jnp.broadcast_to(b3[:, r:r + 1, :], (nb, 2 * s, wd)).reshape(c, wd)
            a = -jnp.abs(b - br)
        z = (jnp.where(is_query, q, k) * jnp.exp(a)).astype(BF16)
        for h in range(nh):
            zh = z[:, h * HG_DK:(h + 1) * HG_DK]
            att[h] = jnp.where(lvl == p, _dot_nt(zh, zh), att[h])
        p += 1
    tot = b[0:1] if reverse else b[c - 1:c]
    qe = (q * jnp.exp(b)).astype(BF16)
    kd = (k * jnp.exp(tot - b)).astype(BF16)
    dec = jnp.exp(tot)
    outs, new_st = [], []
    for h in range(nh):
        cs = slice(h * HG_DK, (h + 1) * HG_DK)
        vh = v[:, cs]
        dsum = jnp.sum(q[:, cs] * k[:, cs], axis=-1, keepdims=True)
        o = _dot(att[h].astype(BF16), vh.astype(BF16)) + dsum * vh
        o = o + _dot_nt(qe[:, cs], st[h].astype(BF16))
        outs.append(o)
        new_st.append(st[h] * dec[:, cs] + _dot(vh.T.astype(BF16), kd[:, cs]))
    return jnp.concatenate(outs, axis=1), new_st


def _gla_kernel(*refs, reverse, gated):
    if gated:
        q_ref, k_ref, v_ref, lf_ref, lvl_ref, ofw_ref, g_ref, on_ref, y_ref, s_scr = refs
    else:
        q_ref, k_ref, v_ref, lf_ref, lvl_ref, y_ref, s_scr = refs
    c = GLA_CHUNK
    rows, wd = q_ref.shape
    nh = wd // HG_DK
    nchunk = rows // c

    @pl.when(pl.program_id(2) == 0)
    def _():
        s_scr[...] = jnp.zeros_like(s_scr)

    lvl = lvl_ref[...]
    row = lax.broadcasted_iota(jnp.int32, (c, wd), 0)

    def body(i, carry):
        ci = (nchunk - 1 - i) if reverse else i
        rs = pl.ds(pl.multiple_of(ci * c, c), c)
        st = [s_scr[h] for h in range(nh)]
        o, st = _gla_chunk(q_ref[rs, :], k_ref[rs, :], v_ref[rs, :], lf_ref[rs, :], st, lvl, row, reverse)
        for h in range(nh):
            s_scr[h] = st[h]
        if gated:
            o = o + ofw_ref[rs, :]
            g = g_ref[rs, :]
            on = on_ref[...]
            ys = []
            for h in range(nh):
                oh = o[:, h * HG_DK:(h + 1) * HG_DK]
                ms = jnp.mean(oh * oh, axis=-1, keepdims=True)
                ys.append(oh * lax.rsqrt(ms + EPS) * on)
            y = jnp.concatenate(ys, axis=1) * (g * _sigmoid(g))
            y_ref[rs, :] = y.astype(y_ref.dtype)
        else:
            y_ref[rs, :] = o
        return carry

    lax.fori_loop(0, nchunk, body, 0)


def _gla_pass(q, k, v, lf, seq, ctx_len, reverse, extra=None):
    n_all, d = q.shape
    rows, wd = GLA_ROWS, GLA_HEADS * HG_DK
    nlat, nctx = seq // rows, ctx_len // rows
    ctx_base = 2 * nlat

    def rmap(b, hh, t):
        tc = (nctx - 1 - t) if reverse else t
        tl = (nlat - 1 - (t - nctx)) if reverse else (t - nctx)
        return (jnp.where(t < nctx, ctx_base + b * nctx + tc, b * nlat + tl), hh)

    blk = pl.BlockSpec((rows, wd), rmap)
    lvl = jnp.asarray(_level_map(GLA_CHUNK, reverse))
    in_specs = [blk, blk, blk, blk, pl.BlockSpec((GLA_CHUNK, GLA_CHUNK), lambda b, hh, t: (0, 0))]
    args = [q, k, v, lf, lvl]
    gated = extra is not None
    if gated:
        in_specs += [blk, blk, pl.BlockSpec((1, HG_DK), lambda b, hh, t: (0, 0))]
        args += list(extra)
    return pl.pallas_call(
        functools.partial(_gla_kernel, reverse=reverse, gated=gated),
        grid=(2, d // wd, nlat + nctx),
        in_specs=in_specs,
        out_specs=blk,
        out_shape=jax.ShapeDtypeStruct((n_all, d), BF16 if gated else F32),
        scratch_shapes=[pltpu.VMEM((GLA_HEADS, HG_DK, HG_DK), F32)],
        compiler_params=_params(("arbitrary", "arbitrary", "arbitrary")),
        name="gla_scan_bwd" if reverse else "gla_scan_fwd",
    )(*args)


def _out_kernel(y_ref, w_ref, b_ref, x_ref, mod_ref, ng_ref, wr_ref, br_ref, xo_ref, h2_ref, lg_ref):
    m = mod_ref[0]
    xn = x_ref[...] + m[2:3] * (_dot(y_ref[...], w_ref[...]) + b_ref[...])
    xo_ref[...] = xn
    h2 = _rms_mod(xn, ng_ref[...], m[3:4], m[4:5])
    h2_ref[...] = h2
    lg_ref[...] = _dot3(_split(h2), _split(wr_ref[...])) + br_ref[...]


def _out_proj(y, w_bf, bias, x, mod, ng, wr_pad, br_pad, seq, n_rows):
    d = x.shape[1]
    tm = 256
    row = lambda i: (i, 0)
    const = lambda i: (0, 0)
    return pl.pallas_call(
        _out_kernel,
        grid=(n_rows // tm,),
        in_specs=[pl.BlockSpec((tm, d), row),
                  pl.BlockSpec((d, d), const),
                  pl.BlockSpec((1, d), const),
                  pl.BlockSpec((tm, d), row),
                  pl.BlockSpec((1, 8, d), lambda i: (jnp.minimum((i * tm) // seq, 2), 0, 0)),
                  pl.BlockSpec((1, d), const),
                  pl.BlockSpec((d, LANES), const),
                  pl.BlockSpec((1, LANES), const)],
        out_specs=[pl.BlockSpec((tm, d), row), pl.BlockSpec((tm, d), row), pl.BlockSpec((tm, LANES), row)],
        out_shape=[jax.ShapeDtypeStruct((n_rows, d), F32), jax.ShapeDtypeStruct((n_rows, d), F32),
                   jax.ShapeDtypeStruct((n_rows, LANES), F32)],
        compiler_params=_params(("arbitrary",)),
        name="out_proj",
    )(y, w_bf, bias, x, mod, ng, wr_pad, br_pad)


def _topk_kernel(lg_ref, idx_ref, w_ref):
    l = lg_ref[...]
    lane = lax.broadcasted_iota(jnp.int32, l.shape, 1).astype(F32)
    l = jnp.where(lane < N_EXPERTS, l, -jnp.inf)
    idx_out = jnp.zeros(l.shape, F32)
    vals = []
    for r in range(TOP_K):
        m = jnp.max(l, axis=-1, keepdims=True)
        sel = jnp.min(jnp.where(l == m, lane, float(LANES)), axis=-1, keepdims=True)
        idx_out = jnp.where(lane == r, sel, idx_out)
        l = jnp.where(lane == sel, -jnp.inf, l)
        vals.append(m)
    es = [jnp.exp(v - vals[0]) for v in vals]
    tot = es[0] + es[1] + es[2] + es[3]
    w_out = jnp.zeros(l.shape, F32)
    for r in range(TOP_K):
        w_out = jnp.where(lane == r, es[r] / tot, w_out)
    idx_ref[...] = idx_out.astype(jnp.int32)
    w_ref[...] = w_out


def _topk(logits):
    n = logits.shape[0]
    tm = 512
    spec = pl.BlockSpec((tm, LANES), lambda i: (i, 0))
    return pl.pallas_call(
        _topk_kernel,
        grid=(n // tm,),
        in_specs=[spec],
        out_specs=[spec, spec],
        out_shape=[jax.ShapeDtypeStruct((n, LANES), jnp.int32), jax.ShapeDtypeStruct((n, LANES), F32)],
        compiler_params=_params(("arbitrary",)),
        name="router_topk",
    )(logits)


def _rank_kernel(idx_ref, rank_ref, cnt_ref, run_scr):
    @pl.when(pl.program_id(0) == 0)
    def _():
        run_scr[...] = jnp.zeros_like(run_scr)

    idx = idx_ref[...]
    tm = idx.shape[0]
    lane = lax.broadcasted_iota(jnp.int32, idx.shape, 1)
    onehots = [jnp.where(lane == idx[:, j:j + 1], 1.0, 0.0) for j in range(TOP_K)]
    esum = onehots[0] + onehots[1] + onehots[2] + onehots[3]
    ri = lax.broadcasted_iota(jnp.int32, (tm, tm), 0)
    ci = lax.broadcasted_iota(jnp.int32, (tm, tm), 1)
    before = jnp.where(ci < ri, 1.0, 0.0).astype(BF16)
    base = run_scr[...] + _dot(before, esum.astype(BF16))
    rank = jnp.zeros(idx.shape, F32)
    for j in range(TOP_K):
        rank = jnp.where(lane == j, jnp.sum(onehots[j] * base, axis=-1, keepdims=True), rank)
    rank_ref[...] = rank.astype(jnp.int32)
    run_scr[...] = run_scr[...] + jnp.sum(esum, axis=0, keepdims=True)
    cnt_ref[...] = run_scr[...]


def _rank(idx):
    n = idx.shape[0]
    tm = 512
    spec = pl.BlockSpec((tm, LANES), lambda i: (i, 0))
    return pl.pallas_call(
        _rank_kernel,
        grid=(n // tm,),
        in_specs=[spec],
        out_specs=[spec, pl.BlockSpec((1, LANES), lambda i: (0, 0))],
        out_shape=[jax.ShapeDtypeStruct((n, LANES), jnp.int32), jax.ShapeDtypeStruct((1, LANES), F32)],
        scratch_shapes=[pltpu.VMEM((1, LANES), F32)],
        compiler_params=_params(("arbitrary",)),
        name="router_rank",
    )(idx)


GATHER_ROWS = MOE_TILE
COMBINE_ROWS = 256


def _row_copy(src_hbm, s, dst, r, sem):
    return pltpu.make_async_copy(src_hbm.at[pl.ds(s, 1)], dst.at[pl.ds(r, 1)], sem)


def _gather_kernel(src_ref, h_hbm, xs_hbm, sem):
    base = pl.program_id(0) * GATHER_ROWS

    def start(r, c):
        _row_copy(h_hbm, src_ref[r], xs_hbm, base + r, sem).start()
        return c

    def wait(r, c):
        _row_copy(h_hbm, src_ref[r], xs_hbm, base + r, sem).wait()
        return c

    lax.fori_loop(0, GATHER_ROWS, start, 0)
    lax.fori_loop(0, GATHER_ROWS, wait, 0)


def _gather_rows(h, src, p_max):
    d = h.shape[1]
    return pl.pallas_call(
        _gather_kernel,
        grid=(p_max // GATHER_ROWS,),
        in_specs=[pl.BlockSpec((GATHER_ROWS,), lambda i: (i,), memory_space=pltpu.SMEM),
                  pl.BlockSpec(memory_space=pl.ANY)],
        out_specs=pl.BlockSpec(memory_space=pl.ANY),
        out_shape=jax.ShapeDtypeStruct((p_max, d), F32),
        scratch_shapes=[pltpu.SemaphoreType.DMA(())],
        compiler_params=_params(("arbitrary",)),
        name="moe_dispatch",
    )(src, h)


def _expert_kernel(te_ref, nv_ref, xs_ref, wgu_ref, bgu_ref, wdn_ref, bdn_ref, ys_ref):
    t = pl.program_id(0)

    @pl.when(t < nv_ref[0])
    def _():
        f = wdn_ref.shape[1]
        gu = _dot(xs_ref[...].astype(BF16), wgu_ref[0]) + bgu_ref[0]
        gate = jnp.minimum(gu[:, :f], SWIGLU_LIMIT)
        up = jnp.clip(gu[:, f:], -SWIGLU_LIMIT, SWIGLU_LIMIT)
        act = (up + 1.0) * gate * _sigmoid(SWIGLU_ALPHA * gate)
        ys_ref[...] = _dot(act.astype(BF16), wdn_ref[0]) + bdn_ref[0]

    @pl.when(t >= nv_ref[0])
    def _():
        ys_ref[...] = jnp.zeros_like(ys_ref)


def _experts(xs, tile_expert, n_valid, wgu_bf, bgu, wdn_bf, bdn):
    p_max, d = xs.shape
    ne, _, f2 = wgu_bf.shape
    f = f2 // 2
    tm = MOE_TILE
    return pl.pallas_call(
        _expert_kernel,
        grid_spec=pltpu.PrefetchScalarGridSpec(
            num_scalar_prefetch=2,
            grid=(p_max // tm,),
            in_specs=[pl.BlockSpec((tm, d), lambda t, te, nv: (jnp.minimum(t, nv[0] - 1), 0)),
                      pl.BlockSpec((1, d, f2), lambda t, te, nv: (te[t], 0, 0)),
                      pl.BlockSpec((1, 1, f2), lambda t, te, nv: (te[t], 0, 0)),
                      pl.BlockSpec((1, f, d), lambda t, te, nv: (te[t], 0, 0)),
                      pl.BlockSpec((1, 1, d), lambda t, te, nv: (te[t], 0, 0))],
            out_specs=pl.BlockSpec((tm, d), lambda t, te, nv: (t, 0))),
        out_shape=jax.ShapeDtypeStruct((p_max, d), F32),
        compiler_params=_params(("arbitrary",)),
        name="moe_experts",
    )(tile_expert, n_valid, xs, wgu_bf, bgu.reshape(ne, 1, f2), wdn_bf, bdn.reshape(ne, 1, d))


def _combine_kernel(dest_ref, x_ref, w_ref, mod_ref, fg_ref, ys_hbm, o_ref, buf, sem, *, final):
    tm = x_ref.shape[0]

    def start(r, c):
        for j in range(TOP_K):
            _row_copy(ys_hbm, dest_ref[r * TOP_K + j], buf.at[j], r, sem).start()
        return c

    def wait(r, c):
        for j in range(TOP_K):
            _row_copy(ys_hbm, dest_ref[r * TOP_K + j], buf.at[j], r, sem).wait()
        return c

    lax.fori_loop(0, tm, start, 0)
    lax.fori_loop(0, tm, wait, 0)
    w = w_ref[...]
    acc = w[:, 0:1] * buf[0]
    for j in range(1, TOP_K):
        acc = acc + w[:, j:j + 1] * buf[j]
    out = x_ref[...] + mod_ref[0][5:6] * acc
    if final:
        ms = jnp.mean(out * out, axis=-1, keepdims=True)
        out = out * lax.rsqrt(ms + EPS) * fg_ref[...]
    o_ref[...] = out


def _combine(dest_flat, x, wts, mod, final_g, ys, seq, n_rows, final):
    d = x.shape[1]
    tm = COMBINE_ROWS
    row = lambda i: (i, 0)
    return pl.pallas_call(
        functools.partial(_combine_kernel, final=final),
        grid=(n_rows // tm,),
        in_specs=[pl.BlockSpec((tm * TOP_K,), lambda i: (i,), memory_space=pltpu.SMEM),
                  pl.BlockSpec((tm, d), row),
                  pl.BlockSpec((tm, LANES), row),
                  pl.BlockSpec((1, 8, d), lambda i: (jnp.minimum((i * tm) // seq, 2), 0, 0)),
                  pl.BlockSpec((1, d), lambda i: (0, 0)),
                  pl.BlockSpec(memory_space=pl.ANY)],
        out_specs=pl.BlockSpec((tm, d), row),
        out_shape=jax.ShapeDtypeStruct((n_rows, d), F32),
        scratch_shapes=[pltpu.VMEM((TOP_K, tm, d), F32), pltpu.SemaphoreType.DMA(())],
        compiler_params=_params(("arbitrary",)),
        name="moe_combine",
    )(dest_flat, x, wts, mod, final_g, ys)


def _moe(h2, logits, x, mod, final_g, wgu_bf, bgu, wdn_bf, bdn, seq, final):
    n = h2.shape[0]
    tm = MOE_TILE
    idx, wts = _topk(logits)
    rank, counts = _rank(idx)
    cnt = counts[0, :N_EXPERTS].astype(jnp.int32)
    tiles_e = (cnt + tm - 1) // tm
    tile_end = jnp.cumsum(tiles_e)
    offs = (tile_end - tiles_e) * tm
    idx4, rank4 = idx[:, :TOP_K], rank[:, :TOP_K]
    dest = (offs[idx4] + rank4).reshape(-1)
    p_max = (n * TOP_K // tm + N_EXPERTS) * tm
    n_tiles = p_max // tm
    n_valid = tile_end[-1:]
    tile_expert = jnp.searchsorted(tile_end, jnp.minimum(jnp.arange(n_tiles), n_valid[0] - 1), side="right")
    tile_expert = jnp.minimum(tile_expert, N_EXPERTS - 1).astype(jnp.int32)
    tok = jnp.repeat(jnp.arange(n, dtype=jnp.int32), TOP_K)
    src = jnp.zeros((p_max,), jnp.int32).at[dest].set(tok)
    xs = _gather_rows(h2, src, p_max)
    ys = _experts(xs, tile_expert, n_valid.astype(jnp.int32), wgu_bf, bgu, wdn_bf, bdn)
    return _combine(dest, x, wts, mod, final_g, ys, seq, n, final)


def _hy_in_kernel(x_ref, mod_ref, ng_ref, w0, w1, w2, bin_ref, sw_ref, sb_ref, u_o, x0_o, h_scr, *, n_lat_tiles, ctx_len):
    i = pl.program_id(0)

    @pl.when(pl.program_id(1) == 0)
    def _():
        m = mod_ref[0]
        h_scr[...] = _rms_mod(x_ref[...], ng_ref[...], m[0:1], m[1:2]).astype(BF16)

    h = h_scr[...]
    tm, tn = u_o.shape
    row = lax.broadcasted_iota(jnp.int32, (tm, tn), 0)
    seg = jnp.where(i >= n_lat_tiles, ctx_len, GRID_W)
    pos = row & (seg - 1)
    first = pos == 0
    last = pos == seg - 1
    bin_ = bin_ref[...]
    sw = sw_ref[...]
    sb = sb_ref[...]
    outs = []
    for s, w in enumerate((w0, w1, w2)):
        z = _dot(h, w[...]) + bin_[s:s + 1]
        zp = jnp.where(first, 0.0, pltpu.roll(z, 1, axis=0))
        zn = jnp.where(last, 0.0, pltpu.roll(z, tm - 1, axis=0))
        outs.append(sw[0, s:s + 1] * zp + sw[1, s:s + 1] * z + sw[2, s:s + 1] * zn + sb[s:s + 1])
    x0_o[...] = outs[0]
    u_o[...] = outs[2] * outs[1]


def _hy_in(x, mod, ng, w_bf, b_in, short_w, short_b, seq, ctx_len, n_rows):
    d = x.shape[1]
    tm, tn = 512, 256
    nj = d // tn
    wspec = lambda s: pl.BlockSpec((d, tn), lambda i, j, s=s: (0, s * nj + j))
    ospec = pl.BlockSpec((tm, tn), lambda i, j: (i, j))
    osh = jax.ShapeDtypeStruct((n_rows, d), F32)
    return pl.pallas_call(
        functools.partial(_hy_in_kernel, n_lat_tiles=2 * seq // tm, ctx_len=ctx_len),
        grid=(n_rows // tm, nj),
        in_specs=[pl.BlockSpec((tm, d), lambda i, j: (i, 0)),
                  pl.BlockSpec((1, 8, d), lambda i, j: (jnp.minimum((i * tm) // seq, 2), 0, 0)),
                  pl.BlockSpec((1, d), lambda i, j: (0, 0)),
                  wspec(0), wspec(1), wspec(2),
                  pl.BlockSpec((3, tn), lambda i, j: (0, j)),
                  pl.BlockSpec((3, 3, tn), lambda i, j: (0, 0, j)),
                  pl.BlockSpec((3, tn), lambda i, j: (0, j))],
        out_specs=[ospec, ospec],
        out_shape=[osh, osh],
        scratch_shapes=[pltpu.VMEM((tm, d), BF16)],
        compiler_params=_params(("arbitrary", "arbitrary")),
        name="hyena_in_proj",
    )(x, mod, ng, w_bf, w_bf, w_bf, b_in.reshape(3, d), short_w.reshape(3, 3, d), short_b.reshape(3, d))


def _filter_kernel(z_ref, w1, b1, fr, w2, b2, w3, b3, w4, dl_ref, hf_o, hb_o, nrm_o):
    i = pl.program_id(0)
    z = z_ref[...]
    freq = fr[...]
    a = jnp.sin(freq * (_dot(z, w1[...]) + b1[...]))
    a = jnp.sin(freq * (_dot(a, w2[...]) + b2[...]))
    a = jnp.sin(freq * (_dot(a, w3[...]) + b3[...]))
    h = _dot(a, w4[...])
    d = hf_o.shape[1]
    decay = jnp.exp(-z[:, 0:1] * dl_ref[...])
    hf = h[:, :d] * decay
    hb = h[:, d:] * decay
    hf_o[...] = hf
    hb_o[...] = hb
    row = lax.broadcasted_iota(jnp.int32, hb.shape, 0) + i * hb.shape[0]
    part = jnp.sum(jnp.abs(hf) + jnp.where(row > 0, jnp.abs(hb), 0.0), axis=0, keepdims=True)

    @pl.when(i == 0)
    def _():
        nrm_o[...] = jnp.zeros_like(nrm_o)

    nrm_o[...] = nrm_o[...] + part


def _pos_features(length):
    t = (np.arange(length, dtype=np.float64) / length)[:, None]
    bands = np.linspace(1e-4, HY_BANDS - 1, HY_BANDS)[None, :]
    ang = 2.0 * math.pi * t * bands
    return np.concatenate([t, np.cos(ang), -np.sin(ang)], axis=-1).astype(np.float32)


def _decay_rates(d):
    return np.abs(np.linspace(math.log(HY_DECAY_PCT_MIN) / HY_DECAY_TARGET,
                              math.log(HY_DECAY_PCT_MAX) / HY_DECAY_TARGET, d)).astype(np.float32)[None, :]


def _hyena_filter(length, w1, b1, freq, w2, b2, w3, b3, w4):
    d = w4.shape[1] // 2
    tr = min(length, 512)
    feats = _pos_features(length)
    z = jnp.asarray(np.pad(feats, ((0, 0), (0, LANES - feats.shape[1]))))
    const = lambda i: (0, 0)
    full = lambda a: pl.BlockSpec(a.shape, const)
    mat = lambda a: jnp.pad(a, ((0, LANES - a.shape[0]), (0, LANES - a.shape[1])))
    vec = lambda a: jnp.pad(a, (0, LANES - a.shape[0])).reshape(1, LANES)
    w4p = jnp.pad(w4, ((0, LANES - w4.shape[0]), (0, 0)))
    args = [z, mat(w1), vec(b1), vec(freq), mat(w2), vec(b2), mat(w3), vec(b3), w4p, jnp.asarray(_decay_rates(d))]
    return pl.pallas_call(
        _filter_kernel,
        grid=(length // tr,),
        in_specs=[pl.BlockSpec((tr, z.shape[1]), lambda i: (i, 0))] + [full(a) for a in args[1:]],
        out_specs=[pl.BlockSpec((tr, d), lambda i: (i, 0)), pl.BlockSpec((tr, d), lambda i: (i, 0)),
                   pl.BlockSpec((1, d), const)],
        out_shape=[jax.ShapeDtypeStruct((length, d), F32), jax.ShapeDtypeStruct((length, d), F32),
                   jax.ShapeDtypeStruct((1, d), F32)],
        compiler_params=_params(("arbitrary",)),
        name="hyena_filter",
    )(*args)


def _dft_mats(m, k, n):
    r = np.arange(m, dtype=np.int64)[:, None]
    c = np.arange(k, dtype=np.int64)[None, :]
    ang = 2.0 * math.pi * ((r * c) % n).astype(np.float64) / n
    return np.cos(ang).astype(np.float32), np.sin(ang).astype(np.float32)


def _cmul_mat(c, s, xr, xi, conj):
    xr = _split(xr)
    cr, sr = _dot3(c, xr), _dot3(s, xr)
    if xi is None:
        return cr, (sr if conj else -sr)
    xi = _split(xi)
    ci, si = _dot3(c, xi), _dot3(s, xi)
    if conj:
        return cr - si, ci + sr
    return cr + si, ci - sr


def _slab_fwd_kernel(*refs, has_imag, has_filter):
    refs = list(refs)
    fc, fs, xr_ref = refs[:3]
    xi_ref = refs[3] if has_imag else None
    rest = refs[3 + has_imag:]
    if has_filter:
        gr_ref, gi_ref, gn_ref, or_ref, oi_ref = rest
    else:
        or_ref, oi_ref = rest
    yr, yi = _cmul_mat(_split(fc[...]), _split(fs[...]), xr_ref[...], xi_ref[...] if has_imag else None, False)
    if has_filter:
        gn = 1.0 / gn_ref[...]
        gr, gi = gr_ref[...] * gn, gi_ref[...] * gn
        yr, yi = yr * gr - yi * gi, yr * gi + yi * gr
    or_ref[...] = yr
    oi_ref[...] = yi


def _slab_fwd(x2, k_in, parts, n_out, n, spectrum=None):
    cols = x2.shape[1]
    tc = min(cols, 2048 if n_out <= LANES else 512)
    fc, fs = _dft_mats(n_out, k_in, n)
    const = lambda c: (0, 0)
    in_specs = [pl.BlockSpec((n_out, k_in), const), pl.BlockSpec((n_out, k_in), const)]
    args = [jnp.asarray(fc), jnp.asarray(fs)]
    for part in parts:
        if part is not None:
            in_specs.append(pl.BlockSpec((k_in, tc), lambda c, part=part: (part, c)))
            args.append(x2)
    ospec = pl.BlockSpec((n_out, tc), lambda c: (0, c))
    if spectrum is not None:
        in_specs += [ospec, ospec, pl.BlockSpec((1, tc), lambda c: (0, c))]
        args += list(spectrum)
    osh = jax.ShapeDtypeStruct((n_out, cols), F32)
    return pl.pallas_call(
        functools.partial(_slab_fwd_kernel, has_imag=parts[1] is not None, has_filter=spectrum is not None),
        grid=(cols // tc,),
        in_specs=in_specs,
        out_specs=[ospec, ospec],
        out_shape=[osh, osh],
        compiler_params=_params(("arbitrary",)),
        name="dft_slab_fwd",
    )(*args)


def _mid_kernel(*refs, has_filter):
    if has_filter:
        fc, fs, twc_ref, tws_ref, ar_ref, ai_ref, gr_ref, gi_ref, gn_ref, or_ref, oi_ref = refs
    else:
        fc, fs, twc_ref, tws_ref, ar_ref, ai_ref, or_ref, oi_ref = refs
    td = ar_ref.shape[2]
    c, s = _split(fc[...]), _split(fs[...])
    twc = jnp.concatenate([twc_ref[0]] * (td // LANES), axis=1)
    tws = jnp.concatenate([tws_ref[0]] * (td // LANES), axis=1)
    ar, ai = ar_ref[0], ai_ref[0]
    pr, pi_ = ar * twc + ai * tws, ai * twc - ar * tws
    br, bi = _cmul_mat(c, s, pr, pi_, False)
    if not has_filter:
        or_ref[0] = br
        oi_ref[0] = bi
        return
    gn = 1.0 / gn_ref[...]
    gr, gi = gr_ref[0] * gn, gi_ref[0] * gn
    yr, yi = br * gr - bi * gi, br * gi + bi * gr
    qr, qi = _cmul_mat(c, s, yr, yi, True)
    or_ref[0] = qr * twc - qi * tws
    oi_ref[0] = qr * tws + qi * twc


def _mid_stage(a_re, a_im, n, spectrum=None):
    n1, _, d = a_re.shape
    td = 512
    fc, fs = _dft_mats(LANES, LANES, LANES)
    k1 = lax.broadcasted_iota(jnp.int32, (n1, LANES, LANES), 0)
    m2 = lax.broadcasted_iota(jnp.int32, (n1, LANES, LANES), 1)
    ang = ((k1 * m2) % n).astype(F32) * (2.0 * math.pi / n)
    twc, tws = jnp.cos(ang), jnp.sin(ang)
    const = lambda s, j: (0, 0)
    slab = pl.BlockSpec((1, LANES, td), lambda s, j: (s, 0, j))
    tw = pl.BlockSpec((1, LANES, LANES), lambda s, j: (s, 0, 0))
    in_specs = [pl.BlockSpec((LANES, LANES), const), pl.BlockSpec((LANES, LANES), const), tw, tw, slab, slab]
    args = [jnp.asarray(fc), jnp.asarray(fs), twc, tws, a_re, a_im]
    if spectrum is not None:
        in_specs += [slab, slab, pl.BlockSpec((1, td), lambda s, j: (0, j))]
        args += list(spectrum)
    osh = jax.ShapeDtypeStruct((n1, LANES, d), F32)
    return pl.pallas_call(
        functools.partial(_mid_kernel, has_filter=spectrum is not None),
        grid=(n1, d // td),
        in_specs=in_specs,
        out_specs=[slab, slab],
        out_shape=[osh, osh],
        compiler_params=_params(("arbitrary", "arbitrary")),
        name="dft_mid",
    )(*args)


def _slab_inv_kernel(fc, fs, pr_ref, pi_ref, u0_ref, u1_ref, a0_ref, a1_ref, skip_ref, o_ref, *, scale):
    yr, yi = _cmul_mat(_split(fc[...]), _split(fs[...]), pr_ref[...], pi_ref[...], True)
    skip = skip_ref[...]
    o_ref[0] = ((yr * scale + u0_ref[...] * skip) * a0_ref[...]).astype(o_ref.dtype)
    o_ref[1] = ((yi * scale + u1_ref[...] * skip) * a1_ref[...]).astype(o_ref.dtype)


def _slab_inv(p_re, p_im, u2, x02, k_out, parts, skip, n, scale):
    n_in, cols = p_re.shape
    d = skip.shape[1]
    tc = min(cols, d if n_in <= LANES else 512)
    fc, fs = _dft_mats(k_out, n_in, n)
    const = lambda c: (0, 0)
    pspec = pl.BlockSpec((n_in, tc), lambda c: (0, c))
    uspec = lambda part: pl.BlockSpec((k_out, tc), lambda c, part=part: (part, c))
    return pl.pallas_call(
        functools.partial(_slab_inv_kernel, scale=scale),
        grid=(cols // tc,),
        in_specs=[pl.BlockSpec((k_out, n_in), const), pl.BlockSpec((k_out, n_in), const),
                  pspec, pspec, uspec(parts[0]), uspec(parts[1]), uspec(parts[0]), uspec(parts[1]),
                  pl.BlockSpec((1, tc), lambda c: (0, c % (d // tc)))],
        out_specs=pl.BlockSpec((2, k_out, tc), lambda c: (0, 0, c)),
        out_shape=jax.ShapeDtypeStruct((2, k_out, cols), BF16),
        compiler_params=_params(("arbitrary",)),
        name="dft_slab_inv",
    )(jnp.asarray(fc), jnp.asarray(fs), p_re, p_im, u2, u2, x02, x02, skip)


def _long_conv(u, x0, row0, length, hf, hb, nrm, skip):
    d = u.shape[1]
    n = 2 * length
    filt = jnp.concatenate([hf, jnp.zeros((1, d), F32), hb[:0:-1]], axis=0)
    if n <= 4 * LANES:
        parts = (row0 // length, row0 // length + 1)
        g_re, g_im = _slab_fwd(filt, n, (0, None), n, n)
        y_re, y_im = _slab_fwd(u, length, parts, n, n, spectrum=(g_re, g_im, nrm))
        y = _slab_inv(y_re, y_im, u, x0, length, parts, skip, n, 1.0 / n)
        return y.reshape(2 * length, d)
    n1 = n // LANES
    cols = LANES * d
    k1 = n1 // 2
    parts = (row0 // length, row0 // length + 1)
    f_re, f_im = _slab_fwd(filt.reshape(n1, cols), n1, (0, None), n1, n1)
    g_re, g_im = _mid_stage(f_re.reshape(n1, LANES, d), f_im.reshape(n1, LANES, d), n)
    u2 = u.reshape(u.shape[0] // LANES, cols)
    a_re, a_im = _slab_fwd(u2, k1, parts, n1, n1)
    p_re, p_im = _mid_stage(a_re.reshape(n1, LANES, d), a_im.reshape(n1, LANES, d), n, spectrum=(g_re, g_im, nrm))
    y = _slab_inv(p_re.reshape(n1, cols), p_im.reshape(n1, cols), u2, x0.reshape(u2.shape), k1, parts, skip,
                  n1, 1.0 / n)
    return y.reshape(2 * length, d)


def kernel(x, c, ctx, c_ctx, ada_w, ada_b, norm_mix_g, norm_ffn_g, final_norm_g, hg_w_in, hg_lb, hg_onorm_g, hg_w_out, hy_w_in, hy_b_in, hy_short_w, hy_short_b, hy_f_w1, hy_f_b1, hy_f_freq, hy_f_w2, hy_f_b2, hy_f_w3, hy_f_b3, hy_f_w4, hy_skip, hy_w_out, hy_b_out, moe_w_router, moe_b_router, moe_w_gu, moe_b_gu, moe_w_down, moe_b_down):
    bsz, seq, d = x.shape
    ctx_len = ctx.shape[1]
    depth = ada_w.shape[0]
    n_lat = bsz * seq
    n_all = n_lat + bsz * ctx_len
    assert bsz == 2 and seq % MOE_TILE == 0 and ctx_len == GLA_ROWS

    cond8 = jnp.zeros((8, d), F32).at[0:2].set(c).at[2].set(c_ctx)
    mods = _ada_all(cond8, ada_w, ada_b).reshape(depth, 8, N_MOD, d)[:, :3]
    mods = jnp.pad(mods, ((0, 0), (0, 0), (0, 8 - N_MOD), (0, 0)))

    lb_soft = jax.nn.softmax(hg_lb.astype(F32), axis=0)
    lower_bounds = jnp.cumsum(lb_soft, axis=0) - lb_soft[0]

    xa = jnp.concatenate([x.reshape(n_lat, d), ctx.reshape(bsz * ctx_len, d)], axis=0)
    zero_bias = jnp.zeros((1, d), F32)
    final_g = final_norm_g.reshape(1, d)
    ctx_needed = [any(l % 2 == 0 for l in range(layer + 1, depth)) for layer in range(depth)]

    for layer in range(depth):
        j = layer // 2
        is_hgrn = layer % 2 == 0
        keep_ctx = ctx_needed[layer]
        n_out = n_all if keep_ctx else n_lat
        mod = mods[layer]
        ng1 = norm_mix_g[layer].reshape(1, d)
        ng2 = norm_ffn_g[layer].reshape(1, d)
        if is_hgrn:
            q, kf, lff, kb, lfb, v, g = _hg_in(xa, mod, ng1, lower_bounds[j].reshape(1, d),
                                               hg_w_in[j].astype(BF16), seq, n_all)
            o_fw = _gla_pass(q, kf, v, lff, seq, ctx_len, False)
            y = _gla_pass(q, kb, v, lfb, seq, ctx_len, True,
                          extra=(o_fw, g, hg_onorm_g[j].reshape(1, HG_DK)))
            w_out, b_out = hg_w_out[j].astype(BF16), zero_bias
        else:
            n_in = n_all if keep_ctx else n_lat
            u, x0 = _hy_in(xa, mod, ng1, hy_w_in[j].astype(BF16), hy_b_in[j], hy_short_w[j], hy_short_b[j],
                           seq, ctx_len, n_in)
            fpar = (hy_f_w1[j], hy_f_b1[j], hy_f_freq[j], hy_f_w2[j], hy_f_b2[j], hy_f_w3[j], hy_f_b3[j], hy_f_w4[j])
            skip = hy_skip[j].reshape(1, d)
            y = _long_conv(u, x0, 0, seq, *_hyena_filter(seq, *fpar), skip)
            if keep_ctx:
                yc = _long_conv(u, x0, n_lat, ctx_len, *_hyena_filter(ctx_len, *fpar), skip)
                y = jnp.concatenate([y, yc], axis=0)
            w_out, b_out = hy_w_out[j].astype(BF16), hy_b_out[j].reshape(1, d)
        wr = jnp.pad(moe_w_router[layer], ((0, 0), (0, LANES - N_EXPERTS)))
        br = jnp.pad(moe_b_router[layer], (0, LANES - N_EXPERTS)).reshape(1, LANES)
        xa, h2, logits = _out_proj(y, w_out, b_out, xa, mod, ng2, wr, br, seq, n_out)
        xa = _moe(h2, logits, xa, mod, final_g, moe_w_gu[layer].astype(BF16), moe_b_gu[layer],
                  moe_w_down[layer].astype(BF16), moe_b_down[layer], seq, layer == depth - 1)
    return xa.reshape(bsz, seq, d)
```

```python
import functools
import math

import numpy as np
import jax
import jax.numpy as jnp
from jax import lax
from jax.experimental import pallas as pl
from jax.experimental.pallas import tpu as pltpu

F32 = jnp.float32
BF16 = jnp.bfloat16

EPS = 1e-6
N_MOD = 6
LANES = 128
GRID_W = 64
HG_DK = 128
GLA_CHUNK = 128
GLA_ROWS = 256
GLA_HEADS = 2
HY_BANDS = 16
HY_DECAY_PCT_MIN = 0.3
HY_DECAY_PCT_MAX = 1.5
HY_DECAY_TARGET = 1e-2
N_EXPERTS = 32
TOP_K = 4
SWIGLU_ALPHA = 1.702
SWIGLU_LIMIT = 7.0
MOE_TILE = 512
VMEM_LIMIT = 56 * 1024 * 1024


def _params(sem, vmem=VMEM_LIMIT):
    return pltpu.CompilerParams(dimension_semantics=sem, vmem_limit_bytes=vmem)


def _sigmoid(x):
    return 1.0 / (1.0 + jnp.exp(-x))


def _dot(a, b):
    return jnp.dot(a, b, preferred_element_type=F32)


def _dot_nt(a, b):
    return lax.dot_general(a, b, (((1,), (1,)), ((), ())), preferred_element_type=F32)


def _split(a):
    hi = a.astype(BF16)
    lo = (a - hi.astype(F32)).astype(BF16)
    return hi, lo


def _dot3(a, b):
    (ah, al), (bh, bl) = a, b
    return _dot(ah, bh) + _dot(ah, bl) + _dot(al, bh)


def _rms_mod(x, g, shift, scale):
    ms = jnp.mean(x * x, axis=-1, keepdims=True)
    return x * lax.rsqrt(ms + EPS) * g * (1.0 + scale) + shift


def _ada_kernel(c_ref, w_ref, b_ref, o_ref):
    c = c_ref[...]
    o_ref[0] = _dot(c * _sigmoid(c), w_ref[0]) + b_ref[0]


def _ada_all(cond8, ada_w, ada_b):
    depth, d, nd = ada_w.shape
    tn = 1024
    return pl.pallas_call(
        _ada_kernel,
        grid=(depth, nd // tn),
        in_specs=[pl.BlockSpec((8, d), lambda l, j: (0, 0)),
                  pl.BlockSpec((1, d, tn), lambda l, j: (l, 0, j)),
                  pl.BlockSpec((1, 1, tn), lambda l, j: (l, 0, j))],
        out_specs=pl.BlockSpec((1, 8, tn), lambda l, j: (l, 0, j)),
        out_shape=jax.ShapeDtypeStruct((depth, 8, nd), F32),
        compiler_params=_params(("arbitrary", "arbitrary")),
        name="ada_mod",
    )(cond8, ada_w, ada_b.reshape(depth, 1, nd))


def _hg_in_kernel(x_ref, mod_ref, ng_ref, lb_ref, wq, wf, wb, wi, wg,
                  q_o, kf_o, lff_o, kb_o, lfb_o, v_o, g_o, h_scr):
    @pl.when(pl.program_id(1) == 0)
    def _():
        m = mod_ref[0]
        h_scr[...] = _rms_mod(x_ref[...], ng_ref[...], m[0:1], m[1:2]).astype(BF16)

    h = h_scr[...]
    q = _dot(h, wq[...])
    q_o[...] = q * _sigmoid(q) * (HG_DK ** -0.5)
    lb = lb_ref[...]
    for w, k_o, lf_o in ((wf, kf_o, lff_o), (wb, kb_o, lfb_o)):
        f = lb + (1.0 - lb) * _sigmoid(_dot(h, w[...]))
        k_o[...] = 1.0 - f
        lf_o[...] = jnp.log(f)
    v_o[...] = _dot(h, wi[...])
    g_o[...] = _dot(h, wg[...])


def _hg_in(x, mod, ng, lb, w_bf, seq, n_rows):
    d = x.shape[1]
    tm, tn = 512, 256
    nj = d // tn
    grp = lambda i, j: (jnp.minimum((i * tm) // seq, 2), 0, 0)
    wspec = lambda s: pl.BlockSpec((d, tn), lambda i, j, s=s: (0, s * nj + j))
    ospec = pl.BlockSpec((tm, tn), lambda i, j: (i, j))
    osh = jax.ShapeDtypeStruct((n_rows, d), F32)
    return pl.pallas_call(
        _hg_in_kernel,
        grid=(n_rows // tm, nj),
        in_specs=[pl.BlockSpec((tm, d), lambda i, j: (i, 0)),
                  pl.BlockSpec((1, 8, d), grp),
                  pl.BlockSpec((1, d), lambda i, j: (0, 0)),
                  pl.BlockSpec((1, tn), lambda i, j: (0, j))] + [wspec(s) for s in range(5)],
        out_specs=[ospec] * 7,
        out_shape=[osh] * 7,
        scratch_shapes=[pltpu.VMEM((tm, d), BF16)],
        compiler_params=_params(("arbitrary", "arbitrary")),
        name="hgrn_in_proj",
    )(x, mod, ng, lb, w_bf, w_bf, w_bf, w_bf, w_bf)


def _level_map(c, reverse):
    i = np.arange(c)[:, None]
    j = np.arange(c)[None, :]
    x = i ^ j
    lvl = np.where(x > 0, np.floor(np.log2(np.maximum(x, 1))), -1).astype(np.int32)
    side = (i < j) if reverse else (i > j)
    return np.where(side, lvl, -1).astype(np.int32)


def _cumsum_rows(x, row, c, reverse):
    sh = 1
    while sh < c:
        if reverse:
            x = x + jnp.where(row < c - sh, pltpu.roll(x, c - sh, axis=0), 0.0)
        else:
            x = x + jnp.where(row >= sh, pltpu.roll(x, sh, axis=0), 0.0)
        sh *= 2
    return x


def _gla_chunk(q, k, v, lf, st, lvl, row, reverse):
    c, wd = q.shape
    nh = wd // HG_DK
    b = _cumsum_rows(lf, row, c, reverse)
    lf_up = pltpu.roll(lf, c - 1, axis=0)
    lf_dn = pltpu.roll(lf, 1, axis=0)
    att = [jnp.zeros((c, c), F32) for _ in range(nh)]
    p = 0
    while (1 << p) < c:
        s = 1 << p
        is_query = ((row & s) == 0) if reverse else ((row & s) != 0)
        if s == 1:
            a = jnp.where(is_query, lf, 0.0)
        elif s == 2:
            m = row & 3
            if reverse:
                a = jnp.where(m == 0, lf + lf_up, jnp.where(m == 1, lf, jnp.where(m == 2, 0.0, lf_dn)))
            else:
                a = jnp.where(m == 0, lf_up, jnp.where(m == 1, 0.0, jnp.where(m == 2, lf, lf + lf_dn)))
        else:
            nb = c // (2 * s)
            r = s if reverse else s - 1
            b3 = b.reshape(nb, 2 * s, wd)
            br = jnp.broadcast_to(b3[:, r:r + 1, :], (nb, 2 * s, wd)).reshape(c, wd)
            a = -jnp.abs(b - br)
        z = (jnp.where(is_query, q, k) * jnp.exp(a)).astype(BF16)
        for h in range(nh):
            zh = z[:, h * HG_DK:(h + 1) * HG_DK]
            att[h] = jnp.where(lvl == p, _dot_nt(zh, zh), att[h])
        p += 1
    tot = b[0:1] if reverse else b[c - 1:c]
    qe = (q * jnp.exp(b)).astype(BF16)
    kd = (k * jnp.exp(tot - b)).astype(BF16)
    dec = jnp.exp(tot)
    outs, new_st = [], []
    for h in range(nh):
        cs = slice(h * HG_DK, (h + 1) * HG_DK)
        vh = v[:, cs]
        dsum = jnp.sum(q[:, cs] * k[:, cs], axis=-1, keepdims=True)
        o = _dot(att[h].astype(BF16), vh.astype(BF16)) + dsum * vh
        o = o + _dot_nt(qe[:, cs], st[h].astype(BF16))
        outs.append(o)
        new_st.append(st[h] * dec[:, cs] + _dot(vh.T.astype(BF16), kd[:, cs]))
    return jnp.concatenate(outs, axis=1), new_st


def _gla_kernel(*refs, reverse, gated):
    if gated:
        q_ref, k_ref, v_ref, lf_ref, lvl_ref, ofw_ref, g_ref, on_ref, y_ref, s_scr = refs
    else:
        q_ref, k_ref, v_ref, lf_ref, lvl_ref, y_ref, s_scr = refs
    c = GLA_CHUNK
    rows, wd = q_ref.shape
    nh = wd // HG_DK
    nchunk = rows // c

    @pl.when(pl.program_id(2) == 0)
    def _():
        s_scr[...] = jnp.zeros_like(s_scr)

    lvl = lvl_ref[...]
    row = lax.broadcasted_iota(jnp.int32, (c, wd), 0)

    def body(i, carry):
        ci = (nchunk - 1 - i) if reverse else i
        rs = pl.ds(pl.multiple_of(ci * c, c), c)
        st = [s_scr[h] for h in range(nh)]
        o, st = _gla_chunk(q_ref[rs, :], k_ref[rs, :], v_ref[rs, :], lf_ref[rs, :], st, lvl, row, reverse)
        for h in range(nh):
            s_scr[h] = st[h]
        if gated:
            o = o + ofw_ref[rs, :]
            g = g_ref[rs, :]
            on = on_ref[...]
            ys = []
            for h in range(nh):
                oh = o[:, h * HG_DK:(h + 1) * HG_DK]
                ms = jnp.mean(oh * oh, axis=-1, keepdims=True)
                ys.append(oh * lax.rsqrt(ms + EPS) * on)
            y = jnp.concatenate(ys, axis=1) * (g * _sigmoid(g))
            y_ref[rs, :] = y.astype(y_ref.dtype)
        else:
            y_ref[rs, :] = o
        return carry

    lax.fori_loop(0, nchunk, body, 0)


def _gla_pass(q, k, v, lf, seq, ctx_len, reverse, extra=None):
    n_all, d = q.shape
    rows, wd = GLA_ROWS, GLA_HEADS * HG_DK
    nlat, nctx = seq // rows, ctx_len // rows
    ctx_base = 2 * nlat

    def rmap(b, hh, t):
        tc = (nctx - 1 - t) if reverse else t
        tl = (nlat - 1 - (t - nctx)) if reverse else (t - nctx)
        return (jnp.where(t < nctx, ctx_base + b * nctx + tc, b * nlat + tl), hh)

    blk = pl.BlockSpec((rows, wd), rmap)
    lvl = jnp.asarray(_level_map(GLA_CHUNK, reverse))
    in_specs = [blk, blk, blk, blk, pl.BlockSpec((GLA_CHUNK, GLA_CHUNK), lambda b, hh, t: (0, 0))]
    args = [q, k, v, lf, lvl]
    gated = extra is not None
    if gated:
        in_specs += [blk, blk, pl.BlockSpec((1, HG_DK), lambda b, hh, t: (0, 0))]
        args += list(extra)
    return pl.pallas_call(
        functools.partial(_gla_kernel, reverse=reverse, gated=gated),
        grid=(2, d // wd, nlat + nctx),
        in_specs=in_specs,
        out_specs=blk,
        out_shape=jax.ShapeDtypeStruct((n_all, d), BF16 if gated else F32),
        scratch_shapes=[pltpu.VMEM((GLA_HEADS, HG_DK, HG_DK), F32)],
        compiler_params=_params(("arbitrary", "arbitrary", "arbitrary")),
        name="gla_scan_bwd" if reverse else "gla_scan_fwd",
    )(*args)


def _out_kernel(y_ref, w_ref, b_ref, x_ref, mod_ref, ng_ref, wr_ref, br_ref, xo_ref, h2_ref, lg_ref):
    m = mod_ref[0]
    xn = x_ref[...] + m[2:3] * (_dot(y_ref[...], w_ref[...]) + b_ref[...])
    xo_ref[...] = xn
    h2 = _rms_mod(xn, ng_ref[...], m[3:4], m[4:5])
    h2_ref[...] = h2
    lg_ref[...] = _dot3(_split(h2), _split(wr_ref[...])) + br_ref[...]


def _out_proj(y, w_bf, bias, x, mod, ng, wr_pad, br_pad, seq, n_rows):
    d = x.shape[1]
    tm = 256
    row = lambda i: (i, 0)
    const = lambda i: (0, 0)
    return pl.pallas_call(
        _out_kernel,
        grid=(n_rows // tm,),
        in_specs=[pl.BlockSpec((tm, d), row),
                  pl.BlockSpec((d, d), const),
                  pl.BlockSpec((1, d), const),
                  pl.BlockSpec((tm, d), row),
                  pl.BlockSpec((1, 8, d), lambda i: (jnp.minimum((i * tm) // seq, 2), 0, 0)),
                  pl.BlockSpec((1, d), const),
                  pl.BlockSpec((d, LANES), const),
                  pl.BlockSpec((1, LANES), const)],
        out_specs=[pl.BlockSpec((tm, d), row), pl.BlockSpec((tm, d), row), pl.BlockSpec((tm, LANES), row)],
        out_shape=[jax.ShapeDtypeStruct((n_rows, d), F32), jax.ShapeDtypeStruct((n_rows, d), F32),
                   jax.ShapeDtypeStruct((n_rows, LANES), F32)],
        compiler_params=_params(("arbitrary",)),
        name="out_proj",
    )(y, w_bf, bias, x, mod, ng, wr_pad, br_pad)


def _topk_kernel(lg_ref, idx_ref, w_ref):
    l = lg_ref[...]
    lane = lax.broadcasted_iota(jnp.int32, l.shape, 1).astype(F32)
    l = jnp.where(lane < N_EXPERTS, l, -jnp.inf)
    idx_out = jnp.zeros(l.shape, F32)
    vals = []
    for r in range(TOP_K):
        m = jnp.max(l, axis=-1, keepdims=True)
        sel = jnp.min(jnp.where(l == m, lane, float(LANES)), axis=-1, keepdims=True)
        idx_out = jnp.where(lane == r, sel, idx_out)
        l = jnp.where(lane == sel, -jnp.inf, l)
        vals.append(m)
    es = [jnp.exp(v - vals[0]) for v in vals]
    tot = es[0] + es[1] + es[2] + es[3]
    w_out = jnp.zeros(l.shape, F32)
    for r in range(TOP_K):
        w_out = jnp.where(lane == r, es[r] / tot, w_out)
    idx_ref[...] = idx_out.astype(jnp.int32)
    w_ref[...] = w_out


def _topk(logits):
    n = logits.shape[0]
    tm = 512
    spec = pl.BlockSpec((tm, LANES), lambda i: (i, 0))
    return pl.pallas_call(
        _topk_kernel,
        grid=(n // tm,),
        in_specs=[spec],
        out_specs=[spec, spec],
        out_shape=[jax.ShapeDtypeStruct((n, LANES), jnp.int32), jax.ShapeDtypeStruct((n, LANES), F32)],
        compiler_params=_params(("arbitrary",)),
        name="router_topk",
    )(logits)


def _rank_kernel(idx_ref, rank_ref, cnt_ref, run_scr):
    @pl.when(pl.program_id(0) == 0)
    def _():
        run_scr[...] = jnp.zeros_like(run_scr)

    idx = idx_ref[...]
    tm = idx.shape[0]
    lane = lax.broadcasted_iota(jnp.int32, idx.shape, 1)
    onehots = [jnp.where(lane == idx[:, j:j + 1], 1.0, 0.0) for j in range(TOP_K)]
    esum = onehots[0] + onehots[1] + onehots[2] + onehots[3]
    ri = lax.broadcasted_iota(jnp.int32, (tm, tm), 0)
    ci = lax.broadcasted_iota(jnp.int32, (tm, tm), 1)
    before = jnp.where(ci < ri, 1.0, 0.0).astype(BF16)
    base = run_scr[...] + _dot(before, esum.astype(BF16))
    rank = jnp.zeros(idx.shape, F32)
    for j in range(TOP_K):
        rank = jnp.where(lane == j, jnp.sum(onehots[j] * base, axis=-1, keepdims=True), rank)
    rank_ref[...] = rank.astype(jnp.int32)
    run_scr[...] = run_scr[...] + jnp.sum(esum, axis=0, keepdims=True)
    cnt_ref[...] = run_scr[...]


def _rank(idx):
    n = idx.shape[0]
    tm = 512
    spec = pl.BlockSpec((tm, LANES), lambda i: (i, 0))
    return pl.pallas_call(
        _rank_kernel,
        grid=(n // tm,),
        in_specs=[spec],
        out_specs=[spec, pl.BlockSpec((1, LANES), lambda i: (0, 0))],
        out_shape=[jax.ShapeDtypeStruct((n, LANES), jnp.int32), jax.ShapeDtypeStruct((1, LANES), F32)],
        scratch_shapes=[pltpu.VMEM((1, LANES), F32)],
        compiler_params=_params(("arbitrary",)),
        name="router_rank",
    )(idx)


GATHER_ROWS = MOE_TILE
COMBINE_ROWS = 256


def _row_copy(src_hbm, s, dst, r, sem):
    return pltpu.make_async_copy(src_hbm.at[pl.ds(s, 1)], dst.at[pl.ds(r, 1)], sem)


def _gather_kernel(src_ref, h_hbm, xs_ref, sem):
    def start(r, c):
        _row_copy(h_hbm, src_ref[r], xs_ref, r, sem).start()
        return c

    def wait(r, c):
        _row_copy(h_hbm, src_ref[r], xs_ref, r, sem).wait()
        return c

    lax.fori_loop(0, GATHER_ROWS, start, 0)
    lax.fori_loop(0, GATHER_ROWS, wait, 0)


def _gather_rows(h, src, p_max):
    d = h.shape[1]
    return pl.pallas_call(
        _gather_kernel,
        grid=(p_max // GATHER_ROWS,),
        in_specs=[pl.BlockSpec((GATHER_ROWS,), lambda i: (i,), memory_space=pltpu.SMEM),
                  pl.BlockSpec(memory_space=pl.ANY)],
        out_specs=pl.BlockSpec((GATHER_ROWS, d), lambda i: (i, 0)),
        out_shape=jax.ShapeDtypeStruct((p_max, d), F32),
        scratch_shapes=[pltpu.SemaphoreType.DMA(())],
        compiler_params=_params(("arbitrary",)),
        name="moe_dispatch",
    )(src, h)


def _expert_kernel(te_ref, nv_ref, xs_ref, wgu_ref, bgu_ref, wdn_ref, bdn_ref, ys_ref):
    t = pl.program_id(0)

    @pl.when(t < nv_ref[0])
    def _():
        f = wdn_ref.shape[1]
        gu = _dot(xs_ref[...].astype(BF16), wgu_ref[0]) + bgu_ref[0]
        gate = jnp.minimum(gu[:, :f], SWIGLU_LIMIT)
        up = jnp.clip(gu[:, f:], -SWIGLU_LIMIT, SWIGLU_LIMIT)
        act = (up + 1.0) * gate * _sigmoid(SWIGLU_ALPHA * gate)
        ys_ref[...] = _dot(act.astype(BF16), wdn_ref[0]) + bdn_ref[0]

    @pl.when(t >= nv_ref[0])
    def _():
        ys_ref[...] = jnp.zeros_like(ys_ref)


def _experts(xs, tile_expert, n_valid, wgu_bf, bgu, wdn_bf, bdn):
    p_max, d = xs.shape
    ne, _, f2 = wgu_bf.shape
    f = f2 // 2
    tm = MOE_TILE
    return pl.pallas_call(
        _expert_kernel,
        grid_spec=pltpu.PrefetchScalarGridSpec(
            num_scalar_prefetch=2,
            grid=(p_max // tm,),
            in_specs=[pl.BlockSpec((tm, d), lambda t, te, nv: (jnp.minimum(t, nv[0] - 1), 0)),
                      pl.BlockSpec((1, d, f2), lambda t, te, nv: (te[t], 0, 0)),
                      pl.BlockSpec((1, 1, f2), lambda t, te, nv: (te[t], 0, 0)),
                      pl.BlockSpec((1, f, d), lambda t, te, nv: (te[t], 0, 0)),
                      pl.BlockSpec((1, 1, d), lambda t, te, nv: (te[t], 0, 0))],
            out_specs=pl.BlockSpec((tm, d), lambda t, te, nv: (t, 0))),
        out_shape=jax.ShapeDtypeStruct((p_max, d), F32),
        compiler_params=_params(("arbitrary",)),
        name="moe_experts",
    )(tile_expert, n_valid, xs, wgu_bf, bgu.reshape(ne, 1, f2), wdn_bf, bdn.reshape(ne, 1, d))


def _combine_kernel(dest_ref, x_ref, w_ref, mod_ref, fg_ref, ys_hbm, o_ref, buf, sem, *, final):
    tm = x_ref.shape[0]

    def start(r, c):
        for j in range(TOP_K):
            _row_copy(ys_hbm, dest_ref[r * TOP_K + j], buf.at[j], r, sem).start()
        return c

    def wait(r, c):
        for j in range(TOP_K):
            _row_copy(ys_hbm, dest_ref[r * TOP_K + j], buf.at[j], r, sem).wait()
        return c

    lax.fori_loop(0, tm, start, 0)
    lax.fori_loop(0, tm, wait, 0)
    w = w_ref[...]
    acc = w[:, 0:1] * buf[0]
    for j in range(1, TOP_K):
        acc = acc + w[:, j:j + 1] * buf[j]
    out = x_ref[...] + mod_ref[0][5:6] * acc
    if final:
        ms = jnp.mean(out * out, axis=-1, keepdims=True)
        out = out * lax.rsqrt(ms + EPS) * fg_ref[...]
    o_ref[...] = out


def _combine(dest_flat, x, wts, mod, final_g, ys, seq, n_rows, final):
    d = x.shape[1]
    tm = COMBINE_ROWS
    row = lambda i: (i, 0)
    return pl.pallas_call(
        functools.partial(_combine_kernel, final=final),
        grid=(n_rows // tm,),
        in_specs=[pl.BlockSpec((tm * TOP_K,), lambda i: (i,), memory_space=pltpu.SMEM),
                  pl.BlockSpec((tm, d), row),
                  pl.BlockSpec((tm, LANES), row),
                  pl.BlockSpec((1, 8, d), lambda i: (jnp.minimum((i * tm) // seq, 2), 0, 0)),
                  pl.BlockSpec((1, d), lambda i: (0, 0)),
                  pl.BlockSpec(memory_space=pl.ANY)],
        out_specs=pl.BlockSpec((tm, d), row),
        out_shape=jax.ShapeDtypeStruct((n_rows, d), F32),
        scratch_shapes=[pltpu.VMEM((TOP_K, tm, d), F32), pltpu.SemaphoreType.DMA(())],
        compiler_params=_params(("arbitrary",)),
        name="moe_combine",
    )(dest_flat, x, wts, mod, final_g, ys)


def _moe(h2, logits, x, mod, final_g, wgu_bf, bgu, wdn_bf, bdn, seq, final):
    n = h2.shape[0]
    tm = MOE_TILE
    idx, wts = _topk(logits)
    rank, counts = _rank(idx)
    cnt = counts[0, :N_EXPERTS].astype(jnp.int32)
    tiles_e = (cnt + tm - 1) // tm
    tile_end = jnp.cumsum(tiles_e)
    offs = (tile_end - tiles_e) * tm
    idx4, rank4 = idx[:, :TOP_K], rank[:, :TOP_K]
    dest = (offs[idx4] + rank4).reshape(-1)
    p_max = (n * TOP_K // tm + N_EXPERTS) * tm
    n_tiles = p_max // tm
    n_valid = tile_end[-1:]
    tile_id = jnp.minimum(jnp.arange(n_tiles, dtype=jnp.int32), n_valid[0] - 1)
    tile_expert = jnp.sum((tile_end[None, :] <= tile_id[:, None]).astype(jnp.int32), axis=1)
    tile_expert = jnp.minimum(tile_expert, N_EXPERTS - 1)
    tok = jnp.repeat(jnp.arange(n, dtype=jnp.int32), TOP_K)
    src = jnp.zeros((p_max,), jnp.int32).at[dest].set(tok)
    xs = _gather_rows(h2, src, p_max)
    ys = _experts(xs, tile_expert, n_valid.astype(jnp.int32), wgu_bf, bgu, wdn_bf, bdn)
    return _combine(dest, x, wts, mod, final_g, ys, seq, n, final)


def _hy_in_kernel(x_ref, mod_ref, ng_ref, w0, w1, w2, bin_ref, sw_ref, sb_ref, u_o, x0_o, h_scr, *, n_lat_tiles, ctx_len):
    i = pl.program_id(0)

    @pl.when(pl.program_id(1) == 0)
    def _():
        m = mod_ref[0]
        h_scr[...] = _rms_mod(x_ref[...], ng_ref[...], m[0:1], m[1:2]).astype(BF16)

    h = h_scr[...]
    tm, tn = u_o.shape
    row = lax.broadcasted_iota(jnp.int32, (tm, tn), 0)
    seg = jnp.where(i >= n_lat_tiles, ctx_len, GRID_W)
    pos = row & (seg - 1)
    first = pos == 0
    last = pos == seg - 1
    bin_ = bin_ref[...]
    sw = sw_ref[...]
    sb = sb_ref[...]
    outs = []
    for s, w in enumerate((w0, w1, w2)):
        z = _dot(h, w[...]) + bin_[s:s + 1]
        zp = jnp.where(first, 0.0, pltpu.roll(z, 1, axis=0))
        zn = jnp.where(last, 0.0, pltpu.roll(z, tm - 1, axis=0))
        outs.append(sw[0, s:s + 1] * zp + sw[1, s:s + 1] * z + sw[2, s:s + 1] * zn + sb[s:s + 1])
    x0_o[...] = outs[0]
    u_o[...] = outs[2] * outs[1]


def _hy_in(x, mod, ng, w_bf, b_in, short_w, short_b, seq, ctx_len, n_rows):
    d = x.shape[1]
    tm, tn = 512, 256
    nj = d // tn
    wspec = lambda s: pl.BlockSpec((d, tn), lambda i, j, s=s: (0, s * nj + j))
    ospec = pl.BlockSpec((tm, tn), lambda i, j: (i, j))
    osh = jax.ShapeDtypeStruct((n_rows, d), F32)
    return pl.pallas_call(
        functools.partial(_hy_in_kernel, n_lat_tiles=2 * seq // tm, ctx_len=ctx_len),
        grid=(n_rows // tm, nj),
        in_specs=[pl.BlockSpec((tm, d), lambda i, j: (i, 0)),
                  pl.BlockSpec((1, 8, d), lambda i, j: (jnp.minimum((i * tm) // seq, 2), 0, 0)),
                  pl.BlockSpec((1, d), lambda i, j: (0, 0)),
                  wspec(0), wspec(1), wspec(2),
                  pl.BlockSpec((3, tn), lambda i, j: (0, j)),
                  pl.BlockSpec((3, 3, tn), lambda i, j: (0, 0, j)),
                  pl.BlockSpec((3, tn), lambda i, j: (0, j))],
        out_specs=[ospec, ospec],
        out_shape=[osh, osh],
        scratch_shapes=[pltpu.VMEM((tm, d), BF16)],
        compiler_params=_params(("arbitrary", "arbitrary")),
        name="hyena_in_proj",
    )(x, mod, ng, w_bf, w_bf, w_bf, b_in.reshape(3, d), short_w.reshape(3, 3, d), short_b.reshape(3, d))


def _filter_kernel(z_ref, w1, b1, fr, w2, b2, w3, b3, w4, dl_ref, hf_o, hb_o, nrm_o):
    i = pl.program_id(0)
    z = z_ref[...]
    freq = fr[...]
    a = jnp.sin(freq * (_dot(z, w1[...]) + b1[...]))
    a = jnp.sin(freq * (_dot(a, w2[...]) + b2[...]))
    a = jnp.sin(freq * (_dot(a, w3[...]) + b3[...]))
    h = _dot(a, w4[...])
    d = hf_o.shape[1]
    decay = jnp.exp(-z[:, 0:1] * dl_ref[...])
    hf = h[:, :d] * decay
    hb = h[:, d:] * decay
    hf_o[...] = hf
    hb_o[...] = hb
    row = lax.broadcasted_iota(jnp.int32, hb.shape, 0) + i * hb.shape[0]
    part = jnp.sum(jnp.abs(hf) + jnp.where(row > 0, jnp.abs(hb), 0.0), axis=0, keepdims=True)

    @pl.when(i == 0)
    def _():
        nrm_o[...] = jnp.zeros_like(nrm_o)

    nrm_o[...] = nrm_o[...] + part


def _pos_features(length):
    t = (np.arange(length, dtype=np.float64) / length)[:, None]
    bands = np.linspace(1e-4, HY_BANDS - 1, HY_BANDS)[None, :]
    ang = 2.0 * math.pi * t * bands
    return np.concatenate([t, np.cos(ang), -np.sin(ang)], axis=-1).astype(np.float32)


def _decay_rates(d):
    return np.abs(np.linspace(math.log(HY_DECAY_PCT_MIN) / HY_DECAY_TARGET,
                              math.log(HY_DECAY_PCT_MAX) / HY_DECAY_TARGET, d)).astype(np.float32)[None, :]


def _hyena_filter(length, w1, b1, freq, w2, b2, w3, b3, w4):
    d = w4.shape[1] // 2
    tr = min(length, 512)
    feats = _pos_features(length)
    z = jnp.asarray(np.pad(feats, ((0, 0), (0, LANES - feats.shape[1]))))
    const = lambda i: (0, 0)
    full = lambda a: pl.BlockSpec(a.shape, const)
    mat = lambda a: jnp.pad(a, ((0, LANES - a.shape[0]), (0, LANES - a.shape[1])))
    vec = lambda a: jnp.pad(a, (0, LANES - a.shape[0])).reshape(1, LANES)
    w4p = jnp.pad(w4, ((0, LANES - w4.shape[0]), (0, 0)))
    args = [z, mat(w1), vec(b1), vec(freq), mat(w2), vec(b2), mat(w3), vec(b3), w4p, jnp.asarray(_decay_rates(d))]
    return pl.pallas_call(
        _filter_kernel,
        grid=(length // tr,),
        in_specs=[pl.BlockSpec((tr, z.shape[1]), lambda i: (i, 0))] + [full(a) for a in args[1:]],
        out_specs=[pl.BlockSpec((tr, d), lambda i: (i, 0)), pl.BlockSpec((tr, d), lambda i: (i, 0)),
                   pl.BlockSpec((1, d), const)],
        out_shape=[jax.ShapeDtypeStruct((length, d), F32), jax.ShapeDtypeStruct((length, d), F32),
                   jax.ShapeDtypeStruct((1, d), F32)],
        compiler_params=_params(("arbitrary",)),
        name="hyena_filter",
    )(*args)


def _dft_mats(m, k, n):
    r = np.arange(m, dtype=np.int64)[:, None]
    c = np.arange(k, dtype=np.int64)[None, :]
    ang = 2.0 * math.pi * ((r * c) % n).astype(np.float64) / n
    return np.cos(ang).astype(np.float32), np.sin(ang).astype(np.float32)


def _cmul_mat(c, s, xr, xi, conj):
    xr = _split(xr)
    cr, sr = _dot3(c, xr), _dot3(s, xr)
    if xi is None:
        return cr, (sr if conj else -sr)
    xi = _split(xi)
    ci, si = _dot3(c, xi), _dot3(s, xi)
    if conj:
        return cr - si, ci + sr
    return cr + si, ci - sr


def _slab_fwd_kernel(*refs, has_imag, has_filter):
    refs = list(refs)
    fc, fs, xr_ref = refs[:3]
    xi_ref = refs[3] if has_imag else None
    rest = refs[3 + has_imag:]
    if has_filter:
        gr_ref, gi_ref, gn_ref, or_ref, oi_ref = rest
    else:
        or_ref, oi_ref = rest
    yr, yi = _cmul_mat(_split(fc[...]), _split(fs[...]), xr_ref[...], xi_ref[...] if has_imag else None, False)
    if has_filter:
        gn = 1.0 / gn_ref[...]
        gr, gi = gr_ref[...] * gn, gi_ref[...] * gn
        yr, yi = yr * gr - yi * gi, yr * gi + yi * gr
    or_ref[...] = yr
    oi_ref[...] = yi


def _slab_fwd(x2, k_in, parts, n_out, n, spectrum=None):
    cols = x2.shape[1]
    tc = min(cols, 2048 if n_out <= LANES else 512)
    fc, fs = _dft_mats(n_out, k_in, n)
    const = lambda c: (0, 0)
    in_specs = [pl.BlockSpec((n_out, k_in), const), pl.BlockSpec((n_out, k_in), const)]
    args = [jnp.asarray(fc), jnp.asarray(fs)]
    for part in parts:
        if part is not None:
            in_specs.append(pl.BlockSpec((k_in, tc), lambda c, part=part: (part, c)))
            args.append(x2)
    ospec = pl.BlockSpec((n_out, tc), lambda c: (0, c))
    if spectrum is not None:
        in_specs += [ospec, ospec, pl.BlockSpec((1, tc), lambda c: (0, c))]
        args += list(spectrum)
    osh = jax.ShapeDtypeStruct((n_out, cols), F32)
    return pl.pallas_call(
        functools.partial(_slab_fwd_kernel, has_imag=parts[1] is not None, has_filter=spectrum is not None),
        grid=(cols // tc,),
        in_specs=in_specs,
        out_specs=[ospec, ospec],
        out_shape=[osh, osh],
        compiler_params=_params(("arbitrary",)),
        name="dft_slab_fwd",
    )(*args)


def _mid_kernel(*refs, has_filter):
    if has_filter:
        fc, fs, twc_ref, tws_ref, ar_ref, ai_ref, gr_ref, gi_ref, gn_ref, or_ref, oi_ref = refs
    else:
        fc, fs, twc_ref, tws_ref, ar_ref, ai_ref, or_ref, oi_ref = refs
    td = ar_ref.shape[2]
    c, s = _split(fc[...]), _split(fs[...])
    twc = jnp.concatenate([twc_ref[0]] * (td // LANES), axis=1)
    tws = jnp.concatenate([tws_ref[0]] * (td // LANES), axis=1)
    ar, ai = ar_ref[0], ai_ref[0]
    pr, pi_ = ar * twc + ai * tws, ai * twc - ar * tws
    br, bi = _cmul_mat(c, s, pr, pi_, False)
    if not has_filter:
        or_ref[0] = br
        oi_ref[0] = bi
        return
    gn = 1.0 / gn_ref[...]
    gr, gi = gr_ref[0] * gn, gi_ref[0] * gn
    yr, yi = br * gr - bi * gi, br * gi + bi * gr
    qr, qi = _cmul_mat(c, s, yr, yi, True)
    or_ref[0] = qr * twc - qi * tws
    oi_ref[0] = qr * tws + qi * twc


def _mid_stage(a_re, a_im, n, spectrum=None):
    n1, _, d = a_re.shape
    td = 512
    fc, fs = _dft_mats(LANES, LANES, LANES)
    k1 = lax.broadcasted_iota(jnp.int32, (n1, LANES, LANES), 0)
    m2 = lax.broadcasted_iota(jnp.int32, (n1, LANES, LANES), 1)
    ang = ((k1 * m2) % n).astype(F32) * (2.0 * math.pi / n)
    twc, tws = jnp.cos(ang), jnp.sin(ang)
    const = lambda s, j: (0, 0)
    slab = pl.BlockSpec((1, LANES, td), lambda s, j: (s, 0, j))
    tw = pl.BlockSpec((1, LANES, LANES), lambda s, j: (s, 0, 0))
    in_specs = [pl.BlockSpec((LANES, LANES), const), pl.BlockSpec((LANES, LANES), const), tw, tw, slab, slab]
    args = [jnp.asarray(fc), jnp.asarray(fs), twc, tws, a_re, a_im]
    if spectrum is not None:
        in_specs += [slab, slab, pl.BlockSpec((1, td), lambda s, j: (0, j))]
        args += list(spectrum)
    osh = jax.ShapeDtypeStruct((n1, LANES, d), F32)
    return pl.pallas_call(
        functools.partial(_mid_kernel, has_filter=spectrum is not None),
        grid=(n1, d // td),
        in_specs=in_specs,
        out_specs=[slab, slab],
        out_shape=[osh, osh],
        compiler_params=_params(("arbitrary", "arbitrary")),
        name="dft_mid",
    )(*args)


def _slab_inv_kernel(fc, fs, pr_ref, pi_ref, u0_ref, u1_ref, a0_ref, a1_ref, skip_ref, o_ref, *, scale):
    yr, yi = _cmul_mat(_split(fc[...]), _split(fs[...]), pr_ref[...], pi_ref[...], True)
    skip = skip_ref[...]
    o_ref[0] = ((yr * scale + u0_ref[...] * skip) * a0_ref[...]).astype(o_ref.dtype)
    o_ref[1] = ((yi * scale + u1_ref[...] * skip) * a1_ref[...]).astype(o_ref.dtype)


def _slab_inv(p_re, p_im, u2, x02, k_out, parts, skip, n, scale):
    n_in, cols = p_re.shape
    d = skip.shape[1]
    tc = min(cols, d if n_in <= LANES else 512)
    fc, fs = _dft_mats(k_out, n_in, n)
    const = lambda c: (0, 0)
    pspec = pl.BlockSpec((n_in, tc), lambda c: (0, c))
    uspec = lambda part: pl.BlockSpec((k_out, tc), lambda c, part=part: (part, c))
    return pl.pallas_call(
        functools.partial(_slab_inv_kernel, scale=scale),
        grid=(cols // tc,),
        in_specs=[pl.BlockSpec((k_out, n_in), const), pl.BlockSpec((k_out, n_in), const),
                  pspec, pspec, uspec(parts[0]), uspec(parts[1]), uspec(parts[0]), uspec(parts[1]),
                  pl.BlockSpec((1, tc), lambda c: (0, c % (d // tc)))],
        out_specs=pl.BlockSpec((2, k_out, tc), lambda c: (0, 0, c)),
        out_shape=jax.ShapeDtypeStruct((2, k_out, cols), BF16),
        compiler_params=_params(("arbitrary",)),
        name="dft_slab_inv",
    )(jnp.asarray(fc), jnp.asarray(fs), p_re, p_im, u2, u2, x02, x02, skip)


def _long_conv(u, x0, row0, length, hf, hb, nrm, skip):
    d = u.shape[1]
    n = 2 * length
    filt = jnp.concatenate([hf, jnp.zeros((1, d), F32), hb[:0:-1]], axis=0)
    if n <= 4 * LANES:
        parts = (row0 // length, row0 // length + 1)
        g_re, g_im = _slab_fwd(filt, n, (0, None), n, n)
        y_re, y_im = _slab_fwd(u, length, parts, n, n, spectrum=(g_re, g_im, nrm))
        y = _slab_inv(y_re, y_im, u, x0, length, parts, skip, n, 1.0 / n)
        return y.reshape(2 * length, d)
    n1 = n // LANES
    cols = LANES * d
    k1 = n1 // 2
    parts = (row0 // length, row0 // length + 1)
    f_re, f_im = _slab_fwd(filt.reshape(n1, cols), n1, (0, None), n1, n1)
    g_re, g_im = _mid_stage(f_re.reshape(n1, LANES, d), f_im.reshape(n1, LANES, d), n)
    u2 = u.reshape(u.shape[0] // LANES, cols)
    a_re, a_im = _slab_fwd(u2, k1, parts, n1, n1)
    p_re, p_im = _mid_stage(a_re.reshape(n1, LANES, d), a_im.reshape(n1, LANES, d), n, spectrum=(g_re, g_im, nrm))
    y = _slab_inv(p_re.reshape(n1, cols), p_im.reshape(n1, cols), u2, x0.reshape(u2.shape), k1, parts, skip,
                  n1, 1.0 / n)
    return y.reshape(2 * length, d)


def kernel(x, c, ctx, c_ctx, ada_w, ada_b, norm_mix_g, norm_ffn_g, final_norm_g, hg_w_in, hg_lb, hg_onorm_g, hg_w_out, hy_w_in, hy_b_in, hy_short_w, hy_short_b, hy_f_w1, hy_f_b1, hy_f_freq, hy_f_w2, hy_f_b2, hy_f_w3, hy_f_b3, hy_f_w4, hy_skip, hy_w_out, hy_b_out, moe_w_router, moe_b_router, moe_w_gu, moe_b_gu, moe_w_down, moe_b_down):
    bsz, seq, d = x.shape
    ctx_len = ctx.shape[1]
    depth = ada_w.shape[0]
    n_lat = bsz * seq
    n_all = n_lat + bsz * ctx_len
    assert bsz == 2 and seq % MOE_TILE == 0 and ctx_len == GLA_ROWS

    cond8 = jnp.zeros((8, d), F32).at[0:2].set(c).at[2].set(c_ctx)
    mods = _ada_all(cond8, ada_w, ada_b).reshape(depth, 8, N_MOD, d)[:, :3]
    mods = jnp.pad(mods, ((0, 0), (0, 0), (0, 8 - N_MOD), (0, 0)))

    lb_soft = jax.nn.softmax(hg_lb.astype(F32), axis=0)
    lower_bounds = jnp.cumsum(lb_soft, axis=0) - lb_soft[0]

    xa = jnp.concatenate([x.reshape(n_lat, d), ctx.reshape(bsz * ctx_len, d)], axis=0)
    zero_bias = jnp.zeros((1, d), F32)
    final_g = final_norm_g.reshape(1, d)
    ctx_needed = [any(l % 2 == 0 for l in range(layer + 1, depth)) for layer in range(depth)]

    for layer in range(depth):
        j = layer // 2
        is_hgrn = layer % 2 == 0
        keep_ctx = ctx_needed[layer]
        n_out = n_all if keep_ctx else n_lat
        mod = mods[layer]
        ng1 = norm_mix_g[layer].reshape(1, d)
        ng2 = norm_ffn_g[layer].reshape(1, d)
        if is_hgrn:
            q, kf, lff, kb, lfb, v, g = _hg_in(xa, mod, ng1, lower_bounds[j].reshape(1, d),
                                               hg_w_in[j].astype(BF16), seq, n_all)
            o_fw = _gla_pass(q, kf, v, lff, seq, ctx_len, False)
            y = _gla_pass(q, kb, v, lfb, seq, ctx_len, True,
                          extra=(o_fw, g, hg_onorm_g[j].reshape(1, HG_DK)))
            w_out, b_out = hg_w_out[j].astype(BF16), zero_bias
        else:
            n_in = n_all if keep_ctx else n_lat
            u, x0 = _hy_in(xa, mod, ng1, hy_w_in[j].astype(BF16), hy_b_in[j], hy_short_w[j], hy_short_b[j],
                           seq, ctx_len, n_in)
            fpar = (hy_f_w1[j], hy_f_b1[j], hy_f_freq[j], hy_f_w2[j], hy_f_b2[j], hy_f_w3[j], hy_f_b3[j], hy_f_w4[j])
            skip = hy_skip[j].reshape(1, d)
            y = _long_conv(u, x0, 0, seq, *_hyena_filter(seq, *fpar), skip)
            if keep_ctx:
                yc = _long_conv(u, x0, n_lat, ctx_len, *_hyena_filter(ctx_len, *fpar), skip)
                y = jnp.concatenate([y, yc], axis=0)
            w_out, b_out = hy_w_out[j].astype(BF16), hy_b_out[j].reshape(1, d)
        wr = jnp.pad(moe_w_router[layer], ((0, 0), (0, LANES - N_EXPERTS)))
        br = jnp.pad(moe_b_router[layer], (0, LANES - N_EXPERTS)).reshape(1, LANES)
        xa, h2, logits = _out_proj(y, w_out, b_out, xa, mod, ng2, wr, br, seq, n_out)
        xa = _moe(h2, logits, xa, mod, final_g, moe_w_gu[layer].astype(BF16), moe_b_gu[layer],
                  moe_w_down[layer].astype(BF16), moe_b_down[layer], seq, layer == depth - 1)
    return xa.reshape(bsz, seq, d)
```

```python
import functools
import math

import numpy as np
import jax
import jax.numpy as jnp
from jax import lax
from jax.experimental import pallas as pl
from jax.experimental.pallas import tpu as pltpu

F32 = jnp.float32
BF16 = jnp.bfloat16

EPS = 1e-6
N_MOD = 6
LANES = 128
GRID_W = 64
HG_DK = 128
GLA_CHUNK = 128
GLA_ROWS = 256
GLA_HEADS = 2
HY_BANDS = 16
HY_DECAY_PCT_MIN = 0.3
HY_DECAY_PCT_MAX = 1.5
HY_DECAY_TARGET = 1e-2
N_EXPERTS = 32
TOP_K = 4
SWIGLU_ALPHA = 1.702
SWIGLU_LIMIT = 7.0
MOE_TILE = 512
VMEM_LIMIT = 56 * 1024 * 1024


def _params(sem, vmem=VMEM_LIMIT):
    return pltpu.CompilerParams(dimension_semantics=sem, vmem_limit_bytes=vmem)


def _sigmoid(x):
    return 1.0 / (1.0 + jnp.exp(-x))


def _dot(a, b):
    return jnp.dot(a, b, preferred_element_type=F32)


def _dot_nt(a, b):
    return lax.dot_general(a, b, (((1,), (1,)), ((), ())), preferred_element_type=F32)


def _split(a):
    hi = a.astype(BF16)
    lo = (a - hi.astype(F32)).astype(BF16)
    return hi, lo


def _dot3(a, b):
    (ah, al), (bh, bl) = a, b
    return _dot(ah, bh) + _dot(ah, bl) + _dot(al, bh)


def _rms_mod(x, g, shift, scale):
    ms = jnp.mean(x * x, axis=-1, keepdims=True)
    return x * lax.rsqrt(ms + EPS) * g * (1.0 + scale) + shift


def _ada_kernel(c_ref, w_ref, b_ref, o_ref):
    c = c_ref[...]
    o_ref[0] = _dot(c * _sigmoid(c), w_ref[0]) + b_ref[0]


def _ada_all(cond8, ada_w, ada_b):
    depth, d, nd = ada_w.shape
    tn = 1024
    return pl.pallas_call(
        _ada_kernel,
        grid=(depth, nd // tn),
        in_specs=[pl.BlockSpec((8, d), lambda l, j: (0, 0)),
                  pl.BlockSpec((1, d, tn), lambda l, j: (l, 0, j)),
                  pl.BlockSpec((1, 1, tn), lambda l, j: (l, 0, j))],
        out_specs=pl.BlockSpec((1, 8, tn), lambda l, j: (l, 0, j)),
        out_shape=jax.ShapeDtypeStruct((depth, 8, nd), F32),
        compiler_params=_params(("arbitrary", "arbitrary")),
        name="ada_mod",
    )(cond8, ada_w, ada_b.reshape(depth, 1, nd))


def _hg_in_kernel(x_ref, mod_ref, ng_ref, lb_ref, wq, wf, wb, wi, wg,
                  q_o, kf_o, lff_o, kb_o, lfb_o, v_o, g_o, h_scr):
    @pl.when(pl.program_id(1) == 0)
    def _():
        m = mod_ref[0]
        h_scr[...] = _rms_mod(x_ref[...], ng_ref[...], m[0:1], m[1:2]).astype(BF16)

    h = h_scr[...]
    q = _dot(h, wq[...])
    q_o[...] = q * _sigmoid(q) * (HG_DK ** -0.5)
    lb = lb_ref[...]
    for w, k_o, lf_o in ((wf, kf_o, lff_o), (wb, kb_o, lfb_o)):
        f = lb + (1.0 - lb) * _sigmoid(_dot(h, w[...]))
        k_o[...] = 1.0 - f
        lf_o[...] = jnp.log(f)
    v_o[...] = _dot(h, wi[...])
    g_o[...] = _dot(h, wg[...])


def _hg_in(x, mod, ng, lb, w_bf, seq, n_rows):
    d = x.shape[1]
    tm, tn = 512, 256
    nj = d // tn
    grp = lambda i, j: (jnp.minimum((i * tm) // seq, 2), 0, 0)
    wspec = lambda s: pl.BlockSpec((d, tn), lambda i, j, s=s: (0, s * nj + j))
    ospec = pl.BlockSpec((tm, tn), lambda i, j: (i, j))
    osh = jax.ShapeDtypeStruct((n_rows, d), F32)
    return pl.pallas_call(
        _hg_in_kernel,
        grid=(n_rows // tm, nj),
        in_specs=[pl.BlockSpec((tm, d), lambda i, j: (i, 0)),
                  pl.BlockSpec((1, 8, d), grp),
                  pl.BlockSpec((1, d), lambda i, j: (0, 0)),
                  pl.BlockSpec((1, tn), lambda i, j: (0, j))] + [wspec(s) for s in range(5)],
        out_specs=[ospec] * 7,
        out_shape=[osh] * 7,
        scratch_shapes=[pltpu.VMEM((tm, d), BF16)],
        compiler_params=_params(("arbitrary", "arbitrary")),
        name="hgrn_in_proj",
    )(x, mod, ng, lb, w_bf, w_bf, w_bf, w_bf, w_bf)


def _level_map(c, reverse):
    i = np.arange(c)[:, None]
    j = np.arange(c)[None, :]
    x = i ^ j
    lvl = np.where(x > 0, np.floor(np.log2(np.maximum(x, 1))), -1).astype(np.int32)
    side = (i < j) if reverse else (i > j)
    return np.where(side, lvl, -1).astype(np.int32)


def _cumsum_rows(x, row, c, reverse):
    sh = 1
    while sh < c:
        if reverse:
            x = x + jnp.where(row < c - sh, pltpu.roll(x, c - sh, axis=0), 0.0)
        else:
            x = x + jnp.where(row >= sh, pltpu.roll(x, sh, axis=0), 0.0)
        sh *= 2
    return x


def _gla_chunk(q, k, v, lf, st, lvl, row, reverse):
    c, wd = q.shape
    nh = wd // HG_DK
    b = _cumsum_rows(lf, row, c, reverse)
    lf_up = pltpu.roll(lf, c - 1, axis=0)
    lf_dn = pltpu.roll(lf, 1, axis=0)
    att = [jnp.zeros((c, c), F32) for _ in range(nh)]
    p = 0
    while (1 << p) < c:
        s = 1 << p
        is_query = ((row & s) == 0) if reverse else ((row & s) != 0)
        if s == 1:
            a = jnp.where(is_query, lf, 0.0)
        elif s == 2:
            m = row & 3
            if reverse:
                a = jnp.where(m == 0, lf + lf_up, jnp.where(m == 1, lf, jnp.where(m == 2, 0.0, lf_dn)))
            else:
                a = jnp.where(m == 0, lf_up, jnp.where(m == 1, 0.0, jnp.where(m == 2, lf, lf + lf_dn)))
        else:
            nb = c // (2 * s)
            r = s if reverse else s - 1
            b3 = b.reshape(nb, 2 * s, wd)
            br = jnp.broadcast_to(b3[:, r:r + 1, :], (nb, 2 * s, wd)).reshape(c, wd)
            a = -jnp.abs(b - br)
        z = (jnp.where(is_query, q, k) * jnp.exp(a)).astype(BF16)
        for h in range(nh):
            zh = z[:, h * HG_DK:(h + 1) * HG_DK]
            att[h] = jnp.where(lvl == p, _dot_nt(zh, zh), att[h])
        p += 1
    tot = b[0:1] if reverse else b[c - 1:c]
    qe = (q * jnp.exp(b)).astype(BF16)
    kd = (k * jnp.exp(tot - b)).astype(BF16)
    dec = jnp.exp(tot)
    outs, new_st = [], []
    for h in range(nh):
        cs = slice(h * HG_DK, (h + 1) * HG_DK)
        vh = v[:, cs]
        dsum = jnp.sum(q[:, cs] * k[:, cs], axis=-1, keepdims=True)
        o = _dot(att[h].astype(BF16), vh.astype(BF16)) + dsum * vh
        o = o + _dot_nt(qe[:, cs], st[h].astype(BF16))
        outs.append(o)
        new_st.append(st[h] * dec[:, cs] + _dot(vh.T.astype(BF16), kd[:, cs]))
    return jnp.concatenate(outs, axis=1), new_st


def _gla_kernel(*refs, reverse, gated):
    if gated:
        q_ref, k_ref, v_ref, lf_ref, lvl_ref, ofw_ref, g_ref, on_ref, y_ref, s_scr = refs
    else:
        q_ref, k_ref, v_ref, lf_ref, lvl_ref, y_ref, s_scr = refs
    c = GLA_CHUNK
    rows, wd = q_ref.shape
    nh = wd // HG_DK
    nchunk = rows // c

    @pl.when(pl.program_id(2) == 0)
    def _():
        s_scr[...] = jnp.zeros_like(s_scr)

    lvl = lvl_ref[...]
    row = lax.broadcasted_iota(jnp.int32, (c, wd), 0)

    def body(i, carry):
        ci = (nchunk - 1 - i) if reverse else i
        rs = pl.ds(pl.multiple_of(ci * c, c), c)
        st = [s_scr[h] for h in range(nh)]
        o, st = _gla_chunk(q_ref[rs, :], k_ref[rs, :], v_ref[rs, :], lf_ref[rs, :], st, lvl, row, reverse)
        for h in range(nh):
            s_scr[h] = st[h]
        if gated:
            o = o + ofw_ref[rs, :]
            g = g_ref[rs, :]
            on = on_ref[...]
            ys = []
            for h in range(nh):
                oh = o[:, h * HG_DK:(h + 1) * HG_DK]
                ms = jnp.mean(oh * oh, axis=-1, keepdims=True)
                ys.append(oh * lax.rsqrt(ms + EPS) * on)
            y = jnp.concatenate(ys, axis=1) * (g * _sigmoid(g))
            y_ref[rs, :] = y.astype(y_ref.dtype)
        else:
            y_ref[rs, :] = o
        return carry

    lax.fori_loop(0, nchunk, body, 0)


def _gla_pass(q, k, v, lf, seq, ctx_len, reverse, extra=None):
    n_all, d = q.shape
    rows, wd = GLA_ROWS, GLA_HEADS * HG_DK
    nlat, nctx = seq // rows, ctx_len // rows
    ctx_base = 2 * nlat

    def rmap(b, hh, t):
        tc = (nctx - 1 - t) if reverse else t
        tl = (nlat - 1 - (t - nctx)) if reverse else (t - nctx)
        return (jnp.where(t < nctx, ctx_base + b * nctx + tc, b * nlat + tl), hh)

    blk = pl.BlockSpec((rows, wd), rmap)
    lvl = jnp.asarray(_level_map(GLA_CHUNK, reverse))
    in_specs = [blk, blk, blk, blk, pl.BlockSpec((GLA_CHUNK, GLA_CHUNK), lambda b, hh, t: (0, 0))]
    args = [q, k, v, lf, lvl]
    gated = extra is not None
    if gated:
        in_specs += [blk, blk, pl.BlockSpec((1, HG_DK), lambda b, hh, t: (0, 0))]
        args += list(extra)
    return pl.pallas_call(
        functools.partial(_gla_kernel, reverse=reverse, gated=gated),
        grid=(2, d // wd, nlat + nctx),
        in_specs=in_specs,
        out_specs=blk,
        out_shape=jax.ShapeDtypeStruct((n_all, d), BF16 if gated else F32),
        scratch_shapes=[pltpu.VMEM((GLA_HEADS, HG_DK, HG_DK), F32)],
        compiler_params=_params(("arbitrary", "arbitrary", "arbitrary")),
        name="gla_scan_bwd" if reverse else "gla_scan_fwd",
    )(*args)


def _out_kernel(y_ref, w_ref, b_ref, x_ref, mod_ref, ng_ref, wr_ref, br_ref, xo_ref, h2_ref, lg_ref):
    m = mod_ref[0]
    xn = x_ref[...] + m[2:3] * (_dot(y_ref[...], w_ref[...]) + b_ref[...])
    xo_ref[...] = xn
    h2 = _rms_mod(xn, ng_ref[...], m[3:4], m[4:5])
    h2_ref[...] = h2
    lg_ref[...] = _dot3(_split(h2), _split(wr_ref[...])) + br_ref[...]


def _out_proj(y, w_bf, bias, x, mod, ng, wr_pad, br_pad, seq, n_rows):
    d = x.shape[1]
    tm = 256
    row = lambda i: (i, 0)
    const = lambda i: (0, 0)
    return pl.pallas_call(
        _out_kernel,
        grid=(n_rows // tm,),
        in_specs=[pl.BlockSpec((tm, d), row),
                  pl.BlockSpec((d, d), const),
                  pl.BlockSpec((1, d), const),
                  pl.BlockSpec((tm, d), row),
                  pl.BlockSpec((1, 8, d), lambda i: (jnp.minimum((i * tm) // seq, 2), 0, 0)),
                  pl.BlockSpec((1, d), const),
                  pl.BlockSpec((d, LANES), const),
                  pl.BlockSpec((1, LANES), const)],
        out_specs=[pl.BlockSpec((tm, d), row), pl.BlockSpec((tm, d), row), pl.BlockSpec((tm, LANES), row)],
        out_shape=[jax.ShapeDtypeStruct((n_rows, d), F32), jax.ShapeDtypeStruct((n_rows, d), F32),
                   jax.ShapeDtypeStruct((n_rows, LANES), F32)],
        compiler_params=_params(("arbitrary",)),
        name="out_proj",
    )(y, w_bf, bias, x, mod, ng, wr_pad, br_pad)


def _topk_kernel(lg_ref, idx_ref, w_ref):
    l = lg_ref[...]
    lane = lax.broadcasted_iota(jnp.int32, l.shape, 1).astype(F32)
    l = jnp.where(lane < N_EXPERTS, l, -jnp.inf)
    idx_out = jnp.zeros(l.shape, F32)
    vals = []
    for r in range(TOP_K):
        m = jnp.max(l, axis=-1, keepdims=True)
        sel = jnp.min(jnp.where(l == m, lane, float(LANES)), axis=-1, keepdims=True)
        idx_out = jnp.where(lane == r, sel, idx_out)
        l = jnp.where(lane == sel, -jnp.inf, l)
        vals.append(m)
    es = [jnp.exp(v - vals[0]) for v in vals]
    tot = es[0] + es[1] + es[2] + es[3]
    w_out = jnp.zeros(l.shape, F32)
    for r in range(TOP_K):
        w_out = jnp.where(lane == r, es[r] / tot, w_out)
    idx_ref[...] = idx_out.astype(jnp.int32)
    w_ref[...] = w_out


def _topk(logits):
    n = logits.shape[0]
    tm = 512
    spec = pl.BlockSpec((tm, LANES), lambda i: (i, 0))
    return pl.pallas_call(
        _topk_kernel,
        grid=(n // tm,),
        in_specs=[spec],
        out_specs=[spec, spec],
        out_shape=[jax.ShapeDtypeStruct((n, LANES), jnp.int32), jax.ShapeDtypeStruct((n, LANES), F32)],
        compiler_params=_params(("arbitrary",)),
        name="router_topk",
    )(logits)


def _rank_kernel(idx_ref, rank_ref, cnt_ref, run_scr):
    @pl.when(pl.program_id(0) == 0)
    def _():
        run_scr[...] = jnp.zeros_like(run_scr)

    idx = idx_ref[...]
    tm = idx.shape[0]
    lane = lax.broadcasted_iota(jnp.int32, idx.shape, 1)
    onehots = [jnp.where(lane == idx[:, j:j + 1], 1.0, 0.0) for j in range(TOP_K)]
    esum = onehots[0] + onehots[1] + onehots[2] + onehots[3]
    ri = lax.broadcasted_iota(jnp.int32, (tm, tm), 0)
    ci = lax.broadcasted_iota(jnp.int32, (tm, tm), 1)
    before = jnp.where(ci < ri, 1.0, 0.0).astype(BF16)
    base = run_scr[...] + _dot(before, esum.astype(BF16))
    rank = jnp.zeros(idx.shape, F32)
    for j in range(TOP_K):
        rank = jnp.where(lane == j, jnp.sum(onehots[j] * base, axis=-1, keepdims=True), rank)
    rank_ref[...] = rank.astype(jnp.int32)
    run_scr[...] = run_scr[...] + jnp.sum(esum, axis=0, keepdims=True)
    cnt_ref[...] = run_scr[...]


def _rank(idx):
    n = idx.shape[0]
    tm = 512
    spec = pl.BlockSpec((tm, LANES), lambda i: (i, 0))
    return pl.pallas_call(
        _rank_kernel,
        grid=(n // tm,),
        in_specs=[spec],
        out_specs=[spec, pl.BlockSpec((1, LANES), lambda i: (0, 0))],
        out_shape=[jax.ShapeDtypeStruct((n, LANES), jnp.int32), jax.ShapeDtypeStruct((1, LANES), F32)],
        scratch_shapes=[pltpu.VMEM((1, LANES), F32)],
        compiler_params=_params(("arbitrary",)),
        name="router_rank",
    )(idx)


COMBINE_ROWS = 256


def _row_copy(src_hbm, s, dst, r, sem):
    return pltpu.make_async_copy(src_hbm.at[pl.ds(s, 1)], dst.at[pl.ds(r, 1)], sem)


DMA_UNROLL = 8


def _rows_in_flight(h_hbm, src_ref, buf, sem, wait):
    def body(i, c):
        for k in range(DMA_UNROLL):
            r = i * DMA_UNROLL + k
            cp = _row_copy(h_hbm, src_ref[r], buf, r, sem)
            cp.wait() if wait else cp.start()
        return c

    lax.fori_loop(0, buf.shape[0] // DMA_UNROLL, body, 0)


def _expert_kernel(te_ref, nv_ref, src_ref, nxt_ref, h_hbm, wgu_ref, bgu_ref, wdn_ref, bdn_ref, ys_ref, xbuf, sem):
    t = pl.program_id(0)
    nv = nv_ref[0]
    slot = t % 2

    @pl.when(t == 0)
    def _():
        _rows_in_flight(h_hbm, src_ref, xbuf.at[0], sem.at[0], False)

    @pl.when(t + 1 < nv)
    def _():
        _rows_in_flight(h_hbm, nxt_ref, xbuf.at[1 - slot], sem.at[1 - slot], False)

    @pl.when(t < nv)
    def _():
        _rows_in_flight(h_hbm, src_ref, xbuf.at[slot], sem.at[slot], True)
        f = wdn_ref.shape[1]
        gu = _dot(xbuf[slot].astype(BF16), wgu_ref[0]) + bgu_ref[0]
        gate = jnp.minimum(gu[:, :f], SWIGLU_LIMIT)
        up = jnp.clip(gu[:, f:], -SWIGLU_LIMIT, SWIGLU_LIMIT)
        act = (up + 1.0) * gate * _sigmoid(SWIGLU_ALPHA * gate)
        ys_ref[...] = _dot(act.astype(BF16), wdn_ref[0]) + bdn_ref[0]

    @pl.when(t >= nv)
    def _():
        ys_ref[...] = jnp.zeros_like(ys_ref)


def _experts(h, src, tile_expert, n_valid, wgu_bf, bgu, wdn_bf, bdn):
    d = h.shape[1]
    p_max = src.shape[0]
    ne, _, f2 = wgu_bf.shape
    f = f2 // 2
    tm = MOE_TILE
    n_tiles = p_max // tm
    return pl.pallas_call(
        _expert_kernel,
        grid_spec=pltpu.PrefetchScalarGridSpec(
            num_scalar_prefetch=2,
            grid=(n_tiles,),
            in_specs=[pl.BlockSpec((tm,), lambda t, te, nv: (t,), memory_space=pltpu.SMEM),
                      pl.BlockSpec((tm,), lambda t, te, nv: (jnp.minimum(t + 1, n_tiles - 1),), memory_space=pltpu.SMEM),
                      pl.BlockSpec(memory_space=pl.ANY),
                      pl.BlockSpec((1, d, f2), lambda t, te, nv: (te[t], 0, 0)),
                      pl.BlockSpec((1, 1, f2), lambda t, te, nv: (te[t], 0, 0)),
                      pl.BlockSpec((1, f, d), lambda t, te, nv: (te[t], 0, 0)),
                      pl.BlockSpec((1, 1, d), lambda t, te, nv: (te[t], 0, 0))],
            out_specs=pl.BlockSpec((tm, d), lambda t, te, nv: (t, 0)),
            scratch_shapes=[pltpu.VMEM((2, tm, d), F32), pltpu.SemaphoreType.DMA((2,))]),
        out_shape=jax.ShapeDtypeStruct((p_max, d), F32),
        compiler_params=_params(("arbitrary",)),
        name="moe_experts",
    )(tile_expert, n_valid, src, src, h, wgu_bf, bgu.reshape(ne, 1, f2), wdn_bf, bdn.reshape(ne, 1, d))


def _combine_kernel(dest_ref, x_ref, w_ref, mod_ref, fg_ref, ys_hbm, o_ref, buf, sem, *, final):
    tm = x_ref.shape[0]

    def start(r, c):
        for j in range(TOP_K):
            _row_copy(ys_hbm, dest_ref[r * TOP_K + j], buf.at[j], r, sem).start()
        return c

    def wait(r, c):
        for j in range(TOP_K):
            _row_copy(ys_hbm, dest_ref[r * TOP_K + j], buf.at[j], r, sem).wait()
        return c

    lax.fori_loop(0, tm, start, 0)
    lax.fori_loop(0, tm, wait, 0)
    w = w_ref[...]
    acc = w[:, 0:1] * buf[0]
    for j in range(1, TOP_K):
        acc = acc + w[:, j:j + 1] * buf[j]
    out = x_ref[...] + mod_ref[0][5:6] * acc
    if final:
        ms = jnp.mean(out * out, axis=-1, keepdims=True)
        out = out * lax.rsqrt(ms + EPS) * fg_ref[...]
    o_ref[...] = out


def _combine(dest_flat, x, wts, mod, final_g, ys, seq, n_rows, final):
    d = x.shape[1]
    tm = COMBINE_ROWS
    row = lambda i: (i, 0)
    return pl.pallas_call(
        functools.partial(_combine_kernel, final=final),
        grid=(n_rows // tm,),
        in_specs=[pl.BlockSpec((tm * TOP_K,), lambda i: (i,), memory_space=pltpu.SMEM),
                  pl.BlockSpec((tm, d), row),
                  pl.BlockSpec((tm, LANES), row),
                  pl.BlockSpec((1, 8, d), lambda i: (jnp.minimum((i * tm) // seq, 2), 0, 0)),
                  pl.BlockSpec((1, d), lambda i: (0, 0)),
                  pl.BlockSpec(memory_space=pl.ANY)],
        out_specs=pl.BlockSpec((tm, d), row),
        out_shape=jax.ShapeDtypeStruct((n_rows, d), F32),
        scratch_shapes=[pltpu.VMEM((TOP_K, tm, d), F32), pltpu.SemaphoreType.DMA(())],
        compiler_params=_params(("arbitrary",)),
        name="moe_combine",
    )(dest_flat, x, wts, mod, final_g, ys)


def _moe(h2, logits, x, mod, final_g, wgu_bf, bgu, wdn_bf, bdn, seq, final):
    n = h2.shape[0]
    tm = MOE_TILE
    idx, wts = _topk(logits)
    rank, counts = _rank(idx)
    cnt = counts[0, :N_EXPERTS].astype(jnp.int32)
    tiles_e = (cnt + tm - 1) // tm
    tile_end = jnp.cumsum(tiles_e)
    offs = (tile_end - tiles_e) * tm
    idx4, rank4 = idx[:, :TOP_K], rank[:, :TOP_K]
    dest = (offs[idx4] + rank4).reshape(-1)
    p_max = (n * TOP_K // tm + N_EXPERTS) * tm
    n_tiles = p_max // tm
    n_valid = tile_end[-1:]
    tile_id = jnp.minimum(jnp.arange(n_tiles, dtype=jnp.int32), n_valid[0] - 1)
    tile_expert = jnp.sum((tile_end[None, :] <= tile_id[:, None]).astype(jnp.int32), axis=1)
    tile_expert = jnp.minimum(tile_expert, N_EXPERTS - 1)
    tok = jnp.repeat(jnp.arange(n, dtype=jnp.int32), TOP_K)
    src = jnp.zeros((p_max,), jnp.int32).at[dest].set(tok)
    ys = _experts(h2, src, tile_expert, n_valid.astype(jnp.int32), wgu_bf, bgu, wdn_bf, bdn)
    return _combine(dest, x, wts, mod, final_g, ys, seq, n, final)


def _hy_in_kernel(x_ref, mod_ref, ng_ref, w0, w1, w2, bin_ref, sw_ref, sb_ref, u_o, x0_o, h_scr, *, n_lat_tiles, ctx_len):
    i = pl.program_id(0)

    @pl.when(pl.program_id(1) == 0)
    def _():
        m = mod_ref[0]
        h_scr[...] = _rms_mod(x_ref[...], ng_ref[...], m[0:1], m[1:2]).astype(BF16)

    h = h_scr[...]
    tm, tn = u_o.shape
    row = lax.broadcasted_iota(jnp.int32, (tm, tn), 0)
    seg = jnp.where(i >= n_lat_tiles, ctx_len, GRID_W)
    pos = row & (seg - 1)
    first = pos == 0
    last = pos == seg - 1
    bin_ = bin_ref[...]
    sw = sw_ref[...]
    sb = sb_ref[...]
    outs = []
    for s, w in enumerate((w0, w1, w2)):
        z = _dot(h, w[...]) + bin_[s:s + 1]
        zp = jnp.where(first, 0.0, pltpu.roll(z, 1, axis=0))
        zn = jnp.where(last, 0.0, pltpu.roll(z, tm - 1, axis=0))
        outs.append(sw[0, s:s + 1] * zp + sw[1, s:s + 1] * z + sw[2, s:s + 1] * zn + sb[s:s + 1])
    x0_o[...] = outs[0]
    u_o[...] = outs[2] * outs[1]


def _hy_in(x, mod, ng, w_bf, b_in, short_w, short_b, seq, ctx_len, n_rows):
    d = x.shape[1]
    tm, tn = 512, 256
    nj = d // tn
    wspec = lambda s: pl.BlockSpec((d, tn), lambda i, j, s=s: (0, s * nj + j))
    ospec = pl.BlockSpec((tm, tn), lambda i, j: (i, j))
    osh = jax.ShapeDtypeStruct((n_rows, d), F32)
    return pl.pallas_call(
        functools.partial(_hy_in_kernel, n_lat_tiles=2 * seq // tm, ctx_len=ctx_len),
        grid=(n_rows // tm, nj),
        in_specs=[pl.BlockSpec((tm, d), lambda i, j: (i, 0)),
                  pl.BlockSpec((1, 8, d), lambda i, j: (jnp.minimum((i * tm) // seq, 2), 0, 0)),
                  pl.BlockSpec((1, d), lambda i, j: (0, 0)),
                  wspec(0), wspec(1), wspec(2),
                  pl.BlockSpec((3, tn), lambda i, j: (0, j)),
                  pl.BlockSpec((3, 3, tn), lambda i, j: (0, 0, j)),
                  pl.BlockSpec((3, tn), lambda i, j: (0, j))],
        out_specs=[ospec, ospec],
        out_shape=[osh, osh],
        scratch_shapes=[pltpu.VMEM((tm, d), BF16)],
        compiler_params=_params(("arbitrary", "arbitrary")),
        name="hyena_in_proj",
    )(x, mod, ng, w_bf, w_bf, w_bf, b_in.reshape(3, d), short_w.reshape(3, 3, d), short_b.reshape(3, d))


def _filter_kernel(z_ref, w1, b1, fr, w2, b2, w3, b3, w4, dl_ref, f_o, nrm_o, *, length):
    i = pl.program_id(0)
    z = z_ref[...]
    freq = fr[...]
    a = jnp.sin(freq * (_dot(z, w1[...]) + b1[...]))
    a = jnp.sin(freq * (_dot(a, w2[...]) + b2[...]))
    a = jnp.sin(freq * (_dot(a, w3[...]) + b3[...]))
    h = _dot(a, w4[...]) * jnp.exp(-z[:, 0:1] * dl_ref[...])
    row = lax.broadcasted_iota(jnp.int32, h.shape, 0) + i * h.shape[0]
    h = jnp.where(row == length, 0.0, h)
    f_o[...] = h

    @pl.when(i == 0)
    def _():
        nrm_o[...] = jnp.zeros_like(nrm_o)

    nrm_o[...] = nrm_o[...] + jnp.sum(jnp.abs(h), axis=0, keepdims=True)


def _pos_features(length):
    t = (np.arange(length, dtype=np.float64) / length)[:, None]
    bands = np.linspace(1e-4, HY_BANDS - 1, HY_BANDS)[None, :]
    ang = 2.0 * math.pi * t * bands
    return np.concatenate([t, np.cos(ang), -np.sin(ang)], axis=-1).astype(np.float32)


def _decay_rates(d):
    return np.abs(np.linspace(math.log(HY_DECAY_PCT_MIN) / HY_DECAY_TARGET,
                              math.log(HY_DECAY_PCT_MAX) / HY_DECAY_TARGET, d)).astype(np.float32)[None, :]


def _hyena_filter(length, w1, b1, freq, w2, b2, w3, b3, w4):
    d = w4.shape[1] // 2
    tr = min(length, 512)
    nf = length // tr
    feats = _pos_features(length)
    feats = np.concatenate([feats, feats[:1], feats[:0:-1]], axis=0)
    z = jnp.asarray(np.pad(feats, ((0, 0), (0, LANES - feats.shape[1]))))
    const = lambda i: (0, 0)
    full = lambda a: pl.BlockSpec(a.shape, const)
    mat = lambda a: jnp.pad(a, ((0, LANES - a.shape[0]), (0, LANES - a.shape[1])))
    vec = lambda a: jnp.pad(a, (0, LANES - a.shape[0])).reshape(1, LANES)
    w4p = jnp.pad(w4, ((0, LANES - w4.shape[0]), (0, 0)))
    args = [z, mat(w1), vec(b1), vec(freq), mat(w2), vec(b2), mat(w3), vec(b3), w4p, jnp.asarray(_decay_rates(d))]
    in_specs = [pl.BlockSpec((tr, LANES), lambda i: (i, 0))] + [full(a) for a in args[1:]]
    in_specs[8] = pl.BlockSpec((LANES, d), lambda i: (0, i // nf))
    return pl.pallas_call(
        functools.partial(_filter_kernel, length=length),
        grid=(2 * nf,),
        in_specs=in_specs,
        out_specs=[pl.BlockSpec((tr, d), lambda i: (i, 0)), pl.BlockSpec((1, d), const)],
        out_shape=[jax.ShapeDtypeStruct((2 * length, d), F32), jax.ShapeDtypeStruct((1, d), F32)],
        compiler_params=_params(("arbitrary",)),
        name="hyena_filter",
    )(*args)


def _dft_mats(m, k, n):
    r = np.arange(m, dtype=np.int64)[:, None]
    c = np.arange(k, dtype=np.int64)[None, :]
    ang = 2.0 * math.pi * ((r * c) % n).astype(np.float64) / n
    return np.cos(ang).astype(np.float32), np.sin(ang).astype(np.float32)


def _cmul_mat(c, s, xr, xi, conj):
    xr = _split(xr)
    cr, sr = _dot3(c, xr), _dot3(s, xr)
    if xi is None:
        return cr, (sr if conj else -sr)
    xi = _split(xi)
    ci, si = _dot3(c, xi), _dot3(s, xi)
    if conj:
        return cr - si, ci + sr
    return cr + si, ci - sr


def _slab_fwd_kernel(*refs, has_imag, has_filter):
    refs = list(refs)
    fc, fs, xr_ref = refs[:3]
    xi_ref = refs[3] if has_imag else None
    rest = refs[3 + has_imag:]
    if has_filter:
        gr_ref, gi_ref, gn_ref, or_ref, oi_ref = rest
    else:
        or_ref, oi_ref = rest
    yr, yi = _cmul_mat(_split(fc[...]), _split(fs[...]), xr_ref[...], xi_ref[...] if has_imag else None, False)
    if has_filter:
        gn = 1.0 / gn_ref[...]
        gr, gi = gr_ref[...] * gn, gi_ref[...] * gn
        yr, yi = yr * gr - yi * gi, yr * gi + yi * gr
    or_ref[...] = yr
    oi_ref[...] = yi


def _slab_fwd(x2, k_in, parts, n_out, n, spectrum=None):
    cols = x2.shape[1]
    tc = min(cols, 2048 if n_out <= LANES else 512)
    fc, fs = _dft_mats(n_out, k_in, n)
    const = lambda c: (0, 0)
    in_specs = [pl.BlockSpec((n_out, k_in), const), pl.BlockSpec((n_out, k_in), const)]
    args = [jnp.asarray(fc), jnp.asarray(fs)]
    for part in parts:
        if part is not None:
            in_specs.append(pl.BlockSpec((k_in, tc), lambda c, part=part: (part, c)))
            args.append(x2)
    ospec = pl.BlockSpec((n_out, tc), lambda c: (0, c))
    if spectrum is not None:
        in_specs += [ospec, ospec, pl.BlockSpec((1, tc), lambda c: (0, c))]
        args += list(spectrum)
    osh = jax.ShapeDtypeStruct((n_out, cols), F32)
    return pl.pallas_call(
        functools.partial(_slab_fwd_kernel, has_imag=parts[1] is not None, has_filter=spectrum is not None),
        grid=(cols // tc,),
        in_specs=in_specs,
        out_specs=[ospec, ospec],
        out_shape=[osh, osh],
        compiler_params=_params(("arbitrary",)),
        name="dft_slab_fwd",
    )(*args)


def _mid_kernel(*refs, has_filter):
    if has_filter:
        fc, fs, twc_ref, tws_ref, ar_ref, ai_ref, gr_ref, gi_ref, gn_ref, or_ref, oi_ref = refs
    else:
        fc, fs, twc_ref, tws_ref, ar_ref, ai_ref, or_ref, oi_ref = refs
    td = ar_ref.shape[2]
    c, s = _split(fc[...]), _split(fs[...])
    twc = jnp.concatenate([twc_ref[0]] * (td // LANES), axis=1)
    tws = jnp.concatenate([tws_ref[0]] * (td // LANES), axis=1)
    ar, ai = ar_ref[0], ai_ref[0]
    pr, pi_ = ar * twc + ai * tws, ai * twc - ar * tws
    br, bi = _cmul_mat(c, s, pr, pi_, False)
    if not has_filter:
        or_ref[0] = br
        oi_ref[0] = bi
        return
    gn = 1.0 / gn_ref[...]
    gr, gi = gr_ref[0] * gn, gi_ref[0] * gn
    yr, yi = br * gr - bi * gi, br * gi + bi * gr
    qr, qi = _cmul_mat(c, s, yr, yi, True)
    or_ref[0] = qr * twc - qi * tws
    oi_ref[0] = qr * tws + qi * twc


def _mid_stage(a_re, a_im, n, spectrum=None):
    n1, _, d = a_re.shape
    td = 512
    fc, fs = _dft_mats(LANES, LANES, LANES)
    k1 = lax.broadcasted_iota(jnp.int32, (n1, LANES, LANES), 0)
    m2 = lax.broadcasted_iota(jnp.int32, (n1, LANES, LANES), 1)
    ang = ((k1 * m2) % n).astype(F32) * (2.0 * math.pi / n)
    twc, tws = jnp.cos(ang), jnp.sin(ang)
    const = lambda s, j: (0, 0)
    slab = pl.BlockSpec((1, LANES, td), lambda s, j: (s, 0, j))
    tw = pl.BlockSpec((1, LANES, LANES), lambda s, j: (s, 0, 0))
    in_specs = [pl.BlockSpec((LANES, LANES), const), pl.BlockSpec((LANES, LANES), const), tw, tw, slab, slab]
    args = [jnp.asarray(fc), jnp.asarray(fs), twc, tws, a_re, a_im]
    if spectrum is not None:
        in_specs += [slab, slab, pl.BlockSpec((1, td), lambda s, j: (0, j))]
        args += list(spectrum)
    osh = jax.ShapeDtypeStruct((n1, LANES, d), F32)
    return pl.pallas_call(
        functools.partial(_mid_kernel, has_filter=spectrum is not None),
        grid=(n1, d // td),
        in_specs=in_specs,
        out_specs=[slab, slab],
        out_shape=[osh, osh],
        compiler_params=_params(("arbitrary", "arbitrary")),
        name="dft_mid",
    )(*args)


def _slab_inv_kernel(fc, fs, pr_ref, pi_ref, u0_ref, u1_ref, a0_ref, a1_ref, skip_ref, o_ref, *, scale):
    yr, yi = _cmul_mat(_split(fc[...]), _split(fs[...]), pr_ref[...], pi_ref[...], True)
    skip = skip_ref[...]
    o_ref[0] = ((yr * scale + u0_ref[...] * skip) * a0_ref[...]).astype(o_ref.dtype)
    o_ref[1] = ((yi * scale + u1_ref[...] * skip) * a1_ref[...]).astype(o_ref.dtype)


def _slab_inv(p_re, p_im, u2, x02, k_out, parts, skip, n, scale):
    n_in, cols = p_re.shape
    d = skip.shape[1]
    tc = min(cols, d if n_in <= LANES else 512)
    fc, fs = _dft_mats(k_out, n_in, n)
    const = lambda c: (0, 0)
    pspec = pl.BlockSpec((n_in, tc), lambda c: (0, c))
    uspec = lambda part: pl.BlockSpec((k_out, tc), lambda c, part=part: (part, c))
    return pl.pallas_call(
        functools.partial(_slab_inv_kernel, scale=scale),
        grid=(cols // tc,),
        in_specs=[pl.BlockSpec((k_out, n_in), const), pl.BlockSpec((k_out, n_in), const),
                  pspec, pspec, uspec(parts[0]), uspec(parts[1]), uspec(parts[0]), uspec(parts[1]),
                  pl.BlockSpec((1, tc), lambda c: (0, c % (d // tc)))],
        out_specs=pl.BlockSpec((2, k_out, tc), lambda c: (0, 0, c)),
        out_shape=jax.ShapeDtypeStruct((2, k_out, cols), BF16),
        compiler_params=_params(("arbitrary",)),
        name="dft_slab_inv",
    )(jnp.asarray(fc), jnp.asarray(fs), p_re, p_im, u2, u2, x02, x02, skip)


def _long_conv(u, x0, row0, length, filt, nrm, skip):
    d = u.shape[1]
    n = 2 * length
    if n <= 4 * LANES:
        parts = (row0 // length, row0 // length + 1)
        g_re, g_im = _slab_fwd(filt, n, (0, None), n, n)
        y_re, y_im = _slab_fwd(u, length, parts, n, n, spectrum=(g_re, g_im, nrm))
        y = _slab_inv(y_re, y_im, u, x0, length, parts, skip, n, 1.0 / n)
        return y.reshape(2 * length, d)
    n1 = n // LANES
    cols = LANES * d
    k1 = n1 // 2
    parts = (row0 // length, row0 // length + 1)
    f_re, f_im = _slab_fwd(filt.reshape(n1, cols), n1, (0, None), n1, n1)
    g_re, g_im = _mid_stage(f_re.reshape(n1, LANES, d), f_im.reshape(n1, LANES, d), n)
    u2 = u.reshape(u.shape[0] // LANES, cols)
    a_re, a_im = _slab_fwd(u2, k1, parts, n1, n1)
    p_re, p_im = _mid_stage(a_re.reshape(n1, LANES, d), a_im.reshape(n1, LANES, d), n, spectrum=(g_re, g_im, nrm))
    y = _slab_inv(p_re.reshape(n1, cols), p_im.reshape(n1, cols), u2, x0.reshape(u2.shape), k1, parts, skip,
                  n1, 1.0 / n)
    return y.reshape(2 * length, d)


def kernel(x, c, ctx, c_ctx, ada_w, ada_b, norm_mix_g, norm_ffn_g, final_norm_g, hg_w_in, hg_lb, hg_onorm_g, hg_w_out, hy_w_in, hy_b_in, hy_short_w, hy_short_b, hy_f_w1, hy_f_b1, hy_f_freq, hy_f_w2, hy_f_b2, hy_f_w3, hy_f_b3, hy_f_w4, hy_skip, hy_w_out, hy_b_out, moe_w_router, moe_b_router, moe_w_gu, moe_b_gu, moe_w_down, moe_b_down):
    bsz, seq, d = x.shape
    ctx_len = ctx.shape[1]
    depth = ada_w.shape[0]
    n_lat = bsz * seq
    n_all = n_lat + bsz * ctx_len
    assert bsz == 2 and seq % MOE_TILE == 0 and ctx_len == GLA_ROWS

    cond8 = jnp.zeros((8, d), F32).at[0:2].set(c).at[2].set(c_ctx)
    mods = _ada_all(cond8, ada_w, ada_b).reshape(depth, 8, N_MOD, d)[:, :3]
    mods = jnp.pad(mods, ((0, 0), (0, 0), (0, 8 - N_MOD), (0, 0)))

    lb_soft = jax.nn.softmax(hg_lb.astype(F32), axis=0)
    lower_bounds = jnp.cumsum(lb_soft, axis=0) - lb_soft[0]

    xa = jnp.concatenate([x.reshape(n_lat, d), ctx.reshape(bsz * ctx_len, d)], axis=0)
    zero_bias = jnp.zeros((1, d), F32)
    final_g = final_norm_g.reshape(1, d)
    ctx_needed = [any(l % 2 == 0 for l in range(layer + 1, depth)) for layer in range(depth)]

    for layer in range(depth):
        j = layer // 2
        is_hgrn = layer % 2 == 0
        keep_ctx = ctx_needed[layer]
        n_out = n_all if keep_ctx else n_lat
        mod = mods[layer]
        ng1 = norm_mix_g[layer].reshape(1, d)
        ng2 = norm_ffn_g[layer].reshape(1, d)
        if is_hgrn:
            q, kf, lff, kb, lfb, v, g = _hg_in(xa, mod, ng1, lower_bounds[j].reshape(1, d),
                                               hg_w_in[j].astype(BF16), seq, n_all)
            o_fw = _gla_pass(q, kf, v, lff, seq, ctx_len, False)
            y = _gla_pass(q, kb, v, lfb, seq, ctx_len, True,
                          extra=(o_fw, g, hg_onorm_g[j].reshape(1, HG_DK)))
            w_out, b_out = hg_w_out[j].astype(BF16), zero_bias
        else:
            n_in = n_all if keep_ctx else n_lat
            u, x0 = _hy_in(xa, mod, ng1, hy_w_in[j].astype(BF16), hy_b_in[j], hy_short_w[j], hy_short_b[j],
                           seq, ctx_len, n_in)
            fpar = (hy_f_w1[j], hy_f_b1[j], hy_f_freq[j], hy_f_w2[j], hy_f_b2[j], hy_f_w3[j], hy_f_b3[j], hy_f_w4[j])
            skip = hy_skip[j].reshape(1, d)
            y = _long_conv(u, x0, 0, seq, *_hyena_filter(seq, *fpar), skip)
            if keep_ctx:
                yc = _long_conv(u, x0, n_lat, ctx_len, *_hyena_filter(ctx_len, *fpar), skip)
                y = jnp.concatenate([y, yc], axis=0)
            w_out, b_out = hy_w_out[j].astype(BF16), hy_b_out[j].reshape(1, d)
        wr = jnp.pad(moe_w_router[layer], ((0, 0), (0, LANES - N_EXPERTS)))
        br = jnp.pad(moe_b_router[layer], (0, LANES - N_EXPERTS)).reshape(1, LANES)
        xa, h2, logits = _out_proj(y, w_out, b_out, xa, mod, ng2, wr, br, seq, n_out)
        xa = _moe(h2, logits, xa, mod, final_g, moe_w_gu[layer].astype(BF16), moe_b_gu[layer],
                  moe_w_down[layer].astype(BF16), moe_b_down[layer], seq, layer == depth - 1)
    return xa.reshape(bsz, seq, d)
```

```python
import functools
import math

import numpy as np
import jax
import jax.numpy as jnp
from jax import lax
from jax.experimental import pallas as pl
from jax.experimental.pallas import tpu as pltpu

F32 = jnp.float32
BF16 = jnp.bfloat16

EPS = 1e-6
N_MOD = 6
LANES = 128
GRID_W = 64
HG_DK = 128
GLA_CHUNK = 128
GLA_ROWS = 256
GLA_HEADS = 2
HY_BANDS = 16
HY_DECAY_PCT_MIN = 0.3
HY_DECAY_PCT_MAX = 1.5
HY_DECAY_TARGET = 1e-2
N_EXPERTS = 32
TOP_K = 4
SWIGLU_ALPHA = 1.702
SWIGLU_LIMIT = 7.0
MOE_TILE = 512
VMEM_LIMIT = 56 * 1024 * 1024


def _params(sem, vmem=VMEM_LIMIT):
    return pltpu.CompilerParams(dimension_semantics=sem, vmem_limit_bytes=vmem)


def _sigmoid(x):
    return 1.0 / (1.0 + jnp.exp(-x))


def _dot(a, b):
    return jnp.dot(a, b, preferred_element_type=F32)


def _dot_nt(a, b):
    return lax.dot_general(a, b, (((1,), (1,)), ((), ())), preferred_element_type=F32)


def _split(a):
    hi = a.astype(BF16)
    lo = (a - hi.astype(F32)).astype(BF16)
    return hi, lo


def _dot3(a, b):
    (ah, al), (bh, bl) = a, b
    return _dot(ah, bh) + _dot(ah, bl) + _dot(al, bh)


def _rms_mod(x, g, shift, scale):
    ms = jnp.mean(x * x, axis=-1, keepdims=True)
    return x * lax.rsqrt(ms + EPS) * g * (1.0 + scale) + shift


def _ada_kernel(c_ref, w_ref, b_ref, o_ref):
    c = c_ref[...]
    o_ref[0] = _dot(c * _sigmoid(c), w_ref[0]) + b_ref[0]


def _ada_all(cond8, ada_w, ada_b):
    depth, d, nd = ada_w.shape
    tn = 1024
    return pl.pallas_call(
        _ada_kernel,
        grid=(depth, nd // tn),
        in_specs=[pl.BlockSpec((8, d), lambda l, j: (0, 0)),
                  pl.BlockSpec((1, d, tn), lambda l, j: (l, 0, j)),
                  pl.BlockSpec((1, 1, tn), lambda l, j: (l, 0, j))],
        out_specs=pl.BlockSpec((1, 8, tn), lambda l, j: (l, 0, j)),
        out_shape=jax.ShapeDtypeStruct((depth, 8, nd), F32),
        compiler_params=_params(("arbitrary", "arbitrary")),
        name="ada_mod",
    )(cond8, ada_w, ada_b.reshape(depth, 1, nd))


def _hg_in_kernel(x_ref, mod_ref, ng_ref, lb_ref, wq, wf, wb, wi, wg,
                  q_o, kf_o, lff_o, kb_o, lfb_o, v_o, g_o, h_scr):
    @pl.when(pl.program_id(1) == 0)
    def _():
        m = mod_ref[0]
        h_scr[...] = _rms_mod(x_ref[...], ng_ref[...], m[0:1], m[1:2]).astype(BF16)

    h = h_scr[...]
    q = _dot(h, wq[...])
    q_o[...] = q * _sigmoid(q) * (HG_DK ** -0.5)
    lb = lb_ref[...]
    for w, k_o, lf_o in ((wf, kf_o, lff_o), (wb, kb_o, lfb_o)):
        f = lb + (1.0 - lb) * _sigmoid(_dot(h, w[...]))
        k_o[...] = 1.0 - f
        lf_o[...] = jnp.log(f)
    v_o[...] = _dot(h, wi[...])
    g_o[...] = _dot(h, wg[...])


def _hg_in(x, mod, ng, lb, w_bf, seq, n_rows):
    d = x.shape[1]
    tm, tn = 512, 256
    nj = d // tn
    grp = lambda i, j: (jnp.minimum((i * tm) // seq, 2), 0, 0)
    wspec = lambda s: pl.BlockSpec((d, tn), lambda i, j, s=s: (0, s * nj + j))
    ospec = pl.BlockSpec((tm, tn), lambda i, j: (i, j))
    osh = jax.ShapeDtypeStruct((n_rows, d), F32)
    return pl.pallas_call(
        _hg_in_kernel,
        grid=(n_rows // tm, nj),
        in_specs=[pl.BlockSpec((tm, d), lambda i, j: (i, 0)),
                  pl.BlockSpec((1, 8, d), grp),
                  pl.BlockSpec((1, d), lambda i, j: (0, 0)),
                  pl.BlockSpec((1, tn), lambda i, j: (0, j))] + [wspec(s) for s in range(5)],
        out_specs=[ospec] * 7,
        out_shape=[osh] * 7,
        scratch_shapes=[pltpu.VMEM((tm, d), BF16)],
        compiler_params=_params(("arbitrary", "arbitrary")),
        name="hgrn_in_proj",
    )(x, mod, ng, lb, w_bf, w_bf, w_bf, w_bf, w_bf)


def _level_map(c, reverse):
    i = np.arange(c)[:, None]
    j = np.arange(c)[None, :]
    x = i ^ j
    lvl = np.where(x > 0, np.floor(np.log2(np.maximum(x, 1))), -1).astype(np.int32)
    side = (i < j) if reverse else (i > j)
    return np.where(side, lvl, -1).astype(np.int32)


def _cumsum_rows(x, row, c, reverse):
    sh = 1
    while sh < c:
        if reverse:
            x = x + jnp.where(row < c - sh, pltpu.roll(x, c - sh, axis=0), 0.0)
        else:
            x = x + jnp.where(row >= sh, pltpu.roll(x, sh, axis=0), 0.0)
        sh *= 2
    return x


def _gla_chunk(q, k, v, lf, st, lvl, row, reverse):
    c, wd = q.shape
    nh = wd // HG_DK
    b = _cumsum_rows(lf, row, c, reverse)
    lf_up = pltpu.roll(lf, c - 1, axis=0)
    lf_dn = pltpu.roll(lf, 1, axis=0)
    att = [jnp.zeros((c, c), F32) for _ in range(nh)]
    p = 0
    while (1 << p) < c:
        s = 1 << p
        is_query = ((row & s) == 0) if reverse else ((row & s) != 0)
        if s == 1:
            a = jnp.where(is_query, lf, 0.0)
        elif s == 2:
            m = row & 3
            if reverse:
                a = jnp.where(m == 0, lf + lf_up, jnp.where(m == 1, lf, jnp.where(m == 2, 0.0, lf_dn)))
            else:
                a = jnp.where(m == 0, lf_up, jnp.where(m == 1, 0.0, jnp.where(m == 2, lf, lf + lf_dn)))
        else:
            nb = c // (2 * s)
            r = s if reverse else s - 1
            b3 = b.reshape(nb, 2 * s, wd)
            br = jnp.broadcast_to(b3[:, r:r + 1, :], (nb, 2 * s, wd)).reshape(c, wd)
            a = -jnp.abs(b - br)
        z = (jnp.where(is_query, q, k) * jnp.exp(a)).astype(BF16)
        for h in range(nh):
            zh = z[:, h * HG_DK:(h + 1) * HG_DK]
            att[h] = jnp.where(lvl == p, _dot_nt(zh, zh), att[h])
        p += 1
    tot = b[0:1] if reverse else b[c - 1:c]
    qe = (q * jnp.exp(b)).astype(BF16)
    kd = (k * jnp.exp(tot - b)).astype(BF16)
    dec = jnp.exp(tot)
    outs, new_st = [], []
    for h in range(nh):
        cs = slice(h * HG_DK, (h + 1) * HG_DK)
        vh = v[:, cs]
        dsum = jnp.sum(q[:, cs] * k[:, cs], axis=-1, keepdims=True)
        o = _dot(att[h].astype(BF16), vh.astype(BF16)) + dsum * vh
        o = o + _dot_nt(qe[:, cs], st[h].astype(BF16))
        outs.append(o)
        new_st.append(st[h] * dec[:, cs] + _dot(vh.T.astype(BF16), kd[:, cs]))
    return jnp.concatenate(outs, axis=1), new_st


def _gla_kernel(*refs, reverse, gated):
    if gated:
        q_ref, k_ref, v_ref, lf_ref, lvl_ref, ofw_ref, g_ref, on_ref, y_ref, s_scr = refs
    else:
        q_ref, k_ref, v_ref, lf_ref, lvl_ref, y_ref, s_scr = refs
    c = GLA_CHUNK
    rows, wd = q_ref.shape
    nh = wd // HG_DK
    nchunk = rows // c

    @pl.when(pl.program_id(2) == 0)
    def _():
        s_scr[...] = jnp.zeros_like(s_scr)

    lvl = lvl_ref[...]
    row = lax.broadcasted_iota(jnp.int32, (c, wd), 0)

    def body(i, carry):
        ci = (nchunk - 1 - i) if reverse else i
        rs = pl.ds(pl.multiple_of(ci * c, c), c)
        st = [s_scr[h] for h in range(nh)]
        o, st = _gla_chunk(q_ref[rs, :], k_ref[rs, :], v_ref[rs, :], lf_ref[rs, :], st, lvl, row, reverse)
        for h in range(nh):
            s_scr[h] = st[h]
        if gated:
            o = o + ofw_ref[rs, :]
            g = g_ref[rs, :]
            on = on_ref[...]
            ys = []
            for h in range(nh):
                oh = o[:, h * HG_DK:(h + 1) * HG_DK]
                ms = jnp.mean(oh * oh, axis=-1, keepdims=True)
                ys.append(oh * lax.rsqrt(ms + EPS) * on)
            y = jnp.concatenate(ys, axis=1) * (g * _sigmoid(g))
            y_ref[rs, :] = y.astype(y_ref.dtype)
        else:
            y_ref[rs, :] = o
        return carry

    lax.fori_loop(0, nchunk, body, 0)


def _gla_pass(q, k, v, lf, seq, ctx_len, reverse, extra=None):
    n_all, d = q.shape
    rows, wd = GLA_ROWS, GLA_HEADS * HG_DK
    nlat, nctx = seq // rows, ctx_len // rows
    ctx_base = 2 * nlat

    def rmap(b, hh, t):
        tc = (nctx - 1 - t) if reverse else t
        tl = (nlat - 1 - (t - nctx)) if reverse else (t - nctx)
        return (jnp.where(t < nctx, ctx_base + b * nctx + tc, b * nlat + tl), hh)

    blk = pl.BlockSpec((rows, wd), rmap)
    lvl = jnp.asarray(_level_map(GLA_CHUNK, reverse))
    in_specs = [blk, blk, blk, blk, pl.BlockSpec((GLA_CHUNK, GLA_CHUNK), lambda b, hh, t: (0, 0))]
    args = [q, k, v, lf, lvl]
    gated = extra is not None
    if gated:
        in_specs += [blk, blk, pl.BlockSpec((1, HG_DK), lambda b, hh, t: (0, 0))]
        args += list(extra)
    return pl.pallas_call(
        functools.partial(_gla_kernel, reverse=reverse, gated=gated),
        grid=(2, d // wd, nlat + nctx),
        in_specs=in_specs,
        out_specs=blk,
        out_shape=jax.ShapeDtypeStruct((n_all, d), BF16 if gated else F32),
        scratch_shapes=[pltpu.VMEM((GLA_HEADS, HG_DK, HG_DK), F32)],
        compiler_params=_params(("arbitrary", "arbitrary", "arbitrary")),
        name="gla_scan_bwd" if reverse else "gla_scan_fwd",
    )(*args)


def _out_kernel(y_ref, w_ref, b_ref, x_ref, mod_ref, ng_ref, wr_ref, br_ref, xo_ref, h2_ref, lg_ref):
    m = mod_ref[0]
    xn = x_ref[...] + m[2:3] * (_dot(y_ref[...], w_ref[...]) + b_ref[...])
    xo_ref[...] = xn
    h2 = _rms_mod(xn, ng_ref[...], m[3:4], m[4:5])
    h2_ref[...] = h2
    lg_ref[...] = _dot3(_split(h2), _split(wr_ref[...])) + br_ref[...]


def _out_proj(y, w_bf, bias, x, mod, ng, wr_pad, br_pad, seq, n_rows):
    d = x.shape[1]
    tm = 256
    row = lambda i: (i, 0)
    const = lambda i: (0, 0)
    return pl.pallas_call(
        _out_kernel,
        grid=(n_rows // tm,),
        in_specs=[pl.BlockSpec((tm, d), row),
                  pl.BlockSpec((d, d), const),
                  pl.BlockSpec((1, d), const),
                  pl.BlockSpec((tm, d), row),
                  pl.BlockSpec((1, 8, d), lambda i: (jnp.minimum((i * tm) // seq, 2), 0, 0)),
                  pl.BlockSpec((1, d), const),
                  pl.BlockSpec((d, LANES), const),
                  pl.BlockSpec((1, LANES), const)],
        out_specs=[pl.BlockSpec((tm, d), row), pl.BlockSpec((tm, d), row), pl.BlockSpec((tm, LANES), row)],
        out_shape=[jax.ShapeDtypeStruct((n_rows, d), F32), jax.ShapeDtypeStruct((n_rows, d), F32),
                   jax.ShapeDtypeStruct((n_rows, LANES), F32)],
        compiler_params=_params(("arbitrary",)),
        name="out_proj",
    )(y, w_bf, bias, x, mod, ng, wr_pad, br_pad)


def _topk_kernel(lg_ref, idx_ref, w_ref):
    l = lg_ref[...]
    lane = lax.broadcasted_iota(jnp.int32, l.shape, 1).astype(F32)
    l = jnp.where(lane < N_EXPERTS, l, -jnp.inf)
    idx_out = jnp.zeros(l.shape, F32)
    vals = []
    for r in range(TOP_K):
        m = jnp.max(l, axis=-1, keepdims=True)
        sel = jnp.min(jnp.where(l == m, lane, float(LANES)), axis=-1, keepdims=True)
        idx_out = jnp.where(lane == r, sel, idx_out)
        l = jnp.where(lane == sel, -jnp.inf, l)
        vals.append(m)
    es = [jnp.exp(v - vals[0]) for v in vals]
    tot = es[0] + es[1] + es[2] + es[3]
    w_out = jnp.zeros(l.shape, F32)
    for r in range(TOP_K):
        w_out = jnp.where(lane == r, es[r] / tot, w_out)
    idx_ref[...] = idx_out.astype(jnp.int32)
    w_ref[...] = w_out


def _topk(logits):
    n = logits.shape[0]
    tm = 512
    spec = pl.BlockSpec((tm, LANES), lambda i: (i, 0))
    return pl.pallas_call(
        _topk_kernel,
        grid=(n // tm,),
        in_specs=[spec],
        out_specs=[spec, spec],
        out_shape=[jax.ShapeDtypeStruct((n, LANES), jnp.int32), jax.ShapeDtypeStruct((n, LANES), F32)],
        compiler_params=_params(("arbitrary",)),
        name="router_topk",
    )(logits)


def _rank_kernel(idx_ref, rank_ref, cnt_ref, run_scr):
    @pl.when(pl.program_id(0) == 0)
    def _():
        run_scr[...] = jnp.zeros_like(run_scr)

    idx = idx_ref[...]
    tm = idx.shape[0]
    lane = lax.broadcasted_iota(jnp.int32, idx.shape, 1)
    onehots = [jnp.where(lane == idx[:, j:j + 1], 1.0, 0.0) for j in range(TOP_K)]
    esum = onehots[0] + onehots[1] + onehots[2] + onehots[3]
    ri = lax.broadcasted_iota(jnp.int32, (tm, tm), 0)
    ci = lax.broadcasted_iota(jnp.int32, (tm, tm), 1)
    before = jnp.where(ci < ri, 1.0, 0.0).astype(BF16)
    base = run_scr[...] + _dot(before, esum.astype(BF16))
    rank = jnp.zeros(idx.shape, F32)
    for j in range(TOP_K):
        rank = jnp.where(lane == j, jnp.sum(onehots[j] * base, axis=-1, keepdims=True), rank)
    rank_ref[...] = rank.astype(jnp.int32)
    run_scr[...] = run_scr[...] + jnp.sum(esum, axis=0, keepdims=True)
    cnt_ref[...] = run_scr[...]


def _rank(idx):
    n = idx.shape[0]
    tm = 512
    spec = pl.BlockSpec((tm, LANES), lambda i: (i, 0))
    return pl.pallas_call(
        _rank_kernel,
        grid=(n // tm,),
        in_specs=[spec],
        out_specs=[spec, pl.BlockSpec((1, LANES), lambda i: (0, 0))],
        out_shape=[jax.ShapeDtypeStruct((n, LANES), jnp.int32), jax.ShapeDtypeStruct((1, LANES), F32)],
        scratch_shapes=[pltpu.VMEM((1, LANES), F32)],
        compiler_params=_params(("arbitrary",)),
        name="router_rank",
    )(idx)


COMBINE_ROWS = 256


def _row_copy(src_hbm, s, dst, r, sem):
    return pltpu.make_async_copy(src_hbm.at[pl.ds(s, 1)], dst.at[pl.ds(r, 1)], sem)


DMA_UNROLL = 8


def _rows_in_flight(h_hbm, src_ref, buf, sem, wait):
    def body(i, c):
        for k in range(DMA_UNROLL):
            r = i * DMA_UNROLL + k
            cp = _row_copy(h_hbm, src_ref[r], buf, r, sem)
            cp.wait() if wait else cp.start()
        return c

    lax.fori_loop(0, buf.shape[0] // DMA_UNROLL, body, 0)


def _expert_kernel(te_ref, nv_ref, src_ref, nxt_ref, h_hbm, wgu_ref, bgu_ref, wdn_ref, bdn_ref, ys_ref, xbuf, sem):
    t = pl.program_id(0)
    nv = nv_ref[0]
    slot = t % 2

    @pl.when(t == 0)
    def _():
        _rows_in_flight(h_hbm, src_ref, xbuf.at[0], sem.at[0], False)

    @pl.when(t + 1 < nv)
    def _():
        _rows_in_flight(h_hbm, nxt_ref, xbuf.at[1 - slot], sem.at[1 - slot], False)

    @pl.when(t < nv)
    def _():
        _rows_in_flight(h_hbm, src_ref, xbuf.at[slot], sem.at[slot], True)
        f = wdn_ref.shape[1]
        gu = _dot(xbuf[slot].astype(BF16), wgu_ref[0]) + bgu_ref[0]
        gate = jnp.minimum(gu[:, :f], SWIGLU_LIMIT)
        up = jnp.clip(gu[:, f:], -SWIGLU_LIMIT, SWIGLU_LIMIT)
        act = (up + 1.0) * gate * _sigmoid(SWIGLU_ALPHA * gate)
        ys_ref[...] = _dot(act.astype(BF16), wdn_ref[0]) + bdn_ref[0]

    @pl.when(t >= nv)
    def _():
        ys_ref[...] = jnp.zeros_like(ys_ref)


def _experts(h, src, tile_expert, n_valid, wgu_bf, bgu, wdn_bf, bdn):
    d = h.shape[1]
    p_max = src.shape[0]
    ne, _, f2 = wgu_bf.shape
    f = f2 // 2
    tm = MOE_TILE
    n_tiles = p_max // tm
    return pl.pallas_call(
        _expert_kernel,
        grid_spec=pltpu.PrefetchScalarGridSpec(
            num_scalar_prefetch=2,
            grid=(n_tiles,),
            in_specs=[pl.BlockSpec((tm,), lambda t, te, nv: (t,), memory_space=pltpu.SMEM),
                      pl.BlockSpec((tm,), lambda t, te, nv: (jnp.minimum(t + 1, n_tiles - 1),), memory_space=pltpu.SMEM),
                      pl.BlockSpec(memory_space=pl.ANY),
                      pl.BlockSpec((1, d, f2), lambda t, te, nv: (te[t], 0, 0)),
                      pl.BlockSpec((1, 1, f2), lambda t, te, nv: (te[t], 0, 0)),
                      pl.BlockSpec((1, f, d), lambda t, te, nv: (te[t], 0, 0)),
                      pl.BlockSpec((1, 1, d), lambda t, te, nv: (te[t], 0, 0))],
            out_specs=pl.BlockSpec((tm, d), lambda t, te, nv: (t, 0)),
            scratch_shapes=[pltpu.VMEM((2, tm, d), F32), pltpu.SemaphoreType.DMA((2,))]),
        out_shape=jax.ShapeDtypeStruct((p_max, d), F32),
        compiler_params=_params(("arbitrary",)),
        name="moe_experts",
    )(tile_expert, n_valid, src, src, h, wgu_bf, bgu.reshape(ne, 1, f2), wdn_bf, bdn.reshape(ne, 1, d))


def _combine_kernel(dest_ref, x_ref, w_ref, mod_ref, fg_ref, ys_hbm, o_ref, buf, sem, *, final):
    tm = x_ref.shape[0]

    def start(r, c):
        for j in range(TOP_K):
            _row_copy(ys_hbm, dest_ref[r * TOP_K + j], buf.at[j], r, sem).start()
        return c

    def wait(r, c):
        for j in range(TOP_K):
            _row_copy(ys_hbm, dest_ref[r * TOP_K + j], buf.at[j], r, sem).wait()
        return c

    lax.fori_loop(0, tm, start, 0)
    lax.fori_loop(0, tm, wait, 0)
    w = w_ref[...]
    acc = w[:, 0:1] * buf[0]
    for j in range(1, TOP_K):
        acc = acc + w[:, j:j + 1] * buf[j]
    out = x_ref[...] + mod_ref[0][5:6] * acc
    if final:
        ms = jnp.mean(out * out, axis=-1, keepdims=True)
        out = out * lax.rsqrt(ms + EPS) * fg_ref[...]
    o_ref[...] = out


def _combine(dest_flat, x, wts, mod, final_g, ys, seq, n_rows, final):
    d = x.shape[1]
    tm = COMBINE_ROWS
    row = lambda i: (i, 0)
    return pl.pallas_call(
        functools.partial(_combine_kernel, final=final),
        grid=(n_rows // tm,),
        in_specs=[pl.BlockSpec((tm * TOP_K,), lambda i: (i,), memory_space=pltpu.SMEM),
                  pl.BlockSpec((tm, d), row),
                  pl.BlockSpec((tm, LANES), row),
                  pl.BlockSpec((1, 8, d), lambda i: (jnp.minimum((i * tm) // seq, 2), 0, 0)),
                  pl.BlockSpec((1, d), lambda i: (0, 0)),
                  pl.BlockSpec(memory_space=pl.ANY)],
        out_specs=pl.BlockSpec((tm, d), row),
        out_shape=jax.ShapeDtypeStruct((n_rows, d), F32),
        scratch_shapes=[pltpu.VMEM((TOP_K, tm, d), F32), pltpu.SemaphoreType.DMA(())],
        compiler_params=_params(("arbitrary",)),
        name="moe_combine",
    )(dest_flat, x, wts, mod, final_g, ys)


def _moe(h2, logits, x, mod, final_g, wgu_bf, bgu, wdn_bf, bdn, seq, final):
    n = h2.shape[0]
    tm = MOE_TILE
    idx, wts = _topk(logits)
    rank, counts = _rank(idx)
    cnt = counts[0, :N_EXPERTS].astype(jnp.int32)
    tiles_e = (cnt + tm - 1) // tm
    tile_end = jnp.cumsum(tiles_e)
    offs = (tile_end - tiles_e) * tm
    idx4, rank4 = idx[:, :TOP_K], rank[:, :TOP_K]
    dest = (offs[idx4] + rank4).reshape(-1)
    p_max = (n * TOP_K // tm + N_EXPERTS) * tm
    n_tiles = p_max // tm
    n_valid = tile_end[-1:]
    tile_id = jnp.minimum(jnp.arange(n_tiles, dtype=jnp.int32), n_valid[0] - 1)
    tile_expert = jnp.sum((tile_end[None, :] <= tile_id[:, None]).astype(jnp.int32), axis=1)
    tile_expert = jnp.minimum(tile_expert, N_EXPERTS - 1)
    tok = jnp.repeat(jnp.arange(n, dtype=jnp.int32), TOP_K)
    src = jnp.zeros((p_max,), jnp.int32).at[dest].set(tok)
    ys = _experts(h2, src, tile_expert, n_valid.astype(jnp.int32), wgu_bf, bgu, wdn_bf, bdn)
    return _combine(dest, x, wts, mod, final_g, ys, seq, n, final)


def _hy_in_kernel(x_ref, mod_ref, ng_ref, w0, w1, w2, bin_ref, sw_ref, sb_ref, u_o, x0_o, h_scr, *, n_lat_tiles, ctx_len):
    i = pl.program_id(0)

    @pl.when(pl.program_id(1) == 0)
    def _():
        m = mod_ref[0]
        h_scr[...] = _rms_mod(x_ref[...], ng_ref[...], m[0:1], m[1:2]).astype(BF16)

    h = h_scr[...]
    tm, tn = u_o.shape
    row = lax.broadcasted_iota(jnp.int32, (tm, tn), 0)
    seg = jnp.where(i >= n_lat_tiles, ctx_len, GRID_W)
    pos = row & (seg - 1)
    first = pos == 0
    last = pos == seg - 1
    bin_ = bin_ref[...]
    sw = sw_ref[...]
    sb = sb_ref[...]
    outs = []
    for s, w in enumerate((w0, w1, w2)):
        z = _dot(h, w[...]) + bin_[s:s + 1]
        zp = jnp.where(first, 0.0, pltpu.roll(z, 1, axis=0))
        zn = jnp.where(last, 0.0, pltpu.roll(z, tm - 1, axis=0))
        outs.append(sw[0, s:s + 1] * zp + sw[1, s:s + 1] * z + sw[2, s:s + 1] * zn + sb[s:s + 1])
    x0_o[...] = outs[0]
    u_o[...] = outs[2] * outs[1]


def _hy_in(x, mod, ng, w_bf, b_in, short_w, short_b, seq, ctx_len, n_rows):
    d = x.shape[1]
    tm, tn = 512, 256
    nj = d // tn
    wspec = lambda s: pl.BlockSpec((d, tn), lambda i, j, s=s: (0, s * nj + j))
    ospec = pl.BlockSpec((tm, tn), lambda i, j: (i, j))
    osh = jax.ShapeDtypeStruct((n_rows, d), F32)
    return pl.pallas_call(
        functools.partial(_hy_in_kernel, n_lat_tiles=2 * seq // tm, ctx_len=ctx_len),
        grid=(n_rows // tm, nj),
        in_specs=[pl.BlockSpec((tm, d), lambda i, j: (i, 0)),
                  pl.BlockSpec((1, 8, d), lambda i, j: (jnp.minimum((i * tm) // seq, 2), 0, 0)),
                  pl.BlockSpec((1, d), lambda i, j: (0, 0)),
                  wspec(0), wspec(1), wspec(2),
                  pl.BlockSpec((3, tn), lambda i, j: (0, j)),
                  pl.BlockSpec((3, 3, tn), lambda i, j: (0, 0, j)),
                  pl.BlockSpec((3, tn), lambda i, j: (0, j))],
        out_specs=[ospec, ospec],
        out_shape=[osh, osh],
        scratch_shapes=[pltpu.VMEM((tm, d), BF16)],
        compiler_params=_params(("arbitrary", "arbitrary")),
        name="hyena_in_proj",
    )(x, mod, ng, w_bf, w_bf, w_bf, b_in.reshape(3, d), short_w.reshape(3, 3, d), short_b.reshape(3, d))


def _filter_kernel(z_ref, w1, b1, fr, w2, b2, w3, b3, w4, dl_ref, f_o, nrm_o, *, length):
    i = pl.program_id(0)
    z = z_ref[...]
    freq = fr[...]
    a = jnp.sin(freq * (_dot(z, w1[...]) + b1[...]))
    a = jnp.sin(freq * (_dot(a, w2[...]) + b2[...]))
    a = jnp.sin(freq * (_dot(a, w3[...]) + b3[...]))
    h = _dot(a, w4[...]) * jnp.exp(-z[:, 0:1] * dl_ref[...])
    row = lax.broadcasted_iota(jnp.int32, h.shape, 0) + i * h.shape[0]
    h = jnp.where(row == length, 0.0, h)
    f_o[...] = h

    @pl.when(i == 0)
    def _():
        nrm_o[...] = jnp.zeros_like(nrm_o)

    nrm_o[...] = nrm_o[...] + jnp.sum(jnp.abs(h), axis=0, keepdims=True)


def _pos_features(length):
    t = (np.arange(length, dtype=np.float64) / length)[:, None]
    bands = np.linspace(1e-4, HY_BANDS - 1, HY_BANDS)[None, :]
    ang = 2.0 * math.pi * t * bands
    return np.concatenate([t, np.cos(ang), -np.sin(ang)], axis=-1).astype(np.float32)


def _decay_rates(d):
    return np.abs(np.linspace(math.log(HY_DECAY_PCT_MIN) / HY_DECAY_TARGET,
                              math.log(HY_DECAY_PCT_MAX) / HY_DECAY_TARGET, d)).astype(np.float32)[None, :]


def _hyena_filter(length, w1, b1, freq, w2, b2, w3, b3, w4):
    d = w4.shape[1] // 2
    tr = min(length, 512)
    nf = length // tr
    feats = _pos_features(length)
    feats = np.concatenate([feats, feats[:1], feats[:0:-1]], axis=0)
    z = jnp.asarray(np.pad(feats, ((0, 0), (0, LANES - feats.shape[1]))))
    const = lambda i: (0, 0)
    full = lambda a: pl.BlockSpec(a.shape, const)
    mat = lambda a: jnp.pad(a, ((0, LANES - a.shape[0]), (0, LANES - a.shape[1])))
    vec = lambda a: jnp.pad(a, (0, LANES - a.shape[0])).reshape(1, LANES)
    w4p = jnp.pad(w4, ((0, LANES - w4.shape[0]), (0, 0)))
    args = [z, mat(w1), vec(b1), vec(freq), mat(w2), vec(b2), mat(w3), vec(b3), w4p, jnp.asarray(_decay_rates(d))]
    in_specs = [pl.BlockSpec((tr, LANES), lambda i: (i, 0))] + [full(a) for a in args[1:]]
    in_specs[8] = pl.BlockSpec((LANES, d), lambda i: (0, i // nf))
    return pl.pallas_call(
        functools.partial(_filter_kernel, length=length),
        grid=(2 * nf,),
        in_specs=in_specs,
        out_specs=[pl.BlockSpec((tr, d), lambda i: (i, 0)), pl.BlockSpec((1, d), const)],
        out_shape=[jax.ShapeDtypeStruct((2 * length, d), F32), jax.ShapeDtypeStruct((1, d), F32)],
        compiler_params=_params(("arbitrary",)),
        name="hyena_filter",
    )(*args)


def _dft_mats(m, k, n):
    r = np.arange(m, dtype=np.int64)[:, None]
    c = np.arange(k, dtype=np.int64)[None, :]
    ang = 2.0 * math.pi * ((r * c) % n).astype(np.float64) / n
    return np.cos(ang).astype(np.float32), np.sin(ang).astype(np.float32)


def _cmul_mat(c, s, xr, xi, conj):
    xr = xr.astype(BF16)
    cr, sr = _dot(c, xr), _dot(s, xr)
    if xi is None:
        return cr, (sr if conj else -sr)
    xi = xi.astype(BF16)
    ci, si = _dot(c, xi), _dot(s, xi)
    if conj:
        return cr - si, ci + sr
    return cr + si, ci - sr


def _slab_fwd_kernel(*refs, has_imag, has_filter):
    refs = list(refs)
    fc, fs, xr_ref = refs[:3]
    xi_ref = refs[3] if has_imag else None
    rest = refs[3 + has_imag:]
    if has_filter:
        gr_ref, gi_ref, gn_ref, or_ref, oi_ref = rest
    else:
        or_ref, oi_ref = rest
    yr, yi = _cmul_mat(fc[...].astype(BF16), fs[...].astype(BF16), xr_ref[...],
                       xi_ref[...] if has_imag else None, False)
    if has_filter:
        gn = 1.0 / gn_ref[...]
        gr, gi = gr_ref[...] * gn, gi_ref[...] * gn
        yr, yi = yr * gr - yi * gi, yr * gi + yi * gr
    or_ref[...] = yr
    oi_ref[...] = yi


def _slab_fwd(x2, k_in, parts, n_out, n, spectrum=None):
    cols = x2.shape[1]
    tc = min(cols, 2048 if n_out <= LANES else 512)
    fc, fs = _dft_mats(n_out, k_in, n)
    const = lambda c: (0, 0)
    in_specs = [pl.BlockSpec((n_out, k_in), const), pl.BlockSpec((n_out, k_in), const)]
    args = [jnp.asarray(fc), jnp.asarray(fs)]
    for part in parts:
        if part is not None:
            in_specs.append(pl.BlockSpec((k_in, tc), lambda c, part=part: (part, c)))
            args.append(x2)
    ospec = pl.BlockSpec((n_out, tc), lambda c: (0, c))
    if spectrum is not None:
        in_specs += [ospec, ospec, pl.BlockSpec((1, tc), lambda c: (0, c))]
        args += list(spectrum)
    osh = jax.ShapeDtypeStruct((n_out, cols), F32)
    return pl.pallas_call(
        functools.partial(_slab_fwd_kernel, has_imag=parts[1] is not None, has_filter=spectrum is not None),
        grid=(cols // tc,),
        in_specs=in_specs,
        out_specs=[ospec, ospec],
        out_shape=[osh, osh],
        compiler_params=_params(("arbitrary",)),
        name="dft_slab_fwd",
    )(*args)


def _mid_kernel(*refs, has_filter):
    if has_filter:
        fc, fs, twc_ref, tws_ref, ar_ref, ai_ref, gr_ref, gi_ref, gn_ref, or_ref, oi_ref = refs
    else:
        fc, fs, twc_ref, tws_ref, ar_ref, ai_ref, or_ref, oi_ref = refs
    td = ar_ref.shape[2]
    c, s = fc[...].astype(BF16), fs[...].astype(BF16)
    twc = jnp.concatenate([twc_ref[0]] * (td // LANES), axis=1)
    tws = jnp.concatenate([tws_ref[0]] * (td // LANES), axis=1)
    ar, ai = ar_ref[0], ai_ref[0]
    pr, pi_ = ar * twc + ai * tws, ai * twc - ar * tws
    br, bi = _cmul_mat(c, s, pr, pi_, False)
    if not has_filter:
        or_ref[0] = br
        oi_ref[0] = bi
        return
    gn = 1.0 / gn_ref[...]
    gr, gi = gr_ref[0] * gn, gi_ref[0] * gn
    yr, yi = br * gr - bi * gi, br * gi + bi * gr
    qr, qi = _cmul_mat(c, s, yr, yi, True)
    or_ref[0] = qr * twc - qi * tws
    oi_ref[0] = qr * tws + qi * twc


def _mid_stage(a_re, a_im, n, spectrum=None):
    n1, _, d = a_re.shape
    td = 1024
    fc, fs = _dft_mats(LANES, LANES, LANES)
    k1 = lax.broadcasted_iota(jnp.int32, (n1, LANES, LANES), 0)
    m2 = lax.broadcasted_iota(jnp.int32, (n1, LANES, LANES), 1)
    ang = ((k1 * m2) % n).astype(F32) * (2.0 * math.pi / n)
    twc, tws = jnp.cos(ang), jnp.sin(ang)
    const = lambda s, j: (0, 0)
    slab = pl.BlockSpec((1, LANES, td), lambda s, j: (s, 0, j))
    tw = pl.BlockSpec((1, LANES, LANES), lambda s, j: (s, 0, 0))
    in_specs = [pl.BlockSpec((LANES, LANES), const), pl.BlockSpec((LANES, LANES), const), tw, tw, slab, slab]
    args = [jnp.asarray(fc), jnp.asarray(fs), twc, tws, a_re, a_im]
    if spectrum is not None:
        in_specs += [slab, slab, pl.BlockSpec((1, td), lambda s, j: (0, j))]
        args += list(spectrum)
    osh = jax.ShapeDtypeStruct((n1, LANES, d), F32)
    return pl.pallas_call(
        functools.partial(_mid_kernel, has_filter=spectrum is not None),
        grid=(n1, d // td),
        in_specs=in_specs,
        out_specs=[slab, slab],
        out_shape=[osh, osh],
        compiler_params=_params(("arbitrary", "arbitrary")),
        name="dft_mid",
    )(*args)


def _slab_inv_kernel(fc, fs, pr_ref, pi_ref, u0_ref, u1_ref, a0_ref, a1_ref, skip_ref, o_ref, *, scale):
    yr, yi = _cmul_mat(fc[...].astype(BF16), fs[...].astype(BF16), pr_ref[...], pi_ref[...], True)
    skip = skip_ref[...]
    o_ref[0] = ((yr * scale + u0_ref[...] * skip) * a0_ref[...]).astype(o_ref.dtype)
    o_ref[1] = ((yi * scale + u1_ref[...] * skip) * a1_ref[...]).astype(o_ref.dtype)


def _slab_inv(p_re, p_im, u2, x02, k_out, parts, skip, n, scale):
    n_in, cols = p_re.shape
    d = skip.shape[1]
    tc = min(cols, d if n_in <= LANES else 512)
    fc, fs = _dft_mats(k_out, n_in, n)
    const = lambda c: (0, 0)
    pspec = pl.BlockSpec((n_in, tc), lambda c: (0, c))
    uspec = lambda part: pl.BlockSpec((k_out, tc), lambda c, part=part: (part, c))
    return pl.pallas_call(
        functools.partial(_slab_inv_kernel, scale=scale),
        grid=(cols // tc,),
        in_specs=[pl.BlockSpec((k_out, n_in), const), pl.BlockSpec((k_out, n_in), const),
                  pspec, pspec, uspec(parts[0]), uspec(parts[1]), uspec(parts[0]), uspec(parts[1]),
                  pl.BlockSpec((1, tc), lambda c: (0, c % (d // tc)))],
        out_specs=pl.BlockSpec((2, k_out, tc), lambda c: (0, 0, c)),
        out_shape=jax.ShapeDtypeStruct((2, k_out, cols), BF16),
        compiler_params=_params(("arbitrary",)),
        name="dft_slab_inv",
    )(jnp.asarray(fc), jnp.asarray(fs), p_re, p_im, u2, u2, x02, x02, skip)


def _long_conv(u, x0, row0, length, filt, nrm, skip):
    d = u.shape[1]
    n = 2 * length
    if n <= 4 * LANES:
        parts = (row0 // length, row0 // length + 1)
        g_re, g_im = _slab_fwd(filt, n, (0, None), n, n)
        y_re, y_im = _slab_fwd(u, length, parts, n, n, spectrum=(g_re, g_im, nrm))
        y = _slab_inv(y_re, y_im, u, x0, length, parts, skip, n, 1.0 / n)
        return y.reshape(2 * length, d)
    n1 = n // LANES
    k1 = n1 // 2
    parts = (row0 // length, row0 // length + 1)
    slabs = lambda a: a.reshape(a.shape[0] // LANES, LANES, d)
    f_re, f_im = _slab3_fwd(slabs(filt), n1, (0, None), n1)
    g_re, g_im = _mid_stage(f_re, f_im, n)
    a_re, a_im = _slab3_fwd(slabs(u), k1, parts, n1)
    p_re, p_im = _mid_stage(a_re, a_im, n, spectrum=(g_re, g_im, nrm))
    y = _slab3_inv(p_re, p_im, slabs(u), slabs(x0), k1, parts, skip, 1.0 / n)
    return y.reshape(2 * length, d)


SLAB_ROWS = 16
SLAB_COLS = 256


def _to_lanes(x3):
    xs = jnp.swapaxes(x3, 0, 1)
    return jnp.concatenate([xs[r] for r in range(xs.shape[0])], axis=1)


def _from_lanes(y2, r):
    c = y2.shape[1] // r
    return jnp.swapaxes(jnp.stack([y2[:, i * c:(i + 1) * c] for i in range(r)], axis=0), 0, 1)


def _slab3_fwd_kernel(*refs, has_imag):
    fc, fs, xr_ref = refs[:3]
    xi_ref = refs[3] if has_imag else None
    or_ref, oi_ref = refs[3 + has_imag:]
    yr, yi = _cmul_mat(fc[...].astype(BF16), fs[...].astype(BF16), _to_lanes(xr_ref[...]),
                       _to_lanes(xi_ref[...]) if has_imag else None, False)
    or_ref[...] = _from_lanes(yr, or_ref.shape[1])
    oi_ref[...] = _from_lanes(yi, oi_ref.shape[1])


def _slab3_fwd(x3, k_in, parts, n1):
    _, rows, d = x3.shape
    fc, fs = _dft_mats(n1, k_in, n1)
    const = lambda g, j: (0, 0)
    in_specs = [pl.BlockSpec((n1, k_in), const), pl.BlockSpec((n1, k_in), const)]
    args = [jnp.asarray(fc), jnp.asarray(fs)]
    for part in parts:
        if part is not None:
            in_specs.append(pl.BlockSpec((k_in, SLAB_ROWS, SLAB_COLS), lambda g, j, part=part: (part, g, j)))
            args.append(x3)
    ospec = pl.BlockSpec((n1, SLAB_ROWS, SLAB_COLS), lambda g, j: (0, g, j))
    osh = jax.ShapeDtypeStruct((n1, rows, d), F32)
    return pl.pallas_call(
        functools.partial(_slab3_fwd_kernel, has_imag=parts[1] is not None),
        grid=(rows // SLAB_ROWS, d // SLAB_COLS),
        in_specs=in_specs,
        out_specs=[ospec, ospec],
        out_shape=[osh, osh],
        compiler_params=_params(("arbitrary", "arbitrary")),
        name="dft_slab3_fwd",
    )(*args)


def _slab3_inv_kernel(fc, fs, pr_ref, pi_ref, u0_ref, u1_ref, a0_ref, a1_ref, skip_ref, o_ref, *, scale):
    yr, yi = _cmul_mat(fc[...].astype(BF16), fs[...].astype(BF16), _to_lanes(pr_ref[...]), _to_lanes(pi_ref[...]), True)
    r = u0_ref.shape[1]
    skip = skip_ref[...]
    o_ref[0] = ((_from_lanes(yr, r) * scale + u0_ref[...] * skip) * a0_ref[...]).astype(o_ref.dtype)
    o_ref[1] = ((_from_lanes(yi, r) * scale + u1_ref[...] * skip) * a1_ref[...]).astype(o_ref.dtype)


def _slab3_inv(p_re, p_im, u3, x03, k_out, parts, skip, scale):
    n1, rows, d = p_re.shape
    fc, fs = _dft_mats(k_out, n1, n1)
    const = lambda g, j: (0, 0)
    pspec = pl.BlockSpec((n1, SLAB_ROWS, SLAB_COLS), lambda g, j: (0, g, j))
    uspec = lambda part: pl.BlockSpec((k_out, SLAB_ROWS, SLAB_COLS), lambda g, j, part=part: (part, g, j))
    return pl.pallas_call(
        functools.partial(_slab3_inv_kernel, scale=scale),
        grid=(rows // SLAB_ROWS, d // SLAB_COLS),
        in_specs=[pl.BlockSpec((k_out, n1), const), pl.BlockSpec((k_out, n1), const),
                  pspec, pspec, uspec(parts[0]), uspec(parts[1]), uspec(parts[0]), uspec(parts[1]),
                  pl.BlockSpec((1, SLAB_COLS), lambda g, j: (0, j))],
        out_specs=pl.BlockSpec((2, k_out, SLAB_ROWS, SLAB_COLS), lambda g, j: (0, 0, g, j)),
        out_shape=jax.ShapeDtypeStruct((2, k_out, rows, d), BF16),
        compiler_params=_params(("arbitrary", "arbitrary")),
        name="dft_slab3_inv",
    )(jnp.asarray(fc), jnp.asarray(fs), p_re, p_im, u3, u3, x03, x03, skip)


def kernel(x, c, ctx, c_ctx, ada_w, ada_b, norm_mix_g, norm_ffn_g, final_norm_g, hg_w_in, hg_lb, hg_onorm_g, hg_w_out, hy_w_in, hy_b_in, hy_short_w, hy_short_b, hy_f_w1, hy_f_b1, hy_f_freq, hy_f_w2, hy_f_b2, hy_f_w3, hy_f_b3, hy_f_w4, hy_skip, hy_w_out, hy_b_out, moe_w_router, moe_b_router, moe_w_gu, moe_b_gu, moe_w_down, moe_b_down):
    bsz, seq, d = x.shape
    ctx_len = ctx.shape[1]
    depth = ada_w.shape[0]
    n_lat = bsz * seq
    n_all = n_lat + bsz * ctx_len
    assert bsz == 2 and seq % MOE_TILE == 0 and ctx_len == GLA_ROWS

    cond8 = jnp.zeros((8, d), F32).at[0:2].set(c).at[2].set(c_ctx)
    mods = _ada_all(cond8, ada_w, ada_b).reshape(depth, 8, N_MOD, d)[:, :3]
    mods = jnp.pad(mods, ((0, 0), (0, 0), (0, 8 - N_MOD), (0, 0)))

    lb_soft = jax.nn.softmax(hg_lb.astype(F32), axis=0)
    lower_bounds = jnp.cumsum(lb_soft, axis=0) - lb_soft[0]

    xa = jnp.concatenate([x.reshape(n_lat, d), ctx.reshape(bsz * ctx_len, d)], axis=0)
    zero_bias = jnp.zeros((1, d), F32)
    final_g = final_norm_g.reshape(1, d)
    ctx_needed = [any(l % 2 == 0 for l in range(layer + 1, depth)) for layer in range(depth)]

    for layer in range(depth):
        j = layer // 2
        is_hgrn = layer % 2 == 0
        keep_ctx = ctx_needed[layer]
        n_out = n_all if keep_ctx else n_lat
        mod = mods[layer]
        ng1 = norm_mix_g[layer].reshape(1, d)
        ng2 = norm_ffn_g[layer].reshape(1, d)
        if is_hgrn:
            q, kf, lff, kb, lfb, v, g = _hg_in(xa, mod, ng1, lower_bounds[j].reshape(1, d),
                                               hg_w_in[j].astype(BF16), seq, n_all)
            o_fw = _gla_pass(q, kf, v, lff, seq, ctx_len, False)
            y = _gla_pass(q, kb, v, lfb, seq, ctx_len, True,
                          extra=(o_fw, g, hg_onorm_g[j].reshape(1, HG_DK)))
            w_out, b_out = hg_w_out[j].astype(BF16), zero_bias
        else:
            n_in = n_all if keep_ctx else n_lat
            u, x0 = _hy_in(xa, mod, ng1, hy_w_in[j].astype(BF16), hy_b_in[j], hy_short_w[j], hy_short_b[j],
                           seq, ctx_len, n_in)
            fpar = (hy_f_w1[j], hy_f_b1[j], hy_f_freq[j], hy_f_w2[j], hy_f_b2[j], hy_f_w3[j], hy_f_b3[j], hy_f_w4[j])
            skip = hy_skip[j].reshape(1, d)
            y = _long_conv(u, x0, 0, seq, *_hyena_filter(seq, *fpar), skip)
            if keep_ctx:
                yc = _long_conv(u, x0, n_lat, ctx_len, *_hyena_filter(ctx_len, *fpar), skip)
                y = jnp.concatenate([y, yc], axis=0)
            w_out, b_out = hy_w_out[j].astype(BF16), hy_b_out[j].reshape(1, d)
        wr = jnp.pad(moe_w_router[layer], ((0, 0), (0, LANES - N_EXPERTS)))
        br = jnp.pad(moe_b_router[layer], (0, LANES - N_EXPERTS)).reshape(1, LANES)
        xa, h2, logits = _out_proj(y, w_out, b_out, xa, mod, ng2, wr, br, seq, n_out)
        xa = _moe(h2, logits, xa, mod, final_g, moe_w_gu[layer].astype(BF16), moe_b_gu[layer],
                  moe_w_down[layer].astype(BF16), moe_b_down[layer], seq, layer == depth - 1)
    return xa.reshape(bsz, seq, d)
```

```python
import functools
import math

import numpy as np
import jax
import jax.numpy as jnp
from jax import lax
from jax.experimental import pallas as pl
from jax.experimental.pallas import tpu as pltpu

F32 = jnp.float32
BF16 = jnp.bfloat16

EPS = 1e-6
N_MOD = 6
LANES = 128
GRID_W = 64
HG_DK = 128
GLA_CHUNK = 128
GLA_ROWS = 256
GLA_HEADS = 2
HY_BANDS = 16
HY_DECAY_PCT_MIN = 0.3
HY_DECAY_PCT_MAX = 1.5
HY_DECAY_TARGET = 1e-2
N_EXPERTS = 32
TOP_K = 4
SWIGLU_ALPHA = 1.702
SWIGLU_LIMIT = 7.0
MOE_TILE = 512
VMEM_LIMIT = 56 * 1024 * 1024


def _params(sem, vmem=VMEM_LIMIT):
    return pltpu.CompilerParams(dimension_semantics=sem, vmem_limit_bytes=vmem)


def _sigmoid(x):
    return 1.0 / (1.0 + jnp.exp(-x))


def _dot(a, b):
    return jnp.dot(a, b, preferred_element_type=F32)


def _dot_nt(a, b):
    return lax.dot_general(a, b, (((1,), (1,)), ((), ())), preferred_element_type=F32)


def _split(a):
    hi = a.astype(BF16)
    lo = (a - hi.astype(F32)).astype(BF16)
    return hi, lo


def _dot3(a, b):
    (ah, al), (bh, bl) = a, b
    return _dot(ah, bh) + _dot(ah, bl) + _dot(al, bh)


def _rms_mod(x, g, shift, scale):
    ms = jnp.mean(x * x, axis=-1, keepdims=True)
    return x * lax.rsqrt(ms + EPS) * g * (1.0 + scale) + shift


def _ada_kernel(c_ref, w_ref, b_ref, o_ref):
    c = c_ref[...]
    o_ref[0] = _dot(c * _sigmoid(c), w_ref[0]) + b_ref[0]


def _ada_all(cond8, ada_w, ada_b):
    depth, d, nd = ada_w.shape
    tn = 1024
    return pl.pallas_call(
        _ada_kernel,
        grid=(depth, nd // tn),
        in_specs=[pl.BlockSpec((8, d), lambda l, j: (0, 0)),
                  pl.BlockSpec((1, d, tn), lambda l, j: (l, 0, j)),
                  pl.BlockSpec((1, 1, tn), lambda l, j: (l, 0, j))],
        out_specs=pl.BlockSpec((1, 8, tn), lambda l, j: (l, 0, j)),
        out_shape=jax.ShapeDtypeStruct((depth, 8, nd), F32),
        compiler_params=_params(("arbitrary", "arbitrary")),
        name="ada_mod",
    )(cond8, ada_w, ada_b.reshape(depth, 1, nd))


def _hg_in_kernel(x_ref, mod_ref, ng_ref, lb_ref, wq, wf, wb, wi, wg,
                  q_o, kf_o, lff_o, kb_o, lfb_o, v_o, g_o, h_scr):
    @pl.when(pl.program_id(1) == 0)
    def _():
        m = mod_ref[0]
        h_scr[...] = _rms_mod(x_ref[...], ng_ref[...], m[0:1], m[1:2]).astype(BF16)

    h = h_scr[...]
    q = _dot(h, wq[...])
    q_o[...] = q * _sigmoid(q) * (HG_DK ** -0.5)
    lb = lb_ref[...]
    for w, k_o, lf_o in ((wf, kf_o, lff_o), (wb, kb_o, lfb_o)):
        f = lb + (1.0 - lb) * _sigmoid(_dot(h, w[...]))
        k_o[...] = 1.0 - f
        lf_o[...] = jnp.log(f)
    v_o[...] = _dot(h, wi[...])
    g_o[...] = _dot(h, wg[...])


def _hg_in(x, mod, ng, lb, w_bf, seq, n_rows):
    d = x.shape[1]
    tm, tn = 512, 256
    nj = d // tn
    grp = lambda i, j: (jnp.minimum((i * tm) // seq, 2), 0, 0)
    wspec = lambda s: pl.BlockSpec((d, tn), lambda i, j, s=s: (0, s * nj + j))
    ospec = pl.BlockSpec((tm, tn), lambda i, j: (i, j))
    osh = jax.ShapeDtypeStruct((n_rows, d), F32)
    return pl.pallas_call(
        _hg_in_kernel,
        grid=(n_rows // tm, nj),
        in_specs=[pl.BlockSpec((tm, d), lambda i, j: (i, 0)),
                  pl.BlockSpec((1, 8, d), grp),
                  pl.BlockSpec((1, d), lambda i, j: (0, 0)),
                  pl.BlockSpec((1, tn), lambda i, j: (0, j))] + [wspec(s) for s in range(5)],
        out_specs=[ospec] * 7,
        out_shape=[osh] * 7,
        scratch_shapes=[pltpu.VMEM((tm, d), BF16)],
        compiler_params=_params(("arbitrary", "arbitrary")),
        name="hgrn_in_proj",
    )(x, mod, ng, lb, w_bf, w_bf, w_bf, w_bf, w_bf)


def _level_map(c, reverse):
    i = np.arange(c)[:, None]
    j = np.arange(c)[None, :]
    x = i ^ j
    lvl = np.where(x > 0, np.floor(np.log2(np.maximum(x, 1))), -1).astype(np.int32)
    side = (i < j) if reverse else (i > j)
    return np.where(side, lvl, -1).astype(np.int32)


def _cumsum_rows(x, row, c, reverse):
    sh = 1
    while sh < c:
        if reverse:
            x = x + jnp.where(row < c - sh, pltpu.roll(x, c - sh, axis=0), 0.0)
        else:
            x = x + jnp.where(row >= sh, pltpu.roll(x, sh, axis=0), 0.0)
        sh *= 2
    return x


def _gla_chunk(q, k, v, lf, st, lvl, row, reverse):
    c, wd = q.shape
    nh = wd // HG_DK
    b = _cumsum_rows(lf, row, c, reverse)
    lf_up = pltpu.roll(lf, c - 1, axis=0)
    lf_dn = pltpu.roll(lf, 1, axis=0)
    att = [jnp.zeros((c, c), F32) for _ in range(nh)]
    p = 0
    while (1 << p) < c:
        s = 1 << p
        is_query = ((row & s) == 0) if reverse else ((row & s) != 0)
        if s == 1:
            a = jnp.where(is_query, lf, 0.0)
        elif s == 2:
            m = row & 3
            if reverse:
                a = jnp.where(m == 0, lf + lf_up, jnp.where(m == 1, lf, jnp.where(m == 2, 0.0, lf_dn)))
            else:
                a = jnp.where(m == 0, lf_up, jnp.where(m == 1, 0.0, jnp.where(m == 2, lf, lf + lf_dn)))
        else:
            nb = c // (2 * s)
            r = s if reverse else s - 1
            b3 = b.reshape(nb, 2 * s, wd)
            br = jnp.broadcast_to(b3[:, r:r + 1, :], (nb, 2 * s, wd)).reshape(c, wd)
            a = -jnp.abs(b - br)
        z = (jnp.where(is_query, q, k) * jnp.exp(a)).astype(BF16)
        for h in range(nh):
            zh = z[:, h * HG_DK:(h + 1) * HG_DK]
            att[h] = jnp.where(lvl == p, _dot_nt(zh, zh), att[h])
        p += 1
    tot = b[0:1] if reverse else b[c - 1:c]
    qe = (q * jnp.exp(b)).astype(BF16)
    kd = (k * jnp.exp(tot - b)).astype(BF16)
    dec = jnp.exp(tot)
    outs, new_st = [], []
    for h in range(nh):
        cs = slice(h * HG_DK, (h + 1) * HG_DK)
        vh = v[:, cs]
        dsum = jnp.sum(q[:, cs] * k[:, cs], axis=-1, keepdims=True)
        o = _dot(att[h].astype(BF16), vh.astype(BF16)) + dsum * vh
        o = o + _dot_nt(qe[:, cs], st[h].astype(BF16))
        outs.append(o)
        new_st.append(st[h] * dec[:, cs] + _dot(vh.T.astype(BF16), kd[:, cs]))
    return jnp.concatenate(outs, axis=1), new_st


def _gla_kernel(*refs, reverse, gated):
    if gated:
        q_ref, k_ref, v_ref, lf_ref, lvl_ref, ofw_ref, g_ref, on_ref, y_ref, s_scr = refs
    else:
        q_ref, k_ref, v_ref, lf_ref, lvl_ref, y_ref, s_scr = refs
    c = GLA_CHUNK
    rows, wd = q_ref.shape
    nh = wd // HG_DK
    nchunk = rows // c

    @pl.when(pl.program_id(2) == 0)
    def _():
        s_scr[...] = jnp.zeros_like(s_scr)

    lvl = lvl_ref[...]
    row = lax.broadcasted_iota(jnp.int32, (c, wd), 0)

    def body(i, carry):
        ci = (nchunk - 1 - i) if reverse else i
        rs = pl.ds(pl.multiple_of(ci * c, c), c)
        st = [s_scr[h] for h in range(nh)]
        o, st = _gla_chunk(q_ref[rs, :], k_ref[rs, :], v_ref[rs, :], lf_ref[rs, :], st, lvl, row, reverse)
        for h in range(nh):
            s_scr[h] = st[h]
        if gated:
            o = o + ofw_ref[rs, :]
            g = g_ref[rs, :]
            on = on_ref[...]
            ys = []
            for h in range(nh):
                oh = o[:, h * HG_DK:(h + 1) * HG_DK]
                ms = jnp.mean(oh * oh, axis=-1, keepdims=True)
                ys.append(oh * lax.rsqrt(ms + EPS) * on)
            y = jnp.concatenate(ys, axis=1) * (g * _sigmoid(g))
            y_ref[rs, :] = y.astype(y_ref.dtype)
        else:
            y_ref[rs, :] = o
        return carry

    lax.fori_loop(0, nchunk, body, 0)


def _gla_pass(q, k, v, lf, seq, ctx_len, reverse, extra=None):
    n_all, d = q.shape
    rows, wd = GLA_ROWS, GLA_HEADS * HG_DK
    nlat, nctx = seq // rows, ctx_len // rows
    ctx_base = 2 * nlat

    def rmap(b, hh, t):
        tc = (nctx - 1 - t) if reverse else t
        tl = (nlat - 1 - (t - nctx)) if reverse else (t - nctx)
        return (jnp.where(t < nctx, ctx_base + b * nctx + tc, b * nlat + tl), hh)

    blk = pl.BlockSpec((rows, wd), rmap)
    lvl = jnp.asarray(_level_map(GLA_CHUNK, reverse))
    in_specs = [blk, blk, blk, blk, pl.BlockSpec((GLA_CHUNK, GLA_CHUNK), lambda b, hh, t: (0, 0))]
    args = [q, k, v, lf, lvl]
    gated = extra is not None
    if gated:
        in_specs += [blk, blk, pl.BlockSpec((1, HG_DK), lambda b, hh, t: (0, 0))]
        args += list(extra)
    return pl.pallas_call(
        functools.partial(_gla_kernel, reverse=reverse, gated=gated),
        grid=(2, d // wd, nlat + nctx),
        in_specs=in_specs,
        out_specs=blk,
        out_shape=jax.ShapeDtypeStruct((n_all, d), BF16 if gated else F32),
        scratch_shapes=[pltpu.VMEM((GLA_HEADS, HG_DK, HG_DK), F32)],
        compiler_params=_params(("arbitrary", "arbitrary", "arbitrary")),
        name="gla_scan_bwd" if reverse else "gla_scan_fwd",
    )(*args)


def _tm_pitch(d):
    return d // LANES + 1


def _tm_store(ref, val):
    rows, d = val.shape
    pitch = _tm_pitch(d)
    for k in range(pitch - 1):
        ref[pl.ds(k, rows, stride=pitch), :] = val[:, k * LANES:(k + 1) * LANES]
    ref[pl.ds(pitch - 1, rows, stride=pitch), :] = jnp.zeros((rows, LANES), val.dtype)


def _tm_load(ref, rows):
    pitch = ref.shape[0] // rows
    return jnp.concatenate([ref[pl.ds(k, rows, stride=pitch), :] for k in range(pitch - 1)], axis=1)


def _out_kernel(y_ref, w_ref, b_ref, x_ref, mod_ref, ng_ref, wr_ref, br_ref, xo_ref, h2_ref, lg_ref):
    m = mod_ref[0]
    xn = x_ref[...] + m[2:3] * (_dot(y_ref[...], w_ref[...]) + b_ref[...])
    xo_ref[...] = xn
    h2 = _rms_mod(xn, ng_ref[...], m[3:4], m[4:5])
    _tm_store(h2_ref, h2)
    lg_ref[...] = _dot3(_split(h2), _split(wr_ref[...])) + br_ref[...]


def _out_proj(y, w_bf, bias, x, mod, ng, wr_pad, br_pad, seq, n_rows):
    d = x.shape[1]
    tm = 256
    row = lambda i: (i, 0)
    const = lambda i: (0, 0)
    return pl.pallas_call(
        _out_kernel,
        grid=(n_rows // tm,),
        in_specs=[pl.BlockSpec((tm, d), row),
                  pl.BlockSpec((d, d), const),
                  pl.BlockSpec((1, d), const),
                  pl.BlockSpec((tm, d), row),
                  pl.BlockSpec((1, 8, d), lambda i: (jnp.minimum((i * tm) // seq, 2), 0, 0)),
                  pl.BlockSpec((1, d), const),
                  pl.BlockSpec((d, LANES), const),
                  pl.BlockSpec((1, LANES), const)],
        out_specs=[pl.BlockSpec((tm, d), row), pl.BlockSpec((tm * _tm_pitch(d), LANES), row),
                   pl.BlockSpec((tm, LANES), row)],
        out_shape=[jax.ShapeDtypeStruct((n_rows, d), F32), jax.ShapeDtypeStruct((n_rows * _tm_pitch(d), LANES), F32),
                   jax.ShapeDtypeStruct((n_rows, LANES), F32)],
        compiler_params=_params(("arbitrary",)),
        name="out_proj",
    )(y, w_bf, bias, x, mod, ng, wr_pad, br_pad)


def _topk_kernel(lg_ref, idx_ref, w_ref):
    l = lg_ref[...]
    lane = lax.broadcasted_iota(jnp.int32, l.shape, 1).astype(F32)
    l = jnp.where(lane < N_EXPERTS, l, -jnp.inf)
    idx_out = jnp.zeros(l.shape, F32)
    vals = []
    for r in range(TOP_K):
        m = jnp.max(l, axis=-1, keepdims=True)
        sel = jnp.min(jnp.where(l == m, lane, float(LANES)), axis=-1, keepdims=True)
        idx_out = jnp.where(lane == r, sel, idx_out)
        l = jnp.where(lane == sel, -jnp.inf, l)
        vals.append(m)
    es = [jnp.exp(v - vals[0]) for v in vals]
    tot = es[0] + es[1] + es[2] + es[3]
    w_out = jnp.zeros(l.shape, F32)
    for r in range(TOP_K):
        w_out = jnp.where(lane == r, es[r] / tot, w_out)
    idx_ref[...] = idx_out.astype(jnp.int32)
    w_ref[...] = w_out


def _topk(logits):
    n = logits.shape[0]
    tm = 512
    spec = pl.BlockSpec((tm, LANES), lambda i: (i, 0))
    return pl.pallas_call(
        _topk_kernel,
        grid=(n // tm,),
        in_specs=[spec],
        out_specs=[spec, spec],
        out_shape=[jax.ShapeDtypeStruct((n, LANES), jnp.int32), jax.ShapeDtypeStruct((n, LANES), F32)],
        compiler_params=_params(("arbitrary",)),
        name="router_topk",
    )(logits)


def _rank_kernel(idx_ref, rank_ref, cnt_ref, run_scr):
    @pl.when(pl.program_id(0) == 0)
    def _():
        run_scr[...] = jnp.zeros_like(run_scr)

    idx = idx_ref[...]
    tm = idx.shape[0]
    lane = lax.broadcasted_iota(jnp.int32, idx.shape, 1)
    onehots = [jnp.where(lane == idx[:, j:j + 1], 1.0, 0.0) for j in range(TOP_K)]
    esum = onehots[0] + onehots[1] + onehots[2] + onehots[3]
    ri = lax.broadcasted_iota(jnp.int32, (tm, tm), 0)
    ci = lax.broadcasted_iota(jnp.int32, (tm, tm), 1)
    before = jnp.where(ci < ri, 1.0, 0.0).astype(BF16)
    base = run_scr[...] + _dot(before, esum.astype(BF16))
    rank = jnp.zeros(idx.shape, F32)
    for j in range(TOP_K):
        rank = jnp.where(lane == j, jnp.sum(onehots[j] * base, axis=-1, keepdims=True), rank)
    rank_ref[...] = rank.astype(jnp.int32)
    run_scr[...] = run_scr[...] + jnp.sum(esum, axis=0, keepdims=True)
    cnt_ref[...] = run_scr[...]


def _rank(idx):
    n = idx.shape[0]
    tm = 512
    spec = pl.BlockSpec((tm, LANES), lambda i: (i, 0))
    return pl.pallas_call(
        _rank_kernel,
        grid=(n // tm,),
        in_specs=[spec],
        out_specs=[spec, pl.BlockSpec((1, LANES), lambda i: (0, 0))],
        out_shape=[jax.ShapeDtypeStruct((n, LANES), jnp.int32), jax.ShapeDtypeStruct((1, LANES), F32)],
        scratch_shapes=[pltpu.VMEM((1, LANES), F32)],
        compiler_params=_params(("arbitrary",)),
        name="router_rank",
    )(idx)


COMBINE_ROWS = 256


def _token_copy(src_hbm, s, dst, r, sem, pitch):
    return pltpu.make_async_copy(src_hbm.at[pl.ds(s * pitch, pitch - 1)], dst.at[pl.ds(r * pitch, pitch - 1)], sem)


DMA_UNROLL = 16


def _tokens_in_flight(h_hbm, src_ref, buf, sem, n_tok, wait):
    pitch = buf.shape[0] // n_tok

    def body(i, c):
        for k in range(DMA_UNROLL):
            r = i * DMA_UNROLL + k
            cp = _token_copy(h_hbm, src_ref[r], buf, r, sem, pitch)
            cp.wait() if wait else cp.start()
        return c

    lax.fori_loop(0, n_tok // DMA_UNROLL, body, 0)


def _expert_kernel(te_ref, nv_ref, src_ref, nxt_ref, h_hbm, wgu_ref, bgu_ref, wdn_ref, bdn_ref, ys_ref, xbuf, sem):
    t = pl.program_id(0)
    nv = nv_ref[0]
    slot = t % 2
    tm = MOE_TILE

    @pl.when(t == 0)
    def _():
        _tokens_in_flight(h_hbm, src_ref, xbuf.at[0], sem.at[0], tm, False)

    @pl.when(t + 1 < nv)
    def _():
        _tokens_in_flight(h_hbm, nxt_ref, xbuf.at[1 - slot], sem.at[1 - slot], tm, False)

    @pl.when(t < nv)
    def _():
        _tokens_in_flight(h_hbm, src_ref, xbuf.at[slot], sem.at[slot], tm, True)
        f = wdn_ref.shape[2]
        gu = _dot(_tm_load(xbuf.at[slot], tm).astype(BF16), wgu_ref[0, 0]) + bgu_ref[0, 0]
        gate = jnp.minimum(gu[:, :f], SWIGLU_LIMIT)
        up = jnp.clip(gu[:, f:], -SWIGLU_LIMIT, SWIGLU_LIMIT)
        act = (up + 1.0) * gate * _sigmoid(SWIGLU_ALPHA * gate)
        _tm_store(ys_ref, _dot(act.astype(BF16), wdn_ref[0, 0]) + bdn_ref[0, 0])

    @pl.when(t >= nv)
    def _():
        ys_ref[...] = jnp.zeros_like(ys_ref)


def _experts(h_tm, src, tile_expert, n_valid, layer, wgu_bf, bgu, wdn_bf, bdn):
    p_max = src.shape[0]
    _, ne, d, f2 = wgu_bf.shape
    f = f2 // 2
    sub = _tm_pitch(d)
    tm = MOE_TILE
    n_tiles = p_max // tm
    wmap = lambda t, te, nv: (layer, te[t], 0, 0)
    return pl.pallas_call(
        _expert_kernel,
        grid_spec=pltpu.PrefetchScalarGridSpec(
            num_scalar_prefetch=2,
            grid=(n_tiles,),
            in_specs=[pl.BlockSpec((tm,), lambda t, te, nv: (t,), memory_space=pltpu.SMEM),
                      pl.BlockSpec((tm,), lambda t, te, nv: (jnp.minimum(t + 1, n_tiles - 1),), memory_space=pltpu.SMEM),
                      pl.BlockSpec(memory_space=pl.ANY),
                      pl.BlockSpec((1, 1, d, f2), wmap),
                      pl.BlockSpec((1, 1, 1, f2), wmap),
                      pl.BlockSpec((1, 1, f, d), wmap),
                      pl.BlockSpec((1, 1, 1, d), wmap)],
            out_specs=pl.BlockSpec((tm * sub, LANES), lambda t, te, nv: (t, 0)),
            scratch_shapes=[pltpu.VMEM((2, tm * sub, LANES), F32), pltpu.SemaphoreType.DMA((2,))]),
        out_shape=jax.ShapeDtypeStruct((p_max * sub, LANES), F32),
        compiler_params=_params(("arbitrary",)),
        name="moe_experts",
    )(tile_expert, n_valid, src, src, h_tm, wgu_bf, bgu[:, :, None, :], wdn_bf, bdn[:, :, None, :])


def _combine_kernel(dest_ref, x_ref, w_ref, mod_ref, fg_ref, ys_hbm, o_ref, buf, sem, *, final):
    tm, d = x_ref.shape
    sub = _tm_pitch(d)

    def in_flight(wait):
        def body(i, c):
            for q in range(DMA_UNROLL):
                r, j = i * (DMA_UNROLL // TOP_K) + q // TOP_K, q % TOP_K
                cp = _token_copy(ys_hbm, dest_ref[i * DMA_UNROLL + q], buf.at[j], r, sem, sub)
                cp.wait() if wait else cp.start()
            return c

        lax.fori_loop(0, tm * TOP_K // DMA_UNROLL, body, 0)

    in_flight(False)
    in_flight(True)
    w = w_ref[...]
    pieces = []
    for k in range(sub - 1):
        piece = w[:, 0:1] * buf.at[0][pl.ds(k, tm, stride=sub), :]
        for j in range(1, TOP_K):
            piece = piece + w[:, j:j + 1] * buf.at[j][pl.ds(k, tm, stride=sub), :]
        pieces.append(piece)
    out = x_ref[...] + mod_ref[0][5:6] * jnp.concatenate(pieces, axis=1)
    if final:
        ms = jnp.mean(out * out, axis=-1, keepdims=True)
        out = out * lax.rsqrt(ms + EPS) * fg_ref[...]
    o_ref[...] = out


def _combine(dest_flat, x, wts, mod, final_g, ys, seq, n_rows, final):
    d = x.shape[1]
    tm = COMBINE_ROWS
    row = lambda i: (i, 0)
    return pl.pallas_call(
        functools.partial(_combine_kernel, final=final),
        grid=(n_rows // tm,),
        in_specs=[pl.BlockSpec((tm * TOP_K,), lambda i: (i,), memory_space=pltpu.SMEM),
                  pl.BlockSpec((tm, d), row),
                  pl.BlockSpec((tm, LANES), row),
                  pl.BlockSpec((1, 8, d), lambda i: (jnp.minimum((i * tm) // seq, 2), 0, 0)),
                  pl.BlockSpec((1, d), lambda i: (0, 0)),
                  pl.BlockSpec(memory_space=pl.ANY)],
        out_specs=pl.BlockSpec((tm, d), row),
        out_shape=jax.ShapeDtypeStruct((n_rows, d), F32),
        scratch_shapes=[pltpu.VMEM((TOP_K, tm * _tm_pitch(d), LANES), F32), pltpu.SemaphoreType.DMA(())],
        compiler_params=_params(("arbitrary",)),
        name="moe_combine",
    )(dest_flat, x, wts, mod, final_g, ys)


def _moe(h2_tm, logits, x, mod, final_g, layer, wgu_bf, bgu, wdn_bf, bdn, seq, final):
    n = logits.shape[0]
    tm = MOE_TILE
    idx, wts = _topk(logits)
    rank, counts = _rank(idx)
    cnt = counts[0, :N_EXPERTS].astype(jnp.int32)
    tiles_e = (cnt + tm - 1) // tm
    tile_end = jnp.cumsum(tiles_e)
    offs = (tile_end - tiles_e) * tm
    idx4, rank4 = idx[:, :TOP_K], rank[:, :TOP_K]
    dest = (offs[idx4] + rank4).reshape(-1)
    p_max = (n * TOP_K // tm + N_EXPERTS) * tm
    n_tiles = p_max // tm
    n_valid = tile_end[-1:]
    tile_id = jnp.minimum(jnp.arange(n_tiles, dtype=jnp.int32), n_valid[0] - 1)
    tile_expert = jnp.sum((tile_end[None, :] <= tile_id[:, None]).astype(jnp.int32), axis=1)
    tile_expert = jnp.minimum(tile_expert, N_EXPERTS - 1)
    tok = jnp.repeat(jnp.arange(n, dtype=jnp.int32), TOP_K)
    src = jnp.zeros((p_max,), jnp.int32).at[dest].set(tok)
    ys = _experts(h2_tm, src, tile_expert, n_valid.astype(jnp.int32), layer, wgu_bf, bgu, wdn_bf, bdn)
    return _combine(dest, x, wts, mod, final_g, ys, seq, n, final)


def _hy_in_kernel(x_ref, mod_ref, ng_ref, w0, w1, w2, bin_ref, sw_ref, sb_ref, u_o, x0_o, h_scr, *, n_lat_tiles, ctx_len):
    i = pl.program_id(0)

    @pl.when(pl.program_id(1) == 0)
    def _():
        m = mod_ref[0]
        h_scr[...] = _rms_mod(x_ref[...], ng_ref[...], m[0:1], m[1:2]).astype(BF16)

    h = h_scr[...]
    tm, tn = u_o.shape
    row = lax.broadcasted_iota(jnp.int32, (tm, tn), 0)
    seg = jnp.where(i >= n_lat_tiles, ctx_len, GRID_W)
    pos = row & (seg - 1)
    first = pos == 0
    last = pos == seg - 1
    bin_ = bin_ref[...]
    sw = sw_ref[...]
    sb = sb_ref[...]
    outs = []
    for s, w in enumerate((w0, w1, w2)):
        z = _dot(h, w[...]) + bin_[s:s + 1]
        zp = jnp.where(first, 0.0, pltpu.roll(z, 1, axis=0))
        zn = jnp.where(last, 0.0, pltpu.roll(z, tm - 1, axis=0))
        outs.append(sw[0, s:s + 1] * zp + sw[1, s:s + 1] * z + sw[2, s:s + 1] * zn + sb[s:s + 1])
    x0_o[...] = outs[0]
    u_o[...] = outs[2] * outs[1]


def _hy_in(x, mod, ng, w_bf, b_in, short_w, short_b, seq, ctx_len, n_rows):
    d = x.shape[1]
    tm, tn = 512, 256
    nj = d // tn
    wspec = lambda s: pl.BlockSpec((d, tn), lambda i, j, s=s: (0, s * nj + j))
    ospec = pl.BlockSpec((tm, tn), lambda i, j: (i, j))
    osh = jax.ShapeDtypeStruct((n_rows, d), F32)
    return pl.pallas_call(
        functools.partial(_hy_in_kernel, n_lat_tiles=2 * seq // tm, ctx_len=ctx_len),
        grid=(n_rows // tm, nj),
        in_specs=[pl.BlockSpec((tm, d), lambda i, j: (i, 0)),
                  pl.BlockSpec((1, 8, d), lambda i, j: (jnp.minimum((i * tm) // seq, 2), 0, 0)),
                  pl.BlockSpec((1, d), lambda i, j: (0, 0)),
                  wspec(0), wspec(1), wspec(2),
                  pl.BlockSpec((3, tn), lambda i, j: (0, j)),
                  pl.BlockSpec((3, 3, tn), lambda i, j: (0, 0, j)),
                  pl.BlockSpec((3, tn), lambda i, j: (0, j))],
        out_specs=[ospec, ospec],
        out_shape=[osh, osh],
        scratch_shapes=[pltpu.VMEM((tm, d), BF16)],
        compiler_params=_params(("arbitrary", "arbitrary")),
        name="hyena_in_proj",
    )(x, mod, ng, w_bf, w_bf, w_bf, b_in.reshape(3, d), short_w.reshape(3, 3, d), short_b.reshape(3, d))


def _filter_kernel(z_ref, w1, b1, fr, w2, b2, w3, b3, w4, dl_ref, f_o, nrm_o, *, length):
    i = pl.program_id(0)
    z = z_ref[...]
    freq = fr[...]
    a = jnp.sin(freq * (_dot(z, w1[...]) + b1[...]))
    a = jnp.sin(freq * (_dot(a, w2[...]) + b2[...]))
    a = jnp.sin(freq * (_dot(a, w3[...]) + b3[...]))
    h = _dot(a, w4[...]) * jnp.exp(-z[:, 0:1] * dl_ref[...])
    row = lax.broadcasted_iota(jnp.int32, h.shape, 0) + i * h.shape[0]
    h = jnp.where(row == length, 0.0, h)
    f_o[...] = h

    @pl.when(i == 0)
    def _():
        nrm_o[...] = jnp.zeros_like(nrm_o)

    nrm_o[...] = nrm_o[...] + jnp.sum(jnp.abs(h), axis=0, keepdims=True)


def _pos_features(length):
    t = (np.arange(length, dtype=np.float64) / length)[:, None]
    bands = np.linspace(1e-4, HY_BANDS - 1, HY_BANDS)[None, :]
    ang = 2.0 * math.pi * t * bands
    return np.concatenate([t, np.cos(ang), -np.sin(ang)], axis=-1).astype(np.float32)


def _decay_rates(d):
    return np.abs(np.linspace(math.log(HY_DECAY_PCT_MIN) / HY_DECAY_TARGET,
                              math.log(HY_DECAY_PCT_MAX) / HY_DECAY_TARGET, d)).astype(np.float32)[None, :]


def _hyena_filter(length, w1, b1, freq, w2, b2, w3, b3, w4):
    d = w4.shape[1] // 2
    tr = min(length, 512)
    nf = length // tr
    feats = _pos_features(length)
    feats = np.concatenate([feats, feats[:1], feats[:0:-1]], axis=0)
    z = jnp.asarray(np.pad(feats, ((0, 0), (0, LANES - feats.shape[1]))))
    const = lambda i: (0, 0)
    full = lambda a: pl.BlockSpec(a.shape, const)
    mat = lambda a: jnp.pad(a, ((0, LANES - a.shape[0]), (0, LANES - a.shape[1])))
    vec = lambda a: jnp.pad(a, (0, LANES - a.shape[0])).reshape(1, LANES)
    w4p = jnp.pad(w4, ((0, LANES - w4.shape[0]), (0, 0)))
    args = [z, mat(w1), vec(b1), vec(freq), mat(w2), vec(b2), mat(w3), vec(b3), w4p, jnp.asarray(_decay_rates(d))]
    in_specs = [pl.BlockSpec((tr, LANES), lambda i: (i, 0))] + [full(a) for a in args[1:]]
    in_specs[8] = pl.BlockSpec((LANES, d), lambda i: (0, i // nf))
    return pl.pallas_call(
        functools.partial(_filter_kernel, length=length),
        grid=(2 * nf,),
        in_specs=in_specs,
        out_specs=[pl.BlockSpec((tr, d), lambda i: (i, 0)), pl.BlockSpec((1, d), const)],
        out_shape=[jax.ShapeDtypeStruct((2 * length, d), F32), jax.ShapeDtypeStruct((1, d), F32)],
        compiler_params=_params(("arbitrary",)),
        name="hyena_filter",
    )(*args)


def _dft_mats(m, k, n):
    r = np.arange(m, dtype=np.int64)[:, None]
    c = np.arange(k, dtype=np.int64)[None, :]
    ang = 2.0 * math.pi * ((r * c) % n).astype(np.float64) / n
    return np.cos(ang).astype(np.float32), np.sin(ang).astype(np.float32)


def _cmul_mat(c, s, xr, xi, conj):
    xr = xr.astype(BF16)
    cr, sr = _dot(c, xr), _dot(s, xr)
    if xi is None:
        return cr, (sr if conj else -sr)
    xi = xi.astype(BF16)
    ci, si = _dot(c, xi), _dot(s, xi)
    if conj:
        return cr - si, ci + sr
    return cr + si, ci - sr


def _slab_fwd_kernel(*refs, has_imag, has_filter):
    refs = list(refs)
    fc, fs, xr_ref = refs[:3]
    xi_ref = refs[3] if has_imag else None
    rest = refs[3 + has_imag:]
    if has_filter:
        gr_ref, gi_ref, gn_ref, or_ref, oi_ref = rest
    else:
        or_ref, oi_ref = rest
    yr, yi = _cmul_mat(fc[...].astype(BF16), fs[...].astype(BF16), xr_ref[...],
                       xi_ref[...] if has_imag else None, False)
    if has_filter:
        gn = 1.0 / gn_ref[...]
        gr, gi = gr_ref[...] * gn, gi_ref[...] * gn
        yr, yi = yr * gr - yi * gi, yr * gi + yi * gr
    or_ref[...] = yr
    oi_ref[...] = yi


def _slab_fwd(x2, k_in, parts, n_out, n, spectrum=None):
    cols = x2.shape[1]
    tc = min(cols, 2048 if n_out <= LANES else 512)
    fc, fs = _dft_mats(n_out, k_in, n)
    const = lambda c: (0, 0)
    in_specs = [pl.BlockSpec((n_out, k_in), const), pl.BlockSpec((n_out, k_in), const)]
    args = [jnp.asarray(fc), jnp.asarray(fs)]
    for part in parts:
        if part is not None:
            in_specs.append(pl.BlockSpec((k_in, tc), lambda c, part=part: (part, c)))
            args.append(x2)
    ospec = pl.BlockSpec((n_out, tc), lambda c: (0, c))
    if spectrum is not None:
        in_specs += [ospec, ospec, pl.BlockSpec((1, tc), lambda c: (0, c))]
        args += list(spectrum)
    osh = jax.ShapeDtypeStruct((n_out, cols), F32)
    return pl.pallas_call(
        functools.partial(_slab_fwd_kernel, has_imag=parts[1] is not None, has_filter=spectrum is not None),
        grid=(cols // tc,),
        in_specs=in_specs,
        out_specs=[ospec, ospec],
        out_shape=[osh, osh],
        compiler_params=_params(("arbitrary",)),
        name="dft_slab_fwd",
    )(*args)


def _mid_kernel(*refs, has_filter):
    if has_filter:
        fc, fs, twc_ref, tws_ref, ar_ref, ai_ref, gr_ref, gi_ref, gn_ref, or_ref, oi_ref = refs
    else:
        fc, fs, twc_ref, tws_ref, ar_ref, ai_ref, or_ref, oi_ref = refs
    td = ar_ref.shape[2]
    c, s = fc[...].astype(BF16), fs[...].astype(BF16)
    twc = jnp.concatenate([twc_ref[0]] * (td // LANES), axis=1)
    tws = jnp.concatenate([tws_ref[0]] * (td // LANES), axis=1)
    ar, ai = ar_ref[0], ai_ref[0]
    pr, pi_ = ar * twc + ai * tws, ai * twc - ar * tws
    br, bi = _cmul_mat(c, s, pr, pi_, False)
    if not has_filter:
        or_ref[0] = br
        oi_ref[0] = bi
        return
    gn = 1.0 / gn_ref[...]
    gr, gi = gr_ref[0] * gn, gi_ref[0] * gn
    yr, yi = br * gr - bi * gi, br * gi + bi * gr
    qr, qi = _cmul_mat(c, s, yr, yi, True)
    or_ref[0] = qr * twc - qi * tws
    oi_ref[0] = qr * tws + qi * twc


def _mid_stage(a_re, a_im, n, spectrum=None):
    n1, _, d = a_re.shape
    td = 1024
    fc, fs = _dft_mats(LANES, LANES, LANES)
    k1 = lax.broadcasted_iota(jnp.int32, (n1, LANES, LANES), 0)
    m2 = lax.broadcasted_iota(jnp.int32, (n1, LANES, LANES), 1)
    ang = ((k1 * m2) % n).astype(F32) * (2.0 * math.pi / n)
    twc, tws = jnp.cos(ang), jnp.sin(ang)
    const = lambda s, j: (0, 0)
    slab = pl.BlockSpec((1, LANES, td), lambda s, j: (s, 0, j))
    tw = pl.BlockSpec((1, LANES, LANES), lambda s, j: (s, 0, 0))
    in_specs = [pl.BlockSpec((LANES, LANES), const), pl.BlockSpec((LANES, LANES), const), tw, tw, slab, slab]
    args = [jnp.asarray(fc), jnp.asarray(fs), twc, tws, a_re, a_im]
    if spectrum is not None:
        in_specs += [slab, slab, pl.BlockSpec((1, td), lambda s, j: (0, j))]
        args += list(spectrum)
    osh = jax.ShapeDtypeStruct((n1, LANES, d), F32)
    return pl.pallas_call(
        functools.partial(_mid_kernel, has_filter=spectrum is not None),
        grid=(n1, d // td),
        in_specs=in_specs,
        out_specs=[slab, slab],
        out_shape=[osh, osh],
        compiler_params=_params(("arbitrary", "arbitrary")),
        name="dft_mid",
    )(*args)


def _slab_inv_kernel(fc, fs, pr_ref, pi_ref, u0_ref, u1_ref, a0_ref, a1_ref, skip_ref, o_ref, *, scale):
    yr, yi = _cmul_mat(fc[...].astype(BF16), fs[...].astype(BF16), pr_ref[...], pi_ref[...], True)
    skip = skip_ref[...]
    o_ref[0] = ((yr * scale + u0_ref[...] * skip) * a0_ref[...]).astype(o_ref.dtype)
    o_ref[1] = ((yi * scale + u1_ref[...] * skip) * a1_ref[...]).astype(o_ref.dtype)


def _slab_inv(p_re, p_im, u2, x02, k_out, parts, skip, n, scale):
    n_in, cols = p_re.shape
    d = skip.shape[1]
    tc = min(cols, d if n_in <= LANES else 512)
    fc, fs = _dft_mats(k_out, n_in, n)
    const = lambda c: (0, 0)
    pspec = pl.BlockSpec((n_in, tc), lambda c: (0, c))
    uspec = lambda part: pl.BlockSpec((k_out, tc), lambda c, part=part: (part, c))
    return pl.pallas_call(
        functools.partial(_slab_inv_kernel, scale=scale),
        grid=(cols // tc,),
        in_specs=[pl.BlockSpec((k_out, n_in), const), pl.BlockSpec((k_out, n_in), const),
                  pspec, pspec, uspec(parts[0]), uspec(parts[1]), uspec(parts[0]), uspec(parts[1]),
                  pl.BlockSpec((1, tc), lambda c: (0, c % (d // tc)))],
        out_specs=pl.BlockSpec((2, k_out, tc), lambda c: (0, 0, c)),
        out_shape=jax.ShapeDtypeStruct((2, k_out, cols), BF16),
        compiler_params=_params(("arbitrary",)),
        name="dft_slab_inv",
    )(jnp.asarray(fc), jnp.asarray(fs), p_re, p_im, u2, u2, x02, x02, skip)


def _long_conv(u, x0, row0, length, filt, nrm, skip):
    d = u.shape[1]
    n = 2 * length
    if n <= 4 * LANES:
        parts = (row0 // length, row0 // length + 1)
        g_re, g_im = _slab_fwd(filt, n, (0, None), n, n)
        y_re, y_im = _slab_fwd(u, length, parts, n, n, spectrum=(g_re, g_im, nrm))
        y = _slab_inv(y_re, y_im, u, x0, length, parts, skip, n, 1.0 / n)
        return y.reshape(2 * length, d)
    n1 = n // LANES
    k1 = n1 // 2
    parts = (row0 // length, row0 // length + 1)
    slabs = lambda a: a.reshape(a.shape[0] // LANES, LANES, d)
    f_re, f_im = _slab3_fwd(slabs(filt), n1, (0, None), n1)
    g_re, g_im = _mid_stage(f_re, f_im, n)
    a_re, a_im = _slab3_fwd(slabs(u), k1, parts, n1)
    p_re, p_im = _mid_stage(a_re, a_im, n, spectrum=(g_re, g_im, nrm))
    y = _slab3_inv(p_re, p_im, slabs(u), slabs(x0), k1, parts, skip, 1.0 / n)
    return y.reshape(2 * length, d)


SLAB_ROWS = 16
SLAB_COLS = 256


def _to_lanes(x3):
    xs = jnp.swapaxes(x3, 0, 1)
    return jnp.concatenate([xs[r] for r in range(xs.shape[0])], axis=1)


def _from_lanes(y2, r):
    c = y2.shape[1] // r
    return jnp.swapaxes(jnp.stack([y2[:, i * c:(i + 1) * c] for i in range(r)], axis=0), 0, 1)


def _slab3_fwd_kernel(*refs, has_imag):
    fc, fs, xr_ref = refs[:3]
    xi_ref = refs[3] if has_imag else None
    or_ref, oi_ref = refs[3 + has_imag:]
    yr, yi = _cmul_mat(fc[...].astype(BF16), fs[...].astype(BF16), _to_lanes(xr_ref[...]),
                       _to_lanes(xi_ref[...]) if has_imag else None, False)
    or_ref[...] = _from_lanes(yr, or_ref.shape[1])
    oi_ref[...] = _from_lanes(yi, oi_ref.shape[1])


def _slab3_fwd(x3, k_in, parts, n1):
    _, rows, d = x3.shape
    fc, fs = _dft_mats(n1, k_in, n1)
    const = lambda g, j: (0, 0)
    in_specs = [pl.BlockSpec((n1, k_in), const), pl.BlockSpec((n1, k_in), const)]
    args = [jnp.asarray(fc), jnp.asarray(fs)]
    for part in parts:
        if part is not None:
            in_specs.append(pl.BlockSpec((k_in, SLAB_ROWS, SLAB_COLS), lambda g, j, part=part: (part, g, j)))
            args.append(x3)
    ospec = pl.BlockSpec((n1, SLAB_ROWS, SLAB_COLS), lambda g, j: (0, g, j))
    osh = jax.ShapeDtypeStruct((n1, rows, d), F32)
    return pl.pallas_call(
        functools.partial(_slab3_fwd_kernel, has_imag=parts[1] is not None),
        grid=(rows // SLAB_ROWS, d // SLAB_COLS),
        in_specs=in_specs,
        out_specs=[ospec, ospec],
        out_shape=[osh, osh],
        compiler_params=_params(("arbitrary", "arbitrary")),
        name="dft_slab3_fwd",
    )(*args)


def _slab3_inv_kernel(fc, fs, pr_ref, pi_ref, u0_ref, u1_ref, a0_ref, a1_ref, skip_ref, o_ref, *, scale):
    yr, yi = _cmul_mat(fc[...].astype(BF16), fs[...].astype(BF16), _to_lanes(pr_ref[...]), _to_lanes(pi_ref[...]), True)
    r = u0_ref.shape[1]
    skip = skip_ref[...]
    o_ref[0] = ((_from_lanes(yr, r) * scale + u0_ref[...] * skip) * a0_ref[...]).astype(o_ref.dtype)
    o_ref[1] = ((_from_lanes(yi, r) * scale + u1_ref[...] * skip) * a1_ref[...]).astype(o_ref.dtype)


def _slab3_inv(p_re, p_im, u3, x03, k_out, parts, skip, scale):
    n1, rows, d = p_re.shape
    fc, fs = _dft_mats(k_out, n1, n1)
    const = lambda g, j: (0, 0)
    pspec = pl.BlockSpec((n1, SLAB_ROWS, SLAB_COLS), lambda g, j: (0, g, j))
    uspec = lambda part: pl.BlockSpec((k_out, SLAB_ROWS, SLAB_COLS), lambda g, j, part=part: (part, g, j))
    return pl.pallas_call(
        functools.partial(_slab3_inv_kernel, scale=scale),
        grid=(rows // SLAB_ROWS, d // SLAB_COLS),
        in_specs=[pl.BlockSpec((k_out, n1), const), pl.BlockSpec((k_out, n1), const),
                  pspec, pspec, uspec(parts[0]), uspec(parts[1]), uspec(parts[0]), uspec(parts[1]),
                  pl.BlockSpec((1, SLAB_COLS), lambda g, j: (0, j))],
        out_specs=pl.BlockSpec((2, k_out, SLAB_ROWS, SLAB_COLS), lambda g, j: (0, 0, g, j)),
        out_shape=jax.ShapeDtypeStruct((2, k_out, rows, d), BF16),
        compiler_params=_params(("arbitrary", "arbitrary")),
        name="dft_slab3_inv",
    )(jnp.asarray(fc), jnp.asarray(fs), p_re, p_im, u3, u3, x03, x03, skip)


def kernel(x, c, ctx, c_ctx, ada_w, ada_b, norm_mix_g, norm_ffn_g, final_norm_g, hg_w_in, hg_lb, hg_onorm_g, hg_w_out, hy_w_in, hy_b_in, hy_short_w, hy_short_b, hy_f_w1, hy_f_b1, hy_f_freq, hy_f_w2, hy_f_b2, hy_f_w3, hy_f_b3, hy_f_w4, hy_skip, hy_w_out, hy_b_out, moe_w_router, moe_b_router, moe_w_gu, moe_b_gu, moe_w_down, moe_b_down):
    bsz, seq, d = x.shape
    ctx_len = ctx.shape[1]
    depth = ada_w.shape[0]
    n_lat = bsz * seq
    n_all = n_lat + bsz * ctx_len
    assert bsz == 2 and seq % MOE_TILE == 0 and ctx_len == GLA_ROWS

    cond8 = jnp.zeros((8, d), F32).at[0:2].set(c).at[2].set(c_ctx)
    mods = _ada_all(cond8, ada_w, ada_b).reshape(depth, 8, N_MOD, d)[:, :3]
    mods = jnp.pad(mods, ((0, 0), (0, 0), (0, 8 - N_MOD), (0, 0)))

    lb_soft = jax.nn.softmax(hg_lb.astype(F32), axis=0)
    lower_bounds = jnp.cumsum(lb_soft, axis=0) - lb_soft[0]

    xa = jnp.concatenate([x.reshape(n_lat, d), ctx.reshape(bsz * ctx_len, d)], axis=0)
    zero_bias = jnp.zeros((1, d), F32)
    final_g = final_norm_g.reshape(1, d)
    wgu_bf, wdn_bf = moe_w_gu.astype(BF16), moe_w_down.astype(BF16)
    ctx_needed = [any(l % 2 == 0 for l in range(layer + 1, depth)) for layer in range(depth)]

    for layer in range(depth):
        j = layer // 2
        is_hgrn = layer % 2 == 0
        keep_ctx = ctx_needed[layer]
        n_out = n_all if keep_ctx else n_lat
        mod = mods[layer]
        ng1 = norm_mix_g[layer].reshape(1, d)
        ng2 = norm_ffn_g[layer].reshape(1, d)
        if is_hgrn:
            q, kf, lff, kb, lfb, v, g = _hg_in(xa, mod, ng1, lower_bounds[j].reshape(1, d),
                                               hg_w_in[j].astype(BF16), seq, n_all)
            o_fw = _gla_pass(q, kf, v, lff, seq, ctx_len, False)
            y = _gla_pass(q, kb, v, lfb, seq, ctx_len, True,
                          extra=(o_fw, g, hg_onorm_g[j].reshape(1, HG_DK)))
            w_out, b_out = hg_w_out[j].astype(BF16), zero_bias
        else:
            n_in = n_all if keep_ctx else n_lat
            u, x0 = _hy_in(xa, mod, ng1, hy_w_in[j].astype(BF16), hy_b_in[j], hy_short_w[j], hy_short_b[j],
                           seq, ctx_len, n_in)
            fpar = (hy_f_w1[j], hy_f_b1[j], hy_f_freq[j], hy_f_w2[j], hy_f_b2[j], hy_f_w3[j], hy_f_b3[j], hy_f_w4[j])
            skip = hy_skip[j].reshape(1, d)
            y = _long_conv(u, x0, 0, seq, *_hyena_filter(seq, *fpar), skip)
            if keep_ctx:
                yc = _long_conv(u, x0, n_lat, ctx_len, *_hyena_filter(ctx_len, *fpar), skip)
                y = jnp.concatenate([y, yc], axis=0)
            w_out, b_out = hy_w_out[j].astype(BF16), hy_b_out[j].reshape(1, d)
        wr = jnp.pad(moe_w_router[layer], ((0, 0), (0, LANES - N_EXPERTS)))
        br = jnp.pad(moe_b_router[layer], (0, LANES - N_EXPERTS)).reshape(1, LANES)
        xa, h2, logits = _out_proj(y, w_out, b_out, xa, mod, ng2, wr, br, seq, n_out)
        xa = _moe(h2, logits, xa, mod, final_g, layer, wgu_bf, moe_b_gu, wdn_bf, moe_b_down, seq, layer == depth - 1)
    return xa.reshape(bsz, seq, d)
```

```python
import functools
import math

import numpy as np
import jax
import jax.numpy as jnp
from jax import lax
from jax.experimental import pallas as pl
from jax.experimental.pallas import tpu as pltpu

F32 = jnp.float32
BF16 = jnp.bfloat16

EPS = 1e-6
N_MOD = 6
LANES = 128
GRID_W = 64
HG_DK = 128
GLA_CHUNK = 128
GLA_ROWS = 256
GLA_HEADS = 4
HY_BANDS = 16
HY_DECAY_PCT_MIN = 0.3
HY_DECAY_PCT_MAX = 1.5
HY_DECAY_TARGET = 1e-2
N_EXPERTS = 32
TOP_K = 4
SWIGLU_ALPHA = 1.702
SWIGLU_LIMIT = 7.0
MOE_TILE = 512
VMEM_LIMIT = 56 * 1024 * 1024


def _params(sem, vmem=VMEM_LIMIT):
    return pltpu.CompilerParams(dimension_semantics=sem, vmem_limit_bytes=vmem)


def _sigmoid(x):
    return 1.0 / (1.0 + jnp.exp(-x))


def _dot(a, b):
    return jnp.dot(a, b, preferred_element_type=F32)


def _dot_nt(a, b):
    return lax.dot_general(a, b, (((1,), (1,)), ((), ())), preferred_element_type=F32)


def _split(a):
    hi = a.astype(BF16)
    lo = (a - hi.astype(F32)).astype(BF16)
    return hi, lo


def _dot3(a, b):
    (ah, al), (bh, bl) = a, b
    return _dot(ah, bh) + _dot(ah, bl) + _dot(al, bh)


def _rms_mod(x, g, shift, scale):
    ms = jnp.mean(x * x, axis=-1, keepdims=True)
    return x * lax.rsqrt(ms + EPS) * g * (1.0 + scale) + shift


def _ada_kernel(c_ref, w_ref, b_ref, o_ref):
    c = c_ref[...]
    o_ref[0] = _dot(c * _sigmoid(c), w_ref[0]) + b_ref[0]


def _ada_all(cond8, ada_w, ada_b):
    depth, d, nd = ada_w.shape
    tn = 1024
    return pl.pallas_call(
        _ada_kernel,
        grid=(depth, nd // tn),
        in_specs=[pl.BlockSpec((8, d), lambda l, j: (0, 0)),
                  pl.BlockSpec((1, d, tn), lambda l, j: (l, 0, j)),
                  pl.BlockSpec((1, 1, tn), lambda l, j: (l, 0, j))],
        out_specs=pl.BlockSpec((1, 8, tn), lambda l, j: (l, 0, j)),
        out_shape=jax.ShapeDtypeStruct((depth, 8, nd), F32),
        compiler_params=_params(("arbitrary", "arbitrary")),
        name="ada_mod",
    )(cond8, ada_w, ada_b.reshape(depth, 1, nd))


def _hg_in_kernel(x_ref, mod_ref, ng_ref, lb_ref, wq, wf, wb, wi, wg,
                  q_o, kf_o, lff_o, kb_o, lfb_o, v_o, g_o, h_scr):
    @pl.when(pl.program_id(1) == 0)
    def _():
        m = mod_ref[0]
        h_scr[...] = _rms_mod(x_ref[...], ng_ref[...], m[0:1], m[1:2]).astype(BF16)

    h = h_scr[...]
    q = _dot(h, wq[...])
    q_o[...] = q * _sigmoid(q) * (HG_DK ** -0.5)
    lb = lb_ref[...]
    for w, k_o, lf_o in ((wf, kf_o, lff_o), (wb, kb_o, lfb_o)):
        f = lb + (1.0 - lb) * _sigmoid(_dot(h, w[...]))
        k_o[...] = 1.0 - f
        lf_o[...] = jnp.log(f)
    v_o[...] = _dot(h, wi[...])
    g_o[...] = _dot(h, wg[...])


def _hg_in(x, mod, ng, lb, w_bf, seq, n_rows):
    d = x.shape[1]
    tm, tn = 512, 256
    nj = d // tn
    grp = lambda i, j: (jnp.minimum((i * tm) // seq, 2), 0, 0)
    wspec = lambda s: pl.BlockSpec((d, tn), lambda i, j, s=s: (0, s * nj + j))
    ospec = pl.BlockSpec((tm, tn), lambda i, j: (i, j))
    osh = jax.ShapeDtypeStruct((n_rows, d), F32)
    return pl.pallas_call(
        _hg_in_kernel,
        grid=(n_rows // tm, nj),
        in_specs=[pl.BlockSpec((tm, d), lambda i, j: (i, 0)),
                  pl.BlockSpec((1, 8, d), grp),
                  pl.BlockSpec((1, d), lambda i, j: (0, 0)),
                  pl.BlockSpec((1, tn), lambda i, j: (0, j))] + [wspec(s) for s in range(5)],
        out_specs=[ospec] * 7,
        out_shape=[osh] * 7,
        scratch_shapes=[pltpu.VMEM((tm, d), BF16)],
        compiler_params=_params(("arbitrary", "arbitrary")),
        name="hgrn_in_proj",
    )(x, mod, ng, lb, w_bf, w_bf, w_bf, w_bf, w_bf)


def _level_map(c, reverse):
    i = np.arange(c)[:, None]
    j = np.arange(c)[None, :]
    x = i ^ j
    lvl = np.where(x > 0, np.floor(np.log2(np.maximum(x, 1))), -1).astype(np.int32)
    side = (i < j) if reverse else (i > j)
    return np.where(side, lvl, -1).astype(np.int32)


def _cumsum_rows(x, row, c, reverse):
    sh = 1
    while sh < c:
        if reverse:
            x = x + jnp.where(row < c - sh, pltpu.roll(x, c - sh, axis=0), 0.0)
        else:
            x = x + jnp.where(row >= sh, pltpu.roll(x, sh, axis=0), 0.0)
        sh *= 2
    return x


def _gla_chunk(q, k, v, lf, st, lvl, row, reverse):
    c, wd = q.shape
    nh = wd // HG_DK
    b = _cumsum_rows(lf, row, c, reverse)
    lf_up = pltpu.roll(lf, c - 1, axis=0)
    lf_dn = pltpu.roll(lf, 1, axis=0)
    att = [jnp.zeros((c, c), F32) for _ in range(nh)]
    p = 0
    while (1 << p) < c:
        s = 1 << p
        is_query = ((row & s) == 0) if reverse else ((row & s) != 0)
        if s == 1:
            a = jnp.where(is_query, lf, 0.0)
        elif s == 2:
            m = row & 3
            if reverse:
                a = jnp.where(m == 0, lf + lf_up, jnp.where(m == 1, lf, jnp.where(m == 2, 0.0, lf_dn)))
            else:
                a = jnp.where(m == 0, lf_up, jnp.where(m == 1, 0.0, jnp.where(m == 2, lf, lf + lf_dn)))
        else:
            nb = c // (2 * s)
            r = s if reverse else s - 1
            b3 = b.reshape(nb, 2 * s, wd)
            br = jnp.broadcast_to(b3[:, r:r + 1, :], (nb, 2 * s, wd)).reshape(c, wd)
            a = -jnp.abs(b - br)
        z = (jnp.where(is_query, q, k) * jnp.exp(a)).astype(BF16)
        for h in range(nh):
            zh = z[:, h * HG_DK:(h + 1) * HG_DK]
            att[h] = jnp.where(lvl == p, _dot_nt(zh, zh), att[h])
        p += 1
    tot = b[0:1] if reverse else b[c - 1:c]
    qe = (q * jnp.exp(b)).astype(BF16)
    kd = (k * jnp.exp(tot - b)).astype(BF16)
    dec = jnp.exp(tot)
    outs, new_st = [], []
    for h in range(nh):
        cs = slice(h * HG_DK, (h + 1) * HG_DK)
        vh = v[:, cs]
        dsum = jnp.sum(q[:, cs] * k[:, cs], axis=-1, keepdims=True)
        o = _dot(att[h].astype(BF16), vh.astype(BF16)) + dsum * vh
        o = o + _dot_nt(qe[:, cs], st[h].astype(BF16))
        outs.append(o)
        new_st.append(st[h] * dec[:, cs] + _dot(vh.T.astype(BF16), kd[:, cs]))
    return jnp.concatenate(outs, axis=1), new_st


def _gla_kernel(*refs, reverse, gated):
    if gated:
        q_ref, k_ref, v_ref, lf_ref, lvl_ref, ofw_ref, g_ref, on_ref, y_ref, s_scr = refs
    else:
        q_ref, k_ref, v_ref, lf_ref, lvl_ref, y_ref, s_scr = refs
    c = GLA_CHUNK
    rows, wd = q_ref.shape
    nh = wd // HG_DK
    nchunk = rows // c

    @pl.when(pl.program_id(2) == 0)
    def _():
        s_scr[...] = jnp.zeros_like(s_scr)

    lvl = lvl_ref[...]
    row = lax.broadcasted_iota(jnp.int32, (c, wd), 0)

    def body(i, carry):
        ci = (nchunk - 1 - i) if reverse else i
        rs = pl.ds(pl.multiple_of(ci * c, c), c)
        st = [s_scr[h] for h in range(nh)]
        o, st = _gla_chunk(q_ref[rs, :], k_ref[rs, :], v_ref[rs, :], lf_ref[rs, :], st, lvl, row, reverse)
        for h in range(nh):
            s_scr[h] = st[h]
        if gated:
            o = o + ofw_ref[rs, :]
            g = g_ref[rs, :]
            on = on_ref[...]
            ys = []
            for h in range(nh):
                oh = o[:, h * HG_DK:(h + 1) * HG_DK]
                ms = jnp.mean(oh * oh, axis=-1, keepdims=True)
                ys.append(oh * lax.rsqrt(ms + EPS) * on)
            y = jnp.concatenate(ys, axis=1) * (g * _sigmoid(g))
            y_ref[rs, :] = y.astype(y_ref.dtype)
        else:
            y_ref[rs, :] = o
        return carry

    lax.fori_loop(0, nchunk, body, 0)


def _gla_pass(q, k, v, lf, seq, ctx_len, reverse, extra=None):
    n_all, d = q.shape
    rows, wd = GLA_ROWS, GLA_HEADS * HG_DK
    nlat, nctx = seq // rows, ctx_len // rows
    ctx_base = 2 * nlat

    def rmap(b, hh, t):
        tc = (nctx - 1 - t) if reverse else t
        tl = (nlat - 1 - (t - nctx)) if reverse else (t - nctx)
        return (jnp.where(t < nctx, ctx_base + b * nctx + tc, b * nlat + tl), hh)

    blk = pl.BlockSpec((rows, wd), rmap)
    lvl = jnp.asarray(_level_map(GLA_CHUNK, reverse))
    in_specs = [blk, blk, blk, blk, pl.BlockSpec((GLA_CHUNK, GLA_CHUNK), lambda b, hh, t: (0, 0))]
    args = [q, k, v, lf, lvl]
    gated = extra is not None
    if gated:
        in_specs += [blk, blk, pl.BlockSpec((1, HG_DK), lambda b, hh, t: (0, 0))]
        args += list(extra)
    return pl.pallas_call(
        functools.partial(_gla_kernel, reverse=reverse, gated=gated),
        grid=(2, d // wd, nlat + nctx),
        in_specs=in_specs,
        out_specs=blk,
        out_shape=jax.ShapeDtypeStruct((n_all, d), BF16 if gated else F32),
        scratch_shapes=[pltpu.VMEM((GLA_HEADS, HG_DK, HG_DK), F32)],
        compiler_params=_params(("arbitrary", "arbitrary", "arbitrary")),
        name="gla_scan_bwd" if reverse else "gla_scan_fwd",
    )(*args)


def _tm_pitch(d):
    return d // LANES + 1


def _tm_store(ref, val):
    rows, d = val.shape
    pitch = _tm_pitch(d)
    for k in range(pitch - 1):
        ref[pl.ds(k, rows, stride=pitch), :] = val[:, k * LANES:(k + 1) * LANES]
    ref[pl.ds(pitch - 1, rows, stride=pitch), :] = jnp.zeros((rows, LANES), val.dtype)


def _tm_load(ref, rows):
    pitch = ref.shape[0] // rows
    return jnp.concatenate([ref[pl.ds(k, rows, stride=pitch), :] for k in range(pitch - 1)], axis=1)


def _out_kernel(y_ref, w_ref, b_ref, x_ref, mod_ref, ng_ref, wr_ref, br_ref, xo_ref, h2_ref, lg_ref):
    m = mod_ref[0]
    xn = x_ref[...] + m[2:3] * (_dot(y_ref[...], w_ref[...]) + b_ref[...])
    xo_ref[...] = xn
    h2 = _rms_mod(xn, ng_ref[...], m[3:4], m[4:5])
    _tm_store(h2_ref, h2)
    lg_ref[...] = _dot3(_split(h2), _split(wr_ref[...])) + br_ref[...]


def _out_proj(y, w_bf, bias, x, mod, ng, wr_pad, br_pad, seq, n_rows):
    d = x.shape[1]
    tm = 256
    row = lambda i: (i, 0)
    const = lambda i: (0, 0)
    return pl.pallas_call(
        _out_kernel,
        grid=(n_rows // tm,),
        in_specs=[pl.BlockSpec((tm, d), row),
                  pl.BlockSpec((d, d), const),
                  pl.BlockSpec((1, d), const),
                  pl.BlockSpec((tm, d), row),
                  pl.BlockSpec((1, 8, d), lambda i: (jnp.minimum((i * tm) // seq, 2), 0, 0)),
                  pl.BlockSpec((1, d), const),
                  pl.BlockSpec((d, LANES), const),
                  pl.BlockSpec((1, LANES), const)],
        out_specs=[pl.BlockSpec((tm, d), row), pl.BlockSpec((tm * _tm_pitch(d), LANES), row),
                   pl.BlockSpec((tm, LANES), row)],
        out_shape=[jax.ShapeDtypeStruct((n_rows, d), F32), jax.ShapeDtypeStruct((n_rows * _tm_pitch(d), LANES), F32),
                   jax.ShapeDtypeStruct((n_rows, LANES), F32)],
        compiler_params=_params(("arbitrary",)),
        name="out_proj",
    )(y, w_bf, bias, x, mod, ng, wr_pad, br_pad)


def _topk_kernel(lg_ref, idx_ref, w_ref):
    l = lg_ref[...]
    lane = lax.broadcasted_iota(jnp.int32, l.shape, 1).astype(F32)
    l = jnp.where(lane < N_EXPERTS, l, -jnp.inf)
    idx_out = jnp.zeros(l.shape, F32)
    vals = []
    for r in range(TOP_K):
        m = jnp.max(l, axis=-1, keepdims=True)
        sel = jnp.min(jnp.where(l == m, lane, float(LANES)), axis=-1, keepdims=True)
        idx_out = jnp.where(lane == r, sel, idx_out)
        l = jnp.where(lane == sel, -jnp.inf, l)
        vals.append(m)
    es = [jnp.exp(v - vals[0]) for v in vals]
    tot = es[0] + es[1] + es[2] + es[3]
    w_out = jnp.zeros(l.shape, F32)
    for r in range(TOP_K):
        w_out = jnp.where(lane == r, es[r] / tot, w_out)
    idx_ref[...] = idx_out.astype(jnp.int32)
    w_ref[...] = w_out


def _topk(logits):
    n = logits.shape[0]
    tm = 512
    spec = pl.BlockSpec((tm, LANES), lambda i: (i, 0))
    return pl.pallas_call(
        _topk_kernel,
        grid=(n // tm,),
        in_specs=[spec],
        out_specs=[spec, spec],
        out_shape=[jax.ShapeDtypeStruct((n, LANES), jnp.int32), jax.ShapeDtypeStruct((n, LANES), F32)],
        compiler_params=_params(("arbitrary",)),
        name="router_topk",
    )(logits)


def _rank_kernel(idx_ref, rank_ref, cnt_ref, run_scr):
    @pl.when(pl.program_id(0) == 0)
    def _():
        run_scr[...] = jnp.zeros_like(run_scr)

    idx = idx_ref[...]
    tm = idx.shape[0]
    lane = lax.broadcasted_iota(jnp.int32, idx.shape, 1)
    onehots = [jnp.where(lane == idx[:, j:j + 1], 1.0, 0.0) for j in range(TOP_K)]
    esum = onehots[0] + onehots[1] + onehots[2] + onehots[3]
    ri = lax.broadcasted_iota(jnp.int32, (tm, tm), 0)
    ci = lax.broadcasted_iota(jnp.int32, (tm, tm), 1)
    before = jnp.where(ci < ri, 1.0, 0.0).astype(BF16)
    base = run_scr[...] + _dot(before, esum.astype(BF16))
    rank = jnp.zeros(idx.shape, F32)
    for j in range(TOP_K):
        rank = jnp.where(lane == j, jnp.sum(onehots[j] * base, axis=-1, keepdims=True), rank)
    rank_ref[...] = rank.astype(jnp.int32)
    run_scr[...] = run_scr[...] + jnp.sum(esum, axis=0, keepdims=True)
    cnt_ref[...] = run_scr[...]


def _rank(idx):
    n = idx.shape[0]
    tm = 512
    spec = pl.BlockSpec((tm, LANES), lambda i: (i, 0))
    return pl.pallas_call(
        _rank_kernel,
        grid=(n // tm,),
        in_specs=[spec],
        out_specs=[spec, pl.BlockSpec((1, LANES), lambda i: (0, 0))],
        out_shape=[jax.ShapeDtypeStruct((n, LANES), jnp.int32), jax.ShapeDtypeStruct((1, LANES), F32)],
        scratch_shapes=[pltpu.VMEM((1, LANES), F32)],
        compiler_params=_params(("arbitrary",)),
        name="router_rank",
    )(idx)


COMBINE_ROWS = 256


def _token_copy(src_hbm, s, dst, r, sem, pitch):
    return pltpu.make_async_copy(src_hbm.at[pl.ds(s * pitch, pitch - 1)], dst.at[pl.ds(r * pitch, pitch - 1)], sem)


DMA_UNROLL = 16


def _tokens_in_flight(h_hbm, src_ref, buf, sem, n_tok, wait):
    pitch = buf.shape[0] // n_tok

    def body(i, c):
        for k in range(DMA_UNROLL):
            r = i * DMA_UNROLL + k
            cp = _token_copy(h_hbm, src_ref[r], buf, r, sem, pitch)
            cp.wait() if wait else cp.start(priority=k % 2)
        return c

    lax.fori_loop(0, n_tok // DMA_UNROLL, body, 0)


def _expert_kernel(te_ref, nv_ref, src_ref, nxt_ref, h_hbm, wgu_ref, bgu_ref, wdn_ref, bdn_ref, ys_ref, xbuf, sem):
    t = pl.program_id(0)
    nv = nv_ref[0]
    slot = t % 2
    tm = MOE_TILE

    @pl.when(t == 0)
    def _():
        _tokens_in_flight(h_hbm, src_ref, xbuf.at[0], sem.at[0], tm, False)

    @pl.when(t + 1 < nv)
    def _():
        _tokens_in_flight(h_hbm, nxt_ref, xbuf.at[1 - slot], sem.at[1 - slot], tm, False)

    @pl.when(t < nv)
    def _():
        _tokens_in_flight(h_hbm, src_ref, xbuf.at[slot], sem.at[slot], tm, True)
        f = wdn_ref.shape[2]
        gu = _dot(_tm_load(xbuf.at[slot], tm).astype(BF16), wgu_ref[0, 0]) + bgu_ref[0, 0]
        gate = jnp.minimum(gu[:, :f], SWIGLU_LIMIT)
        up = jnp.clip(gu[:, f:], -SWIGLU_LIMIT, SWIGLU_LIMIT)
        act = (up + 1.0) * gate * _sigmoid(SWIGLU_ALPHA * gate)
        _tm_store(ys_ref, _dot(act.astype(BF16), wdn_ref[0, 0]) + bdn_ref[0, 0])

    @pl.when(t >= nv)
    def _():
        ys_ref[...] = jnp.zeros_like(ys_ref)


def _experts(h_tm, src, tile_expert, n_valid, layer, wgu_bf, bgu, wdn_bf, bdn):
    p_max = src.shape[0]
    _, ne, d, f2 = wgu_bf.shape
    f = f2 // 2
    sub = _tm_pitch(d)
    tm = MOE_TILE
    n_tiles = p_max // tm
    wmap = lambda t, te, nv: (layer, te[t], 0, 0)
    return pl.pallas_call(
        _expert_kernel,
        grid_spec=pltpu.PrefetchScalarGridSpec(
            num_scalar_prefetch=2,
            grid=(n_tiles,),
            in_specs=[pl.BlockSpec((tm,), lambda t, te, nv: (t,), memory_space=pltpu.SMEM),
                      pl.BlockSpec((tm,), lambda t, te, nv: (jnp.minimum(t + 1, n_tiles - 1),), memory_space=pltpu.SMEM),
                      pl.BlockSpec(memory_space=pl.ANY),
                      pl.BlockSpec((1, 1, d, f2), wmap),
                      pl.BlockSpec((1, 1, 1, f2), wmap),
                      pl.BlockSpec((1, 1, f, d), wmap),
                      pl.BlockSpec((1, 1, 1, d), wmap)],
            out_specs=pl.BlockSpec((tm * sub, LANES), lambda t, te, nv: (t, 0)),
            scratch_shapes=[pltpu.VMEM((2, tm * sub, LANES), F32), pltpu.SemaphoreType.DMA((2,))]),
        out_shape=jax.ShapeDtypeStruct((p_max * sub, LANES), F32),
        compiler_params=_params(("arbitrary",)),
        name="moe_experts",
    )(tile_expert, n_valid, src, src, h_tm, wgu_bf, bgu[:, :, None, :], wdn_bf, bdn[:, :, None, :])


def _combine_kernel(dest_ref, nxt_ref, x_ref, w_ref, mod_ref, fg_ref, ys_hbm, o_ref, buf, sem, *, final):
    tm, d = x_ref.shape
    sub = _tm_pitch(d)
    t = pl.program_id(0)
    slot = t % 2

    def in_flight(idx_ref, s, wait):
        def body(i, c):
            for q in range(DMA_UNROLL):
                r, j = i * (DMA_UNROLL // TOP_K) + q // TOP_K, q % TOP_K
                cp = _token_copy(ys_hbm, idx_ref[i * DMA_UNROLL + q], buf.at[s * TOP_K + j], r, sem.at[s], sub)
                cp.wait() if wait else cp.start(priority=q % 2)
            return c

        lax.fori_loop(0, tm * TOP_K // DMA_UNROLL, body, 0)

    @pl.when(t == 0)
    def _():
        in_flight(dest_ref, 0, False)

    @pl.when(t + 1 < pl.num_programs(0))
    def _():
        in_flight(nxt_ref, 1 - slot, False)

    in_flight(dest_ref, slot, True)
    w = w_ref[...]
    pieces = []
    for k in range(sub - 1):
        piece = w[:, 0:1] * buf.at[slot * TOP_K][pl.ds(k, tm, stride=sub), :]
        for j in range(1, TOP_K):
            piece = piece + w[:, j:j + 1] * buf.at[slot * TOP_K + j][pl.ds(k, tm, stride=sub), :]
        pieces.append(piece)
    out = x_ref[...] + mod_ref[0][5:6] * jnp.concatenate(pieces, axis=1)
    if final:
        ms = jnp.mean(out * out, axis=-1, keepdims=True)
        out = out * lax.rsqrt(ms + EPS) * fg_ref[...]
    o_ref[...] = out


def _combine(dest_flat, x, wts, mod, final_g, ys, seq, n_rows, final):
    d = x.shape[1]
    tm = COMBINE_ROWS
    n_steps = n_rows // tm
    row = lambda i: (i, 0)
    return pl.pallas_call(
        functools.partial(_combine_kernel, final=final),
        grid=(n_steps,),
        in_specs=[pl.BlockSpec((tm * TOP_K,), lambda i: (i,), memory_space=pltpu.SMEM),
                  pl.BlockSpec((tm * TOP_K,), lambda i: (jnp.minimum(i + 1, n_steps - 1),), memory_space=pltpu.SMEM),
                  pl.BlockSpec((tm, d), row),
                  pl.BlockSpec((tm, LANES), row),
                  pl.BlockSpec((1, 8, d), lambda i: (jnp.minimum((i * tm) // seq, 2), 0, 0)),
                  pl.BlockSpec((1, d), lambda i: (0, 0)),
                  pl.BlockSpec(memory_space=pl.ANY)],
        out_specs=pl.BlockSpec((tm, d), row),
        out_shape=jax.ShapeDtypeStruct((n_rows, d), F32),
        scratch_shapes=[pltpu.VMEM((2 * TOP_K, tm * _tm_pitch(d), LANES), F32), pltpu.SemaphoreType.DMA((2,))],
        compiler_params=_params(("arbitrary",)),
        name="moe_combine",
    )(dest_flat, dest_flat, x, wts, mod, final_g, ys)


def _moe(h2_tm, logits, x, mod, final_g, layer, wgu_bf, bgu, wdn_bf, bdn, seq, final):
    n = logits.shape[0]
    tm = MOE_TILE
    idx, wts = _topk(logits)
    rank, counts = _rank(idx)
    cnt = counts[0, :N_EXPERTS].astype(jnp.int32)
    tiles_e = (cnt + tm - 1) // tm
    tile_end = jnp.cumsum(tiles_e)
    offs = (tile_end - tiles_e) * tm
    idx4, rank4 = idx[:, :TOP_K], rank[:, :TOP_K]
    dest = (offs[idx4] + rank4).reshape(-1)
    p_max = (n * TOP_K // tm + N_EXPERTS) * tm
    n_tiles = p_max // tm
    n_valid = tile_end[-1:]
    tile_id = jnp.minimum(jnp.arange(n_tiles, dtype=jnp.int32), n_valid[0] - 1)
    tile_expert = jnp.sum((tile_end[None, :] <= tile_id[:, None]).astype(jnp.int32), axis=1)
    tile_expert = jnp.minimum(tile_expert, N_EXPERTS - 1)
    tok = jnp.repeat(jnp.arange(n, dtype=jnp.int32), TOP_K)
    src = jnp.zeros((p_max,), jnp.int32).at[dest].set(tok)
    ys = _experts(h2_tm, src, tile_expert, n_valid.astype(jnp.int32), layer, wgu_bf, bgu, wdn_bf, bdn)
    return _combine(dest, x, wts, mod, final_g, ys, seq, n, final)


def _hy_in_kernel(x_ref, mod_ref, ng_ref, w0, w1, w2, bin_ref, sw_ref, sb_ref, u_o, x0_o, h_scr, *, n_lat_tiles, ctx_len):
    i = pl.program_id(0)

    @pl.when(pl.program_id(1) == 0)
    def _():
        m = mod_ref[0]
        h_scr[...] = _rms_mod(x_ref[...], ng_ref[...], m[0:1], m[1:2]).astype(BF16)

    h = h_scr[...]
    tm, tn = u_o.shape
    row = lax.broadcasted_iota(jnp.int32, (tm, tn), 0)
    seg = jnp.where(i >= n_lat_tiles, ctx_len, GRID_W)
    pos = row & (seg - 1)
    first = pos == 0
    last = pos == seg - 1
    bin_ = bin_ref[...]
    sw = sw_ref[...]
    sb = sb_ref[...]
    outs = []
    for s, w in enumerate((w0, w1, w2)):
        z = _dot(h, w[...]) + bin_[s:s + 1]
        zp = jnp.where(first, 0.0, pltpu.roll(z, 1, axis=0))
        zn = jnp.where(last, 0.0, pltpu.roll(z, tm - 1, axis=0))
        outs.append(sw[0, s:s + 1] * zp + sw[1, s:s + 1] * z + sw[2, s:s + 1] * zn + sb[s:s + 1])
    x0_o[...] = outs[0]
    u_o[...] = outs[2] * outs[1]


def _hy_in(x, mod, ng, w_bf, b_in, short_w, short_b, seq, ctx_len, n_rows):
    d = x.shape[1]
    tm, tn = 512, 256
    nj = d // tn
    wspec = lambda s: pl.BlockSpec((d, tn), lambda i, j, s=s: (0, s * nj + j))
    ospec = pl.BlockSpec((tm, tn), lambda i, j: (i, j))
    osh = jax.ShapeDtypeStruct((n_rows, d), F32)
    return pl.pallas_call(
        functools.partial(_hy_in_kernel, n_lat_tiles=2 * seq // tm, ctx_len=ctx_len),
        grid=(n_rows // tm, nj),
        in_specs=[pl.BlockSpec((tm, d), lambda i, j: (i, 0)),
                  pl.BlockSpec((1, 8, d), lambda i, j: (jnp.minimum((i * tm) // seq, 2), 0, 0)),
                  pl.BlockSpec((1, d), lambda i, j: (0, 0)),
                  wspec(0), wspec(1), wspec(2),
                  pl.BlockSpec((3, tn), lambda i, j: (0, j)),
                  pl.BlockSpec((3, 3, tn), lambda i, j: (0, 0, j)),
                  pl.BlockSpec((3, tn), lambda i, j: (0, j))],
        out_specs=[ospec, ospec],
        out_shape=[osh, osh],
        scratch_shapes=[pltpu.VMEM((tm, d), BF16)],
        compiler_params=_params(("arbitrary", "arbitrary")),
        name="hyena_in_proj",
    )(x, mod, ng, w_bf, w_bf, w_bf, b_in.reshape(3, d), short_w.reshape(3, 3, d), short_b.reshape(3, d))


def _filter_kernel(z_ref, w1, b1, fr, w2, b2, w3, b3, w4, dl_ref, f_o, nrm_o, *, length):
    i = pl.program_id(0)
    z = z_ref[...]
    freq = fr[...]
    a = jnp.sin(freq * (_dot(z, w1[...]) + b1[...]))
    a = jnp.sin(freq * (_dot(a, w2[...]) + b2[...]))
    a = jnp.sin(freq * (_dot(a, w3[...]) + b3[...]))
    h = _dot(a, w4[...]) * jnp.exp(-z[:, 0:1] * dl_ref[...])
    row = lax.broadcasted_iota(jnp.int32, h.shape, 0) + i * h.shape[0]
    h = jnp.where(row == length, 0.0, h)
    f_o[...] = h

    @pl.when(i == 0)
    def _():
        nrm_o[...] = jnp.zeros_like(nrm_o)

    nrm_o[...] = nrm_o[...] + jnp.sum(jnp.abs(h), axis=0, keepdims=True)


def _pos_features(length):
    t = (np.arange(length, dtype=np.float64) / length)[:, None]
    bands = np.linspace(1e-4, HY_BANDS - 1, HY_BANDS)[None, :]
    ang = 2.0 * math.pi * t * bands
    return np.concatenate([t, np.cos(ang), -np.sin(ang)], axis=-1).astype(np.float32)


def _decay_rates(d):
    return np.abs(np.linspace(math.log(HY_DECAY_PCT_MIN) / HY_DECAY_TARGET,
                              math.log(HY_DECAY_PCT_MAX) / HY_DECAY_TARGET, d)).astype(np.float32)[None, :]


def _hyena_filter(length, w1, b1, freq, w2, b2, w3, b3, w4):
    d = w4.shape[1] // 2
    tr = min(length, 512)
    nf = length // tr
    feats = _pos_features(length)
    feats = np.concatenate([feats, feats[:1], feats[:0:-1]], axis=0)
    z = jnp.asarray(np.pad(feats, ((0, 0), (0, LANES - feats.shape[1]))))
    const = lambda i: (0, 0)
    full = lambda a: pl.BlockSpec(a.shape, const)
    mat = lambda a: jnp.pad(a, ((0, LANES - a.shape[0]), (0, LANES - a.shape[1])))
    vec = lambda a: jnp.pad(a, (0, LANES - a.shape[0])).reshape(1, LANES)
    w4p = jnp.pad(w4, ((0, LANES - w4.shape[0]), (0, 0)))
    args = [z, mat(w1), vec(b1), vec(freq), mat(w2), vec(b2), mat(w3), vec(b3), w4p, jnp.asarray(_decay_rates(d))]
    in_specs = [pl.BlockSpec((tr, LANES), lambda i: (i, 0))] + [full(a) for a in args[1:]]
    in_specs[8] = pl.BlockSpec((LANES, d), lambda i: (0, i // nf))
    return pl.pallas_call(
        functools.partial(_filter_kernel, length=length),
        grid=(2 * nf,),
        in_specs=in_specs,
        out_specs=[pl.BlockSpec((tr, d), lambda i: (i, 0)), pl.BlockSpec((1, d), const)],
        out_shape=[jax.ShapeDtypeStruct((2 * length, d), F32), jax.ShapeDtypeStruct((1, d), F32)],
        compiler_params=_params(("arbitrary",)),
        name="hyena_filter",
    )(*args)


def _dft_mats(m, k, n):
    r = np.arange(m, dtype=np.int64)[:, None]
    c = np.arange(k, dtype=np.int64)[None, :]
    ang = 2.0 * math.pi * ((r * c) % n).astype(np.float64) / n
    return np.cos(ang).astype(np.float32), np.sin(ang).astype(np.float32)


def _cmul_mat(c, s, xr, xi, conj):
    xr = xr.astype(BF16)
    cr, sr = _dot(c, xr), _dot(s, xr)
    if xi is None:
        return cr, (sr if conj else -sr)
    xi = xi.astype(BF16)
    ci, si = _dot(c, xi), _dot(s, xi)
    if conj:
        return cr - si, ci + sr
    return cr + si, ci - sr


def _slab_fwd_kernel(*refs, has_imag, has_filter):
    refs = list(refs)
    fc, fs, xr_ref = refs[:3]
    xi_ref = refs[3] if has_imag else None
    rest = refs[3 + has_imag:]
    if has_filter:
        gr_ref, gi_ref, gn_ref, or_ref, oi_ref = rest
    else:
        or_ref, oi_ref = rest
    yr, yi = _cmul_mat(fc[...].astype(BF16), fs[...].astype(BF16), xr_ref[...],
                       xi_ref[...] if has_imag else None, False)
    if has_filter:
        gn = 1.0 / gn_ref[...]
        gr, gi = gr_ref[...] * gn, gi_ref[...] * gn
        yr, yi = yr * gr - yi * gi, yr * gi + yi * gr
    or_ref[...] = yr
    oi_ref[...] = yi


def _slab_fwd(x2, k_in, parts, n_out, n, spectrum=None):
    cols = x2.shape[1]
    tc = min(cols, 2048 if n_out <= LANES else 512)
    fc, fs = _dft_mats(n_out, k_in, n)
    const = lambda c: (0, 0)
    in_specs = [pl.BlockSpec((n_out, k_in), const), pl.BlockSpec((n_out, k_in), const)]
    args = [jnp.asarray(fc), jnp.asarray(fs)]
    for part in parts:
        if part is not None:
            in_specs.append(pl.BlockSpec((k_in, tc), lambda c, part=part: (part, c)))
            args.append(x2)
    ospec = pl.BlockSpec((n_out, tc), lambda c: (0, c))
    if spectrum is not None:
        in_specs += [ospec, ospec, pl.BlockSpec((1, tc), lambda c: (0, c))]
        args += list(spectrum)
    osh = jax.ShapeDtypeStruct((n_out, cols), F32)
    return pl.pallas_call(
        functools.partial(_slab_fwd_kernel, has_imag=parts[1] is not None, has_filter=spectrum is not None),
        grid=(cols // tc,),
        in_specs=in_specs,
        out_specs=[ospec, ospec],
        out_shape=[osh, osh],
        compiler_params=_params(("arbitrary",)),
        name="dft_slab_fwd",
    )(*args)


def _mid_kernel(*refs, has_filter):
    if has_filter:
        fc, fs, twc_ref, tws_ref, ar_ref, ai_ref, gr_ref, gi_ref, gn_ref, or_ref, oi_ref = refs
    else:
        fc, fs, twc_ref, tws_ref, ar_ref, ai_ref, or_ref, oi_ref = refs
    td = ar_ref.shape[2]
    c, s = fc[...].astype(BF16), fs[...].astype(BF16)
    twc = jnp.concatenate([twc_ref[0]] * (td // LANES), axis=1)
    tws = jnp.concatenate([tws_ref[0]] * (td // LANES), axis=1)
    ar, ai = ar_ref[0], ai_ref[0]
    pr, pi_ = ar * twc + ai * tws, ai * twc - ar * tws
    br, bi = _cmul_mat(c, s, pr, pi_, False)
    if not has_filter:
        or_ref[0] = br
        oi_ref[0] = bi
        return
    gn = 1.0 / gn_ref[...]
    gr, gi = gr_ref[0] * gn, gi_ref[0] * gn
    yr, yi = br * gr - bi * gi, br * gi + bi * gr
    qr, qi = _cmul_mat(c, s, yr, yi, True)
    or_ref[0] = qr * twc - qi * tws
    oi_ref[0] = qr * tws + qi * twc


def _mid_stage(a_re, a_im, n, spectrum=None):
    n1, _, d = a_re.shape
    td = 1024
    fc, fs = _dft_mats(LANES, LANES, LANES)
    k1 = lax.broadcasted_iota(jnp.int32, (n1, LANES, LANES), 0)
    m2 = lax.broadcasted_iota(jnp.int32, (n1, LANES, LANES), 1)
    ang = ((k1 * m2) % n).astype(F32) * (2.0 * math.pi / n)
    twc, tws = jnp.cos(ang), jnp.sin(ang)
    const = lambda s, j: (0, 0)
    slab = pl.BlockSpec((1, LANES, td), lambda s, j: (s, 0, j))
    tw = pl.BlockSpec((1, LANES, LANES), lambda s, j: (s, 0, 0))
    in_specs = [pl.BlockSpec((LANES, LANES), const), pl.BlockSpec((LANES, LANES), const), tw, tw, slab, slab]
    args = [jnp.asarray(fc), jnp.asarray(fs), twc, tws, a_re, a_im]
    if spectrum is not None:
        in_specs += [slab, slab, pl.BlockSpec((1, td), lambda s, j: (0, j))]
        args += list(spectrum)
    osh = jax.ShapeDtypeStruct((n1, LANES, d), F32)
    return pl.pallas_call(
        functools.partial(_mid_kernel, has_filter=spectrum is not None),
        grid=(n1, d // td),
        in_specs=in_specs,
        out_specs=[slab, slab],
        out_shape=[osh, osh],
        compiler_params=_params(("arbitrary", "arbitrary")),
        name="dft_mid",
    )(*args)


def _slab_inv_kernel(fc, fs, pr_ref, pi_ref, u0_ref, u1_ref, a0_ref, a1_ref, skip_ref, o_ref, *, scale):
    yr, yi = _cmul_mat(fc[...].astype(BF16), fs[...].astype(BF16), pr_ref[...], pi_ref[...], True)
    skip = skip_ref[...]
    o_ref[0] = ((yr * scale + u0_ref[...] * skip) * a0_ref[...]).astype(o_ref.dtype)
    o_ref[1] = ((yi * scale + u1_ref[...] * skip) * a1_ref[...]).astype(o_ref.dtype)


def _slab_inv(p_re, p_im, u2, x02, k_out, parts, skip, n, scale):
    n_in, cols = p_re.shape
    d = skip.shape[1]
    tc = min(cols, d if n_in <= LANES else 512)
    fc, fs = _dft_mats(k_out, n_in, n)
    const = lambda c: (0, 0)
    pspec = pl.BlockSpec((n_in, tc), lambda c: (0, c))
    uspec = lambda part: pl.BlockSpec((k_out, tc), lambda c, part=part: (part, c))
    return pl.pallas_call(
        functools.partial(_slab_inv_kernel, scale=scale),
        grid=(cols // tc,),
        in_specs=[pl.BlockSpec((k_out, n_in), const), pl.BlockSpec((k_out, n_in), const),
                  pspec, pspec, uspec(parts[0]), uspec(parts[1]), uspec(parts[0]), uspec(parts[1]),
                  pl.BlockSpec((1, tc), lambda c: (0, c % (d // tc)))],
        out_specs=pl.BlockSpec((2, k_out, tc), lambda c: (0, 0, c)),
        out_shape=jax.ShapeDtypeStruct((2, k_out, cols), BF16),
        compiler_params=_params(("arbitrary",)),
        name="dft_slab_inv",
    )(jnp.asarray(fc), jnp.asarray(fs), p_re, p_im, u2, u2, x02, x02, skip)


def _long_conv(u, x0, row0, length, filt, nrm, skip):
    d = u.shape[1]
    n = 2 * length
    if n <= 4 * LANES:
        parts = (row0 // length, row0 // length + 1)
        g_re, g_im = _slab_fwd(filt, n, (0, None), n, n)
        y_re, y_im = _slab_fwd(u, length, parts, n, n, spectrum=(g_re, g_im, nrm))
        y = _slab_inv(y_re, y_im, u, x0, length, parts, skip, n, 1.0 / n)
        return y.reshape(2 * length, d)
    n1 = n // LANES
    k1 = n1 // 2
    parts = (row0 // length, row0 // length + 1)
    slabs = lambda a: a.reshape(a.shape[0] // LANES, LANES, d)
    f_re, f_im = _slab3_fwd(slabs(filt), n1, (0, None), n1)
    g_re, g_im = _mid_stage(f_re, f_im, n)
    a_re, a_im = _slab3_fwd(slabs(u), k1, parts, n1)
    p_re, p_im = _mid_stage(a_re, a_im, n, spectrum=(g_re, g_im, nrm))
    y = _slab3_inv(p_re, p_im, slabs(u), slabs(x0), k1, parts, skip, 1.0 / n)
    return y.reshape(2 * length, d)


SLAB_ROWS = 16
SLAB_COLS = 256


def _to_lanes(x3):
    xs = jnp.swapaxes(x3, 0, 1)
    return jnp.concatenate([xs[r] for r in range(xs.shape[0])], axis=1)


def _from_lanes(y2, r):
    c = y2.shape[1] // r
    return jnp.swapaxes(jnp.stack([y2[:, i * c:(i + 1) * c] for i in range(r)], axis=0), 0, 1)


def _slab3_fwd_kernel(*refs, has_imag):
    fc, fs, xr_ref = refs[:3]
    xi_ref = refs[3] if has_imag else None
    or_ref, oi_ref = refs[3 + has_imag:]
    yr, yi = _cmul_mat(fc[...].astype(BF16), fs[...].astype(BF16), _to_lanes(xr_ref[...]),
                       _to_lanes(xi_ref[...]) if has_imag else None, False)
    or_ref[...] = _from_lanes(yr, or_ref.shape[1])
    oi_ref[...] = _from_lanes(yi, oi_ref.shape[1])


def _slab3_fwd(x3, k_in, parts, n1):
    _, rows, d = x3.shape
    fc, fs = _dft_mats(n1, k_in, n1)
    const = lambda g, j: (0, 0)
    in_specs = [pl.BlockSpec((n1, k_in), const), pl.BlockSpec((n1, k_in), const)]
    args = [jnp.asarray(fc), jnp.asarray(fs)]
    for part in parts:
        if part is not None:
            in_specs.append(pl.BlockSpec((k_in, SLAB_ROWS, SLAB_COLS), lambda g, j, part=part: (part, g, j)))
            args.append(x3)
    ospec = pl.BlockSpec((n1, SLAB_ROWS, SLAB_COLS), lambda g, j: (0, g, j))
    osh = jax.ShapeDtypeStruct((n1, rows, d), F32)
    return pl.pallas_call(
        functools.partial(_slab3_fwd_kernel, has_imag=parts[1] is not None),
        grid=(rows // SLAB_ROWS, d // SLAB_COLS),
        in_specs=in_specs,
        out_specs=[ospec, ospec],
        out_shape=[osh, osh],
        compiler_params=_params(("arbitrary", "arbitrary")),
        name="dft_slab3_fwd",
    )(*args)


def _slab3_inv_kernel(fc, fs, pr_ref, pi_ref, u0_ref, u1_ref, a0_ref, a1_ref, skip_ref, o_ref, *, scale):
    yr, yi = _cmul_mat(fc[...].astype(BF16), fs[...].astype(BF16), _to_lanes(pr_ref[...]), _to_lanes(pi_ref[...]), True)
    r = u0_ref.shape[1]
    skip = skip_ref[...]
    o_ref[0] = ((_from_lanes(yr, r) * scale + u0_ref[...] * skip) * a0_ref[...]).astype(o_ref.dtype)
    o_ref[1] = ((_from_lanes(yi, r) * scale + u1_ref[...] * skip) * a1_ref[...]).astype(o_ref.dtype)


def _slab3_inv(p_re, p_im, u3, x03, k_out, parts, skip, scale):
    n1, rows, d = p_re.shape
    fc, fs = _dft_mats(k_out, n1, n1)
    const = lambda g, j: (0, 0)
    pspec = pl.BlockSpec((n1, SLAB_ROWS, SLAB_COLS), lambda g, j: (0, g, j))
    uspec = lambda part: pl.BlockSpec((k_out, SLAB_ROWS, SLAB_COLS), lambda g, j, part=part: (part, g, j))
    return pl.pallas_call(
        functools.partial(_slab3_inv_kernel, scale=scale),
        grid=(rows // SLAB_ROWS, d // SLAB_COLS),
        in_specs=[pl.BlockSpec((k_out, n1), const), pl.BlockSpec((k_out, n1), const),
                  pspec, pspec, uspec(parts[0]), uspec(parts[1]), uspec(parts[0]), uspec(parts[1]),
                  pl.BlockSpec((1, SLAB_COLS), lambda g, j: (0, j))],
        out_specs=pl.BlockSpec((2, k_out, SLAB_ROWS, SLAB_COLS), lambda g, j: (0, 0, g, j)),
        out_shape=jax.ShapeDtypeStruct((2, k_out, rows, d), BF16),
        compiler_params=_params(("arbitrary", "arbitrary")),
        name="dft_slab3_inv",
    )(jnp.asarray(fc), jnp.asarray(fs), p_re, p_im, u3, u3, x03, x03, skip)


def kernel(x, c, ctx, c_ctx, ada_w, ada_b, norm_mix_g, norm_ffn_g, final_norm_g, hg_w_in, hg_lb, hg_onorm_g, hg_w_out, hy_w_in, hy_b_in, hy_short_w, hy_short_b, hy_f_w1, hy_f_b1, hy_f_freq, hy_f_w2, hy_f_b2, hy_f_w3, hy_f_b3, hy_f_w4, hy_skip, hy_w_out, hy_b_out, moe_w_router, moe_b_router, moe_w_gu, moe_b_gu, moe_w_down, moe_b_down):
    bsz, seq, d = x.shape
    ctx_len = ctx.shape[1]
    depth = ada_w.shape[0]
    n_lat = bsz * seq
    n_all = n_lat + bsz * ctx_len
    assert bsz == 2 and seq % MOE_TILE == 0 and ctx_len == GLA_ROWS

    cond8 = jnp.zeros((8, d), F32).at[0:2].set(c).at[2].set(c_ctx)
    mods = _ada_all(cond8, ada_w, ada_b).reshape(depth, 8, N_MOD, d)[:, :3]
    mods = jnp.pad(mods, ((0, 0), (0, 0), (0, 8 - N_MOD), (0, 0)))

    lb_soft = jax.nn.softmax(hg_lb.astype(F32), axis=0)
    lower_bounds = jnp.cumsum(lb_soft, axis=0) - lb_soft[0]

    xa = jnp.concatenate([x.reshape(n_lat, d), ctx.reshape(bsz * ctx_len, d)], axis=0)
    zero_bias = jnp.zeros((1, d), F32)
    final_g = final_norm_g.reshape(1, d)
    wgu_bf, wdn_bf = moe_w_gu.astype(BF16), moe_w_down.astype(BF16)
    ctx_needed = [any(l % 2 == 0 for l in range(layer + 1, depth)) for layer in range(depth)]

    for layer in range(depth):
        j = layer // 2
        is_hgrn = layer % 2 == 0
        keep_ctx = ctx_needed[layer]
        n_out = n_all if keep_ctx else n_lat
        mod = mods[layer]
        ng1 = norm_mix_g[layer].reshape(1, d)
        ng2 = norm_ffn_g[layer].reshape(1, d)
        if is_hgrn:
            q, kf, lff, kb, lfb, v, g = _hg_in(xa, mod, ng1, lower_bounds[j].reshape(1, d),
                                               hg_w_in[j].astype(BF16), seq, n_all)
            o_fw = _gla_pass(q, kf, v, lff, seq, ctx_len, False)
            y = _gla_pass(q, kb, v, lfb, seq, ctx_len, True,
                          extra=(o_fw, g, hg_onorm_g[j].reshape(1, HG_DK)))
            w_out, b_out = hg_w_out[j].astype(BF16), zero_bias
        else:
            n_in = n_all if keep_ctx else n_lat
            u, x0 = _hy_in(xa, mod, ng1, hy_w_in[j].astype(BF16), hy_b_in[j], hy_short_w[j], hy_short_b[j],
                           seq, ctx_len, n_in)
            fpar = (hy_f_w1[j], hy_f_b1[j], hy_f_freq[j], hy_f_w2[j], hy_f_b2[j], hy_f_w3[j], hy_f_b3[j], hy_f_w4[j])
            skip = hy_skip[j].reshape(1, d)
            y = _long_conv(u, x0, 0, seq, *_hyena_filter(seq, *fpar), skip)
            if keep_ctx:
                yc = _long_conv(u, x0, n_lat, ctx_len, *_hyena_filter(ctx_len, *fpar), skip)
                y = jnp.concatenate([y, yc], axis=0)
            w_out, b_out = hy_w_out[j].astype(BF16), hy_b_out[j].reshape(1, d)
        wr = jnp.pad(moe_w_router[layer], ((0, 0), (0, LANES - N_EXPERTS)))
        br = jnp.pad(moe_b_router[layer], (0, LANES - N_EXPERTS)).reshape(1, LANES)
        xa, h2, logits = _out_proj(y, w_out, b_out, xa, mod, ng2, wr, br, seq, n_out)
        xa = _moe(h2, logits, xa, mod, final_g, layer, wgu_bf, moe_b_gu, wdn_bf, moe_b_down, seq, layer == depth - 1)
    return xa.reshape(bsz, seq, d)
```

```python
import functools
import math

import numpy as np
import jax
import jax.numpy as jnp
from jax import lax
from jax.experimental import pallas as pl
from jax.experimental.pallas import tpu as pltpu

F32 = jnp.float32
BF16 = jnp.bfloat16

EPS = 1e-6
LOG2_E = 1.0 / math.log(2.0)
N_MOD = 6
LANES = 128
GRID_W = 64
HG_DK = 128
GLA_CHUNK = 128
GLA_ROWS = 256
GLA_HEADS = 4
HY_BANDS = 16
HY_DECAY_PCT_MIN = 0.3
HY_DECAY_PCT_MAX = 1.5
HY_DECAY_TARGET = 1e-2
N_EXPERTS = 32
TOP_K = 4
SWIGLU_ALPHA = 1.702
SWIGLU_LIMIT = 7.0
MOE_TILE = 512
VMEM_LIMIT = 56 * 1024 * 1024


def _params(sem, vmem=VMEM_LIMIT):
    return pltpu.CompilerParams(dimension_semantics=sem, vmem_limit_bytes=vmem)


def _sigmoid(x):
    return 1.0 / (1.0 + jnp.exp(-x))


def _dot(a, b):
    return jnp.dot(a, b, preferred_element_type=F32)


def _dot_nt(a, b):
    return lax.dot_general(a, b, (((1,), (1,)), ((), ())), preferred_element_type=F32)


def _split(a):
    hi = a.astype(BF16)
    lo = (a - hi.astype(F32)).astype(BF16)
    return hi, lo


def _dot3(a, b):
    (ah, al), (bh, bl) = a, b
    return _dot(ah, bh) + _dot(ah, bl) + _dot(al, bh)


def _rms_mod(x, g, shift, scale):
    ms = jnp.mean(x * x, axis=-1, keepdims=True)
    return x * lax.rsqrt(ms + EPS) * g * (1.0 + scale) + shift


def _ada_kernel(c_ref, w_ref, b_ref, o_ref):
    c = c_ref[...]
    o_ref[0] = _dot(c * _sigmoid(c), w_ref[0]) + b_ref[0]


def _ada_all(cond8, ada_w, ada_b):
    depth, d, nd = ada_w.shape
    tn = 1024
    return pl.pallas_call(
        _ada_kernel,
        grid=(depth, nd // tn),
        in_specs=[pl.BlockSpec((8, d), lambda l, j: (0, 0)),
                  pl.BlockSpec((1, d, tn), lambda l, j: (l, 0, j)),
                  pl.BlockSpec((1, 1, tn), lambda l, j: (l, 0, j))],
        out_specs=pl.BlockSpec((1, 8, tn), lambda l, j: (l, 0, j)),
        out_shape=jax.ShapeDtypeStruct((depth, 8, nd), F32),
        compiler_params=_params(("arbitrary", "arbitrary")),
        name="ada_mod",
    )(cond8, ada_w, ada_b.reshape(depth, 1, nd))


def _hg_in_kernel(x_ref, mod_ref, ng_ref, lb_ref, wq, wf, wb, wi, wg,
                  q_o, kf_o, lff_o, kb_o, lfb_o, v_o, g_o, h_scr):
    @pl.when(pl.program_id(1) == 0)
    def _():
        m = mod_ref[0]
        h_scr[...] = _rms_mod(x_ref[...], ng_ref[...], m[0:1], m[1:2]).astype(BF16)

    h = h_scr[...]
    q = _dot(h, wq[...])
    q_o[...] = q * _sigmoid(q) * (HG_DK ** -0.5)
    lb = lb_ref[...]
    for w, k_o, lf_o in ((wf, kf_o, lff_o), (wb, kb_o, lfb_o)):
        f = lb + (1.0 - lb) * _sigmoid(_dot(h, w[...]))
        k_o[...] = 1.0 - f
        lf_o[...] = jnp.log(f) * LOG2_E
    v_o[...] = _dot(h, wi[...])
    g_o[...] = _dot(h, wg[...])


def _hg_in(x, mod, ng, lb, w_bf, seq, n_rows):
    d = x.shape[1]
    tm, tn = 512, 256
    nj = d // tn
    grp = lambda i, j: (jnp.minimum((i * tm) // seq, 2), 0, 0)
    wspec = lambda s: pl.BlockSpec((d, tn), lambda i, j, s=s: (0, s * nj + j))
    ospec = pl.BlockSpec((tm, tn), lambda i, j: (i, j))
    osh = jax.ShapeDtypeStruct((n_rows, d), F32)
    return pl.pallas_call(
        _hg_in_kernel,
        grid=(n_rows // tm, nj),
        in_specs=[pl.BlockSpec((tm, d), lambda i, j: (i, 0)),
                  pl.BlockSpec((1, 8, d), grp),
                  pl.BlockSpec((1, d), lambda i, j: (0, 0)),
                  pl.BlockSpec((1, tn), lambda i, j: (0, j))] + [wspec(s) for s in range(5)],
        out_specs=[ospec] * 7,
        out_shape=[osh] * 7,
        scratch_shapes=[pltpu.VMEM((tm, d), BF16)],
        compiler_params=_params(("arbitrary", "arbitrary")),
        name="hgrn_in_proj",
    )(x, mod, ng, lb, w_bf, w_bf, w_bf, w_bf, w_bf)


def _level_map(c, reverse):
    i = np.arange(c)[:, None]
    j = np.arange(c)[None, :]
    x = i ^ j
    lvl = np.where(x > 0, np.floor(np.log2(np.maximum(x, 1))), -1).astype(np.int32)
    side = (i < j) if reverse else (i > j)
    return np.where(side, lvl, -1).astype(np.int32)


def _tri_ones(c, reverse):
    i = np.arange(c)[:, None]
    m = np.arange(c)[None, :]
    return ((m >= i) if reverse else (m <= i)).astype(np.float32)


def _cumsum_rows(x, tri):
    hi = x.astype(BF16)
    r1 = x - hi.astype(F32)
    mid = r1.astype(BF16)
    lo = (r1 - mid.astype(F32)).astype(BF16)
    return _dot(tri, hi) + _dot(tri, mid) + _dot(tri, lo)


def _neg_abs(x):
    return pltpu.bitcast(pltpu.bitcast(x, jnp.uint32) | jnp.uint32(0x80000000), F32)


def _gla_chunk(q, k, v, lf, st, lvl, tri, row, reverse):
    c, wd = q.shape
    nh = wd // HG_DK
    b = _cumsum_rows(lf, tri)
    lf_up = pltpu.roll(lf, c - 1, axis=0)
    lf_dn = pltpu.roll(lf, 1, axis=0)
    att = [jnp.zeros((c, c), F32) for _ in range(nh)]
    p = 0
    while (1 << p) < c:
        s = 1 << p
        is_query = ((row & s) == 0) if reverse else ((row & s) != 0)
        if s == 1:
            a = jnp.where(is_query, lf, 0.0)
        elif s == 2:
            m = row & 3
            if reverse:
                a = jnp.where(m == 0, lf + lf_up, jnp.where(m == 1, lf, jnp.where(m == 2, 0.0, lf_dn)))
            else:
                a = jnp.where(m == 0, lf_up, jnp.where(m == 1, 0.0, jnp.where(m == 2, lf, lf + lf_dn)))
        else:
            nb = c // (2 * s)
            r = s if reverse else s - 1
            b3 = b.reshape(nb, 2 * s, wd)
            br = jnp.broadcast_to(b3[:, r:r + 1, :], (nb, 2 * s, wd)).reshape(c, wd)
            a = _neg_abs(b - br)
        z = (jnp.where(is_query, q, k) * jnp.exp2(a)).astype(BF16)
        for h in range(nh):
            zh = z[:, h * HG_DK:(h + 1) * HG_DK]
            att[h] = jnp.where(lvl == p, _dot_nt(zh, zh), att[h])
        p += 1
    tot = b[0:1] if reverse else b[c - 1:c]
    qe = (q * jnp.exp2(b)).astype(BF16)
    kd = (k * jnp.exp2(tot - b)).astype(BF16)
    dec = jnp.exp2(tot)
    outs, new_st = [], []
    for h in range(nh):
        cs = slice(h * HG_DK, (h + 1) * HG_DK)
        vh = v[:, cs]
        dsum = jnp.sum(q[:, cs] * k[:, cs], axis=-1, keepdims=True)
        o = _dot(att[h].astype(BF16), vh.astype(BF16)) + dsum * vh
        o = o + _dot_nt(qe[:, cs], st[h].astype(BF16))
        outs.append(o)
        new_st.append(st[h] * dec[:, cs] + _dot(vh.T.astype(BF16), kd[:, cs]))
    return jnp.concatenate(outs, axis=1), new_st


def _gla_kernel(*refs, reverse, gated):
    if gated:
        q_ref, k_ref, v_ref, lf_ref, lvl_ref, tri_ref, ofw_ref, g_ref, on_ref, y_ref, s_scr = refs
    else:
        q_ref, k_ref, v_ref, lf_ref, lvl_ref, tri_ref, y_ref, s_scr = refs
    c = GLA_CHUNK
    rows, wd = q_ref.shape
    nh = wd // HG_DK
    nchunk = rows // c

    @pl.when(pl.program_id(2) == 0)
    def _():
        s_scr[...] = jnp.zeros_like(s_scr)

    lvl = lvl_ref[...]
    tri = tri_ref[...]
    row = lax.broadcasted_iota(jnp.int32, (c, wd), 0)

    def body(i, carry):
        ci = (nchunk - 1 - i) if reverse else i
        rs = pl.ds(pl.multiple_of(ci * c, c), c)
        st = [s_scr[h] for h in range(nh)]
        o, st = _gla_chunk(q_ref[rs, :], k_ref[rs, :], v_ref[rs, :], lf_ref[rs, :], st, lvl, tri, row, reverse)
        for h in range(nh):
            s_scr[h] = st[h]
        if gated:
            o = o + ofw_ref[rs, :]
            g = g_ref[rs, :]
            on = on_ref[...]
            ys = []
            for h in range(nh):
                oh = o[:, h * HG_DK:(h + 1) * HG_DK]
                ms = jnp.mean(oh * oh, axis=-1, keepdims=True)
                ys.append(oh * lax.rsqrt(ms + EPS) * on)
            y = jnp.concatenate(ys, axis=1) * (g * _sigmoid(g))
            y_ref[rs, :] = y.astype(y_ref.dtype)
        else:
            y_ref[rs, :] = o
        return carry

    lax.fori_loop(0, nchunk, body, 0)


def _gla_pass(q, k, v, lf, seq, ctx_len, reverse, extra=None):
    n_all, d = q.shape
    rows, wd = GLA_ROWS, GLA_HEADS * HG_DK
    nlat, nctx = seq // rows, ctx_len // rows
    ctx_base = 2 * nlat

    def rmap(b, hh, t):
        tc = (nctx - 1 - t) if reverse else t
        tl = (nlat - 1 - (t - nctx)) if reverse else (t - nctx)
        return (jnp.where(t < nctx, ctx_base + b * nctx + tc, b * nlat + tl), hh)

    blk = pl.BlockSpec((rows, wd), rmap)
    lvl = jnp.asarray(_level_map(GLA_CHUNK, reverse))
    tri = jnp.asarray(_tri_ones(GLA_CHUNK, reverse), dtype=BF16)
    sq = pl.BlockSpec((GLA_CHUNK, GLA_CHUNK), lambda b, hh, t: (0, 0))
    in_specs = [blk, blk, blk, blk, sq, sq]
    args = [q, k, v, lf, lvl, tri]
    gated = extra is not None
    if gated:
        in_specs += [blk, blk, pl.BlockSpec((1, HG_DK), lambda b, hh, t: (0, 0))]
        args += list(extra)
    return pl.pallas_call(
        functools.partial(_gla_kernel, reverse=reverse, gated=gated),
        grid=(2, d // wd, nlat + nctx),
        in_specs=in_specs,
        out_specs=blk,
        out_shape=jax.ShapeDtypeStruct((n_all, d), BF16 if gated else F32),
        scratch_shapes=[pltpu.VMEM((GLA_HEADS, HG_DK, HG_DK), F32)],
        compiler_params=_params(("arbitrary", "arbitrary", "arbitrary")),
        name="gla_scan_bwd" if reverse else "gla_scan_fwd",
    )(*args)


def _tm_pitch(d):
    return d // LANES + 1


def _tm_store(ref, val):
    rows, d = val.shape
    pitch = _tm_pitch(d)
    for k in range(pitch - 1):
        ref[pl.ds(k, rows, stride=pitch), :] = val[:, k * LANES:(k + 1) * LANES]
    ref[pl.ds(pitch - 1, rows, stride=pitch), :] = jnp.zeros((rows, LANES), val.dtype)


def _tm_load(ref, rows):
    pitch = ref.shape[0] // rows
    return jnp.concatenate([ref[pl.ds(k, rows, stride=pitch), :] for k in range(pitch - 1)], axis=1)


def _out_kernel(y_ref, w_ref, b_ref, x_ref, mod_ref, ng_ref, wr_ref, br_ref, xo_ref, h2_ref, lg_ref):
    m = mod_ref[0]
    xn = x_ref[...] + m[2:3] * (_dot(y_ref[...], w_ref[...]) + b_ref[...])
    xo_ref[...] = xn
    h2 = _rms_mod(xn, ng_ref[...], m[3:4], m[4:5])
    _tm_store(h2_ref, h2)
    lg_ref[...] = _dot3(_split(h2), _split(wr_ref[...])) + br_ref[...]


def _out_proj(y, w_bf, bias, x, mod, ng, wr_pad, br_pad, seq, n_rows):
    d = x.shape[1]
    tm = 256
    row = lambda i: (i, 0)
    const = lambda i: (0, 0)
    return pl.pallas_call(
        _out_kernel,
        grid=(n_rows // tm,),
        in_specs=[pl.BlockSpec((tm, d), row),
                  pl.BlockSpec((d, d), const),
                  pl.BlockSpec((1, d), const),
                  pl.BlockSpec((tm, d), row),
                  pl.BlockSpec((1, 8, d), lambda i: (jnp.minimum((i * tm) // seq, 2), 0, 0)),
                  pl.BlockSpec((1, d), const),
                  pl.BlockSpec((d, LANES), const),
                  pl.BlockSpec((1, LANES), const)],
        out_specs=[pl.BlockSpec((tm, d), row), pl.BlockSpec((tm * _tm_pitch(d), LANES), row),
                   pl.BlockSpec((tm, LANES), row)],
        out_shape=[jax.ShapeDtypeStruct((n_rows, d), F32), jax.ShapeDtypeStruct((n_rows * _tm_pitch(d), LANES), F32),
                   jax.ShapeDtypeStruct((n_rows, LANES), F32)],
        compiler_params=_params(("arbitrary",)),
        name="out_proj",
    )(y, w_bf, bias, x, mod, ng, wr_pad, br_pad)


def _topk_kernel(lg_ref, idx_ref, w_ref):
    l = lg_ref[...]
    lane = lax.broadcasted_iota(jnp.int32, l.shape, 1).astype(F32)
    l = jnp.where(lane < N_EXPERTS, l, -jnp.inf)
    idx_out = jnp.zeros(l.shape, F32)
    vals = []
    for r in range(TOP_K):
        m = jnp.max(l, axis=-1, keepdims=True)
        sel = jnp.min(jnp.where(l == m, lane, float(LANES)), axis=-1, keepdims=True)
        idx_out = jnp.where(lane == r, sel, idx_out)
        l = jnp.where(lane == sel, -jnp.inf, l)
        vals.append(m)
    es = [jnp.exp(v - vals[0]) for v in vals]
    tot = es[0] + es[1] + es[2] + es[3]
    w_out = jnp.zeros(l.shape, F32)
    for r in range(TOP_K):
        w_out = jnp.where(lane == r, es[r] / tot, w_out)
    idx_ref[...] = idx_out.astype(jnp.int32)
    w_ref[...] = w_out


def _topk(logits):
    n = logits.shape[0]
    tm = 512
    spec = pl.BlockSpec((tm, LANES), lambda i: (i, 0))
    return pl.pallas_call(
        _topk_kernel,
        grid=(n // tm,),
        in_specs=[spec],
        out_specs=[spec, spec],
        out_shape=[jax.ShapeDtypeStruct((n, LANES), jnp.int32), jax.ShapeDtypeStruct((n, LANES), F32)],
        compiler_params=_params(("arbitrary",)),
        name="router_topk",
    )(logits)


def _rank_kernel(idx_ref, rank_ref, cnt_ref, run_scr):
    @pl.when(pl.program_id(0) == 0)
    def _():
        run_scr[...] = jnp.zeros_like(run_scr)

    idx = idx_ref[...]
    tm = idx.shape[0]
    lane = lax.broadcasted_iota(jnp.int32, idx.shape, 1)
    onehots = [jnp.where(lane == idx[:, j:j + 1], 1.0, 0.0) for j in range(TOP_K)]
    esum = onehots[0] + onehots[1] + onehots[2] + onehots[3]
    ri = lax.broadcasted_iota(jnp.int32, (tm, tm), 0)
    ci = lax.broadcasted_iota(jnp.int32, (tm, tm), 1)
    before = jnp.where(ci < ri, 1.0, 0.0).astype(BF16)
    base = run_scr[...] + _dot(before, esum.astype(BF16))
    rank = jnp.zeros(idx.shape, F32)
    for j in range(TOP_K):
        rank = jnp.where(lane == j, jnp.sum(onehots[j] * base, axis=-1, keepdims=True), rank)
    rank_ref[...] = rank.astype(jnp.int32)
    run_scr[...] = run_scr[...] + jnp.sum(esum, axis=0, keepdims=True)
    cnt_ref[...] = run_scr[...]


def _rank(idx):
    n = idx.shape[0]
    tm = 512
    spec = pl.BlockSpec((tm, LANES), lambda i: (i, 0))
    return pl.pallas_call(
        _rank_kernel,
        grid=(n // tm,),
        in_specs=[spec],
        out_specs=[spec, pl.BlockSpec((1, LANES), lambda i: (0, 0))],
        out_shape=[jax.ShapeDtypeStruct((n, LANES), jnp.int32), jax.ShapeDtypeStruct((1, LANES), F32)],
        scratch_shapes=[pltpu.VMEM((1, LANES), F32)],
        compiler_params=_params(("arbitrary",)),
        name="router_rank",
    )(idx)


COMBINE_ROWS = 256


def _token_copy(src_hbm, s_row, dst, r, sem, pitch):
    return pltpu.make_async_copy(src_hbm.at[pl.ds(s_row, pitch - 1)], dst.at[pl.ds(r * pitch, pitch - 1)], sem)


DMA_UNROLL = 16


def _tokens_in_flight(h_hbm, src_ref, buf, sem, n_tok, wait):
    pitch = buf.shape[0] // n_tok

    def body(i, c):
        for k in range(DMA_UNROLL):
            r = i * DMA_UNROLL + k
            cp = _token_copy(h_hbm, src_ref[r], buf, r, sem, pitch)
            cp.wait() if wait else cp.start(priority=k % 2)
        return c

    lax.fori_loop(0, n_tok // DMA_UNROLL, body, 0)


def _tokens_in_flight_inline(h_hbm, src_ref, buf, sem, n_tok, wait):
    pitch = buf.shape[0] // n_tok
    for r in range(n_tok):
        cp = _token_copy(h_hbm, src_ref[r], buf, r, sem, pitch)
        cp.wait() if wait else cp.start(priority=r % 2)


def _expert_kernel(te_ref, nv_ref, src_ref, nxt_ref, h_hbm, wgu_ref, bgu_ref, wdn_ref, bdn_ref, ys_ref,
                   xbuf0, xbuf1, sem):
    t = pl.program_id(0)
    nv = nv_ref[0]
    tm = MOE_TILE

    @pl.when(t == 0)
    def _():
        _tokens_in_flight(h_hbm, src_ref, xbuf0, sem.at[0], tm, False)

    for par, (cur, nxt) in enumerate(((xbuf0, xbuf1), (xbuf1, xbuf0))):
        @pl.when((t < nv) & (t % 2 == par))
        def _(par=par, cur=cur, nxt=nxt):
            _tokens_in_flight_inline(h_hbm, nxt_ref, nxt, sem.at[1 - par], tm, False)
            _tokens_in_flight_inline(h_hbm, src_ref, cur, sem.at[par], tm, True)
            f = wdn_ref.shape[2]
            gu = _dot(_tm_load(cur, tm).astype(BF16), wgu_ref[0, 0]) + bgu_ref[0, 0]
            gate = jnp.minimum(gu[:, :f], SWIGLU_LIMIT)
            up = jnp.clip(gu[:, f:], -SWIGLU_LIMIT, SWIGLU_LIMIT)
            act = (up + 1.0) * gate * _sigmoid(SWIGLU_ALPHA * gate)
            _tm_store(ys_ref, _dot(act.astype(BF16), wdn_ref[0, 0]) + bdn_ref[0, 0])

        @pl.when((t == nv) & (t % 2 == par))
        def _(par=par, cur=cur):
            _tokens_in_flight(h_hbm, src_ref, cur, sem.at[par], tm, True)

    @pl.when(t >= nv)
    def _():
        ys_ref[...] = jnp.zeros_like(ys_ref)


def _experts(h_tm, src, tile_expert, n_valid, layer, wgu_bf, bgu, wdn_bf, bdn):
    p_max = src.shape[0]
    _, ne, d, f2 = wgu_bf.shape
    f = f2 // 2
    sub = _tm_pitch(d)
    tm = MOE_TILE
    n_tiles = p_max // tm
    wmap = lambda t, te, nv: (layer, te[t], 0, 0)
    return pl.pallas_call(
        _expert_kernel,
        grid_spec=pltpu.PrefetchScalarGridSpec(
            num_scalar_prefetch=2,
            grid=(n_tiles,),
            in_specs=[pl.BlockSpec((tm,), lambda t, te, nv: (t,), memory_space=pltpu.SMEM),
                      pl.BlockSpec((tm,), lambda t, te, nv: (jnp.minimum(t + 1, n_tiles - 1),), memory_space=pltpu.SMEM),
                      pl.BlockSpec(memory_space=pl.ANY),
                      pl.BlockSpec((1, 1, d, f2), wmap),
                      pl.BlockSpec((1, 1, 1, f2), wmap),
                      pl.BlockSpec((1, 1, f, d), wmap),
                      pl.BlockSpec((1, 1, 1, d), wmap)],
            out_specs=pl.BlockSpec((tm * sub, LANES), lambda t, te, nv: (t, 0)),
            scratch_shapes=[pltpu.VMEM((tm * sub, LANES), F32), pltpu.VMEM((tm * sub, LANES), F32),
                            pltpu.SemaphoreType.DMA((2,))]),
        out_shape=jax.ShapeDtypeStruct((p_max * sub, LANES), F32),
        compiler_params=_params(("arbitrary",)),
        name="moe_experts",
    )(tile_expert, n_valid, src, src, h_tm, wgu_bf, bgu[:, :, None, :], wdn_bf, bdn[:, :, None, :])


def _combine_kernel(dest_ref, nxt_ref, x_ref, w_ref, mod_ref, fg_ref, ys_hbm, o_ref, buf0, buf1, sem, *, final):
    tm, d = x_ref.shape
    sub = _tm_pitch(d)
    t = pl.program_id(0)
    last = pl.num_programs(0) - 1

    def copy(idx_ref, buf, s, q):
        return _token_copy(ys_hbm, idx_ref[q], buf.at[q % TOP_K], q // TOP_K, sem.at[s], sub)

    def in_flight_loop(idx_ref, buf, s, wait):
        def body(i, c):
            for q in range(DMA_UNROLL):
                cp = copy(idx_ref, buf, s, i * DMA_UNROLL + q)
                cp.wait() if wait else cp.start(priority=q % 2)
            return c

        lax.fori_loop(0, tm * TOP_K // DMA_UNROLL, body, 0)

    @pl.when(t == 0)
    def _():
        in_flight_loop(dest_ref, buf0, 0, False)

    for par, (cur, nxt) in enumerate(((buf0, buf1), (buf1, buf0))):
        @pl.when(t % 2 == par)
        def _(par=par, cur=cur, nxt=nxt):
            for q in range(tm * TOP_K):
                copy(nxt_ref, nxt, 1 - par, q).start(priority=q % 2)
            for q in range(tm * TOP_K):
                copy(dest_ref, cur, par, q).wait()
            w = w_ref[...]
            pieces = []
            for k in range(sub - 1):
                piece = w[:, 0:1] * cur.at[0][pl.ds(k, tm, stride=sub), :]
                for j in range(1, TOP_K):
                    piece = piece + w[:, j:j + 1] * cur.at[j][pl.ds(k, tm, stride=sub), :]
                pieces.append(piece)
            out = x_ref[...] + mod_ref[0][5:6] * jnp.concatenate(pieces, axis=1)
            if final:
                ms = jnp.mean(out * out, axis=-1, keepdims=True)
                out = out * lax.rsqrt(ms + EPS) * fg_ref[...]
            o_ref[...] = out

        @pl.when((t == last) & (t % 2 == par))
        def _(par=par, nxt=nxt):
            in_flight_loop(nxt_ref, nxt, 1 - par, True)


def _combine(dest_flat, x, wts, mod, final_g, ys, seq, n_rows, final):
    d = x.shape[1]
    tm = COMBINE_ROWS
    n_steps = n_rows // tm
    row = lambda i: (i, 0)
    return pl.pallas_call(
        functools.partial(_combine_kernel, final=final),
        grid=(n_steps,),
        in_specs=[pl.BlockSpec((tm * TOP_K,), lambda i: (i,), memory_space=pltpu.SMEM),
                  pl.BlockSpec((tm * TOP_K,), lambda i: (jnp.minimum(i + 1, n_steps - 1),), memory_space=pltpu.SMEM),
                  pl.BlockSpec((tm, d), row),
                  pl.BlockSpec((tm, LANES), row),
                  pl.BlockSpec((1, 8, d), lambda i: (jnp.minimum((i * tm) // seq, 2), 0, 0)),
                  pl.BlockSpec((1, d), lambda i: (0, 0)),
                  pl.BlockSpec(memory_space=pl.ANY)],
        out_specs=pl.BlockSpec((tm, d), row),
        out_shape=jax.ShapeDtypeStruct((n_rows, d), F32),
        scratch_shapes=[pltpu.VMEM((TOP_K, tm * _tm_pitch(d), LANES), F32),
                        pltpu.VMEM((TOP_K, tm * _tm_pitch(d), LANES), F32), pltpu.SemaphoreType.DMA((2,))],
        compiler_params=_params(("arbitrary",)),
        name="moe_combine",
    )(dest_flat, dest_flat, x, wts, mod, final_g, ys)


def _moe(h2_tm, logits, x, mod, final_g, layer, wgu_bf, bgu, wdn_bf, bdn, seq, final):
    n = logits.shape[0]
    tm = MOE_TILE
    idx, wts = _topk(logits)
    rank, counts = _rank(idx)
    cnt = counts[0, :N_EXPERTS].astype(jnp.int32)
    tiles_e = (cnt + tm - 1) // tm
    tile_end = jnp.cumsum(tiles_e)
    offs = (tile_end - tiles_e) * tm
    idx4, rank4 = idx[:, :TOP_K], rank[:, :TOP_K]
    dest = (offs[idx4] + rank4).reshape(-1)
    p_max = (n * TOP_K // tm + N_EXPERTS) * tm
    n_tiles = p_max // tm
    n_valid = tile_end[-1:]
    tile_id = jnp.minimum(jnp.arange(n_tiles, dtype=jnp.int32), n_valid[0] - 1)
    tile_expert = jnp.sum((tile_end[None, :] <= tile_id[:, None]).astype(jnp.int32), axis=1)
    tile_expert = jnp.minimum(tile_expert, N_EXPERTS - 1)
    pitch = _tm_pitch(x.shape[1])
    tok_row = jnp.repeat(jnp.arange(n, dtype=jnp.int32) * pitch, TOP_K)
    src_row = jnp.zeros((p_max,), jnp.int32).at[dest].set(tok_row)
    ys = _experts(h2_tm, src_row, tile_expert, n_valid.astype(jnp.int32), layer, wgu_bf, bgu, wdn_bf, bdn)
    return _combine(dest * pitch, x, wts, mod, final_g, ys, seq, n, final)


def _hy_in_kernel(x_ref, mod_ref, ng_ref, w0, w1, w2, bin_ref, sw_ref, sb_ref, u_o, x0_o, h_scr, *, n_lat_tiles, ctx_len):
    i = pl.program_id(0)

    @pl.when(pl.program_id(1) == 0)
    def _():
        m = mod_ref[0]
        h_scr[...] = _rms_mod(x_ref[...], ng_ref[...], m[0:1], m[1:2]).astype(BF16)

    h = h_scr[...]
    tm, tn = u_o.shape
    row = lax.broadcasted_iota(jnp.int32, (tm, tn), 0)
    seg = jnp.where(i >= n_lat_tiles, ctx_len, GRID_W)
    pos = row & (seg - 1)
    first = pos == 0
    last = pos == seg - 1
    bin_ = bin_ref[...]
    sw = sw_ref[...]
    sb = sb_ref[...]
    outs = []
    for s, w in enumerate((w0, w1, w2)):
        z = _dot(h, w[...]) + bin_[s:s + 1]
        zp = jnp.where(first, 0.0, pltpu.roll(z, 1, axis=0))
        zn = jnp.where(last, 0.0, pltpu.roll(z, tm - 1, axis=0))
        outs.append(sw[0, s:s + 1] * zp + sw[1, s:s + 1] * z + sw[2, s:s + 1] * zn + sb[s:s + 1])
    x0_o[...] = outs[0]
    u_o[...] = outs[2] * outs[1]


def _hy_in(x, mod, ng, w_bf, b_in, short_w, short_b, seq, ctx_len, n_rows):
    d = x.shape[1]
    tm, tn = 512, 256
    nj = d // tn
    wspec = lambda s: pl.BlockSpec((d, tn), lambda i, j, s=s: (0, s * nj + j))
    ospec = pl.BlockSpec((tm, tn), lambda i, j: (i, j))
    osh = jax.ShapeDtypeStruct((n_rows, d), F32)
    return pl.pallas_call(
        functools.partial(_hy_in_kernel, n_lat_tiles=2 * seq // tm, ctx_len=ctx_len),
        grid=(n_rows // tm, nj),
        in_specs=[pl.BlockSpec((tm, d), lambda i, j: (i, 0)),
                  pl.BlockSpec((1, 8, d), lambda i, j: (jnp.minimum((i * tm) // seq, 2), 0, 0)),
                  pl.BlockSpec((1, d), lambda i, j: (0, 0)),
                  wspec(0), wspec(1), wspec(2),
                  pl.BlockSpec((3, tn), lambda i, j: (0, j)),
                  pl.BlockSpec((3, 3, tn), lambda i, j: (0, 0, j)),
                  pl.BlockSpec((3, tn), lambda i, j: (0, j))],
        out_specs=[ospec, ospec],
        out_shape=[osh, osh],
        scratch_shapes=[pltpu.VMEM((tm, d), BF16)],
        compiler_params=_params(("arbitrary", "arbitrary")),
        name="hyena_in_proj",
    )(x, mod, ng, w_bf, w_bf, w_bf, b_in.reshape(3, d), short_w.reshape(3, 3, d), short_b.reshape(3, d))


def _filter_kernel(z_ref, w1, b1, fr, w2, b2, w3, b3, w4, dl_ref, f_o, nrm_o, *, length):
    i = pl.program_id(0)
    z = z_ref[...]
    freq = fr[...]
    a = jnp.sin(freq * (_dot(z, w1[...]) + b1[...]))
    a = jnp.sin(freq * (_dot(a, w2[...]) + b2[...]))
    a = jnp.sin(freq * (_dot(a, w3[...]) + b3[...]))
    h = _dot(a, w4[...]) * jnp.exp(-z[:, 0:1] * dl_ref[...])
    row = lax.broadcasted_iota(jnp.int32, h.shape, 0) + i * h.shape[0]
    h = jnp.where(row == length, 0.0, h)
    f_o[...] = h

    @pl.when(i == 0)
    def _():
        nrm_o[...] = jnp.zeros_like(nrm_o)

    nrm_o[...] = nrm_o[...] + jnp.sum(jnp.abs(h), axis=0, keepdims=True)


def _pos_features(length):
    t = (np.arange(length, dtype=np.float64) / length)[:, None]
    bands = np.linspace(1e-4, HY_BANDS - 1, HY_BANDS)[None, :]
    ang = 2.0 * math.pi * t * bands
    return np.concatenate([t, np.cos(ang), -np.sin(ang)], axis=-1).astype(np.float32)


def _decay_rates(d):
    return np.abs(np.linspace(math.log(HY_DECAY_PCT_MIN) / HY_DECAY_TARGET,
                              math.log(HY_DECAY_PCT_MAX) / HY_DECAY_TARGET, d)).astype(np.float32)[None, :]


def _hyena_filter(length, w1, b1, freq, w2, b2, w3, b3, w4):
    d = w4.shape[1] // 2
    tr = min(length, 512)
    nf = length // tr
    feats = _pos_features(length)
    feats = np.concatenate([feats, feats[:1], feats[:0:-1]], axis=0)
    z = jnp.asarray(np.pad(feats, ((0, 0), (0, LANES - feats.shape[1]))))
    const = lambda i: (0, 0)
    full = lambda a: pl.BlockSpec(a.shape, const)
    mat = lambda a: jnp.pad(a, ((0, LANES - a.shape[0]), (0, LANES - a.shape[1])))
    vec = lambda a: jnp.pad(a, (0, LANES - a.shape[0])).reshape(1, LANES)
    w4p = jnp.pad(w4, ((0, LANES - w4.shape[0]), (0, 0)))
    args = [z, mat(w1), vec(b1), vec(freq), mat(w2), vec(b2), mat(w3), vec(b3), w4p, jnp.asarray(_decay_rates(d))]
    in_specs = [pl.BlockSpec((tr, LANES), lambda i: (i, 0))] + [full(a) for a in args[1:]]
    in_specs[8] = pl.BlockSpec((LANES, d), lambda i: (0, i // nf))
    return pl.pallas_call(
        functools.partial(_filter_kernel, length=length),
        grid=(2 * nf,),
        in_specs=in_specs,
        out_specs=[pl.BlockSpec((tr, d), lambda i: (i, 0)), pl.BlockSpec((1, d), const)],
        out_shape=[jax.ShapeDtypeStruct((2 * length, d), F32), jax.ShapeDtypeStruct((1, d), F32)],
        compiler_params=_params(("arbitrary",)),
        name="hyena_filter",
    )(*args)


def _dft_mats(m, k, n):
    r = np.arange(m, dtype=np.int64)[:, None]
    c = np.arange(k, dtype=np.int64)[None, :]
    ang = 2.0 * math.pi * ((r * c) % n).astype(np.float64) / n
    return np.cos(ang).astype(np.float32), np.sin(ang).astype(np.float32)


def _cmul_mat(c, s, xr, xi, conj):
    xr = xr.astype(BF16)
    cr, sr = _dot(c, xr), _dot(s, xr)
    if xi is None:
        return cr, (sr if conj else -sr)
    xi = xi.astype(BF16)
    ci, si = _dot(c, xi), _dot(s, xi)
    if conj:
        return cr - si, ci + sr
    return cr + si, ci - sr


def _slab_fwd_kernel(*refs, has_imag, has_filter):
    refs = list(refs)
    fc, fs, xr_ref = refs[:3]
    xi_ref = refs[3] if has_imag else None
    rest = refs[3 + has_imag:]
    if has_filter:
        gr_ref, gi_ref, gn_ref, or_ref, oi_ref = rest
    else:
        or_ref, oi_ref = rest
    yr, yi = _cmul_mat(fc[...].astype(BF16), fs[...].astype(BF16), xr_ref[...],
                       xi_ref[...] if has_imag else None, False)
    if has_filter:
        gn = 1.0 / gn_ref[...]
        gr, gi = gr_ref[...] * gn, gi_ref[...] * gn
        yr, yi = yr * gr - yi * gi, yr * gi + yi * gr
    or_ref[...] = yr
    oi_ref[...] = yi


def _slab_fwd(x2, k_in, parts, n_out, n, spectrum=None):
    cols = x2.shape[1]
    tc = min(cols, 2048 if n_out <= LANES else 512)
    fc, fs = _dft_mats(n_out, k_in, n)
    const = lambda c: (0, 0)
    in_specs = [pl.BlockSpec((n_out, k_in), const), pl.BlockSpec((n_out, k_in), const)]
    args = [jnp.asarray(fc), jnp.asarray(fs)]
    for part in parts:
        if part is not None:
            in_specs.append(pl.BlockSpec((k_in, tc), lambda c, part=part: (part, c)))
            args.append(x2)
    ospec = pl.BlockSpec((n_out, tc), lambda c: (0, c))
    if spectrum is not None:
        in_specs += [ospec, ospec, pl.BlockSpec((1, tc), lambda c: (0, c))]
        args += list(spectrum)
    osh = jax.ShapeDtypeStruct((n_out, cols), F32)
    return pl.pallas_call(
        functools.partial(_slab_fwd_kernel, has_imag=parts[1] is not None, has_filter=spectrum is not None),
        grid=(cols // tc,),
        in_specs=in_specs,
        out_specs=[ospec, ospec],
        out_shape=[osh, osh],
        compiler_params=_params(("arbitrary",)),
        name="dft_slab_fwd",
    )(*args)


def _mid_kernel(*refs, has_filter):
    if has_filter:
        fc, fs, twc_ref, tws_ref, ar_ref, ai_ref, gr_ref, gi_ref, gn_ref, or_ref, oi_ref = refs
    else:
        fc, fs, twc_ref, tws_ref, ar_ref, ai_ref, or_ref, oi_ref = refs
    td = ar_ref.shape[2]
    c, s = fc[...].astype(BF16), fs[...].astype(BF16)
    twc = jnp.concatenate([twc_ref[0]] * (td // LANES), axis=1)
    tws = jnp.concatenate([tws_ref[0]] * (td // LANES), axis=1)
    ar, ai = ar_ref[0], ai_ref[0]
    pr, pi_ = ar * twc + ai * tws, ai * twc - ar * tws
    br, bi = _cmul_mat(c, s, pr, pi_, False)
    if not has_filter:
        or_ref[0] = br
        oi_ref[0] = bi
        return
    gn = 1.0 / gn_ref[...]
    gr, gi = gr_ref[0] * gn, gi_ref[0] * gn
    yr, yi = br * gr - bi * gi, br * gi + bi * gr
    qr, qi = _cmul_mat(c, s, yr, yi, True)
    or_ref[0] = qr * twc - qi * tws
    oi_ref[0] = qr * tws + qi * twc


def _mid_stage(a_re, a_im, n, spectrum=None):
    n1, _, d = a_re.shape
    td = 1024
    fc, fs = _dft_mats(LANES, LANES, LANES)
    k1 = lax.broadcasted_iota(jnp.int32, (n1, LANES, LANES), 0)
    m2 = lax.broadcasted_iota(jnp.int32, (n1, LANES, LANES), 1)
    ang = ((k1 * m2) % n).astype(F32) * (2.0 * math.pi / n)
    twc, tws = jnp.cos(ang), jnp.sin(ang)
    const = lambda s, j: (0, 0)
    slab = pl.BlockSpec((1, LANES, td), lambda s, j: (s, 0, j))
    tw = pl.BlockSpec((1, LANES, LANES), lambda s, j: (s, 0, 0))
    in_specs = [pl.BlockSpec((LANES, LANES), const), pl.BlockSpec((LANES, LANES), const), tw, tw, slab, slab]
    args = [jnp.asarray(fc), jnp.asarray(fs), twc, tws, a_re, a_im]
    if spectrum is not None:
        in_specs += [slab, slab, pl.BlockSpec((1, td), lambda s, j: (0, j))]
        args += list(spectrum)
    osh = jax.ShapeDtypeStruct((n1, LANES, d), F32)
    return pl.pallas_call(
        functools.partial(_mid_kernel, has_filter=spectrum is not None),
        grid=(n1, d // td),
        in_specs=in_specs,
        out_specs=[slab, slab],
        out_shape=[osh, osh],
        compiler_params=_params(("arbitrary", "arbitrary")),
        name="dft_mid",
    )(*args)


def _slab_inv_kernel(fc, fs, pr_ref, pi_ref, u0_ref, u1_ref, a0_ref, a1_ref, skip_ref, o_ref, *, scale):
    yr, yi = _cmul_mat(fc[...].astype(BF16), fs[...].astype(BF16), pr_ref[...], pi_ref[...], True)
    skip = skip_ref[...]
    o_ref[0] = ((yr * scale + u0_ref[...] * skip) * a0_ref[...]).astype(o_ref.dtype)
    o_ref[1] = ((yi * scale + u1_ref[...] * skip) * a1_ref[...]).astype(o_ref.dtype)


def _slab_inv(p_re, p_im, u2, x02, k_out, parts, skip, n, scale):
    n_in, cols = p_re.shape
    d = skip.shape[1]
    tc = min(cols, d if n_in <= LANES else 512)
    fc, fs = _dft_mats(k_out, n_in, n)
    const = lambda c: (0, 0)
    pspec = pl.BlockSpec((n_in, tc), lambda c: (0, c))
    uspec = lambda part: pl.BlockSpec((k_out, tc), lambda c, part=part: (part, c))
    return pl.pallas_call(
        functools.partial(_slab_inv_kernel, scale=scale),
        grid=(cols // tc,),
        in_specs=[pl.BlockSpec((k_out, n_in), const), pl.BlockSpec((k_out, n_in), const),
                  pspec, pspec, uspec(parts[0]), uspec(parts[1]), uspec(parts[0]), uspec(parts[1]),
                  pl.BlockSpec((1, tc), lambda c: (0, c % (d // tc)))],
        out_specs=pl.BlockSpec((2, k_out, tc), lambda c: (0, 0, c)),
        out_shape=jax.ShapeDtypeStruct((2, k_out, cols), BF16),
        compiler_params=_params(("arbitrary",)),
        name="dft_slab_inv",
    )(jnp.asarray(fc), jnp.asarray(fs), p_re, p_im, u2, u2, x02, x02, skip)


def _long_conv(u, x0, row0, length, filt, nrm, skip):
    d = u.shape[1]
    n = 2 * length
    if n <= 4 * LANES:
        parts = (row0 // length, row0 // length + 1)
        g_re, g_im = _slab_fwd(filt, n, (0, None), n, n)
        y_re, y_im = _slab_fwd(u, length, parts, n, n, spectrum=(g_re, g_im, nrm))
        y = _slab_inv(y_re, y_im, u, x0, length, parts, skip, n, 1.0 / n)
        return y.reshape(2 * length, d)
    n1 = n // LANES
    k1 = n1 // 2
    parts = (row0 // length, row0 // length + 1)
    slabs = lambda a: a.reshape(a.shape[0] // LANES, LANES, d)
    f_re, f_im = _slab3_fwd(slabs(filt), n1, (0, None), n1)
    g_re, g_im = _mid_stage(f_re, f_im, n)
    a_re, a_im = _slab3_fwd(slabs(u), k1, parts, n1)
    p_re, p_im = _mid_stage(a_re, a_im, n, spectrum=(g_re, g_im, nrm))
    y = _slab3_inv(p_re, p_im, slabs(u), slabs(x0), k1, parts, skip, 1.0 / n)
    return y.reshape(2 * length, d)


SLAB_ROWS = 16
SLAB_COLS = 256


def _to_lanes(x3):
    xs = jnp.swapaxes(x3, 0, 1)
    return jnp.concatenate([xs[r] for r in range(xs.shape[0])], axis=1)


def _from_lanes(y2, r):
    c = y2.shape[1] // r
    return jnp.swapaxes(jnp.stack([y2[:, i * c:(i + 1) * c] for i in range(r)], axis=0), 0, 1)


def _slab3_fwd_kernel(*refs, has_imag):
    fc, fs, xr_ref = refs[:3]
    xi_ref = refs[3] if has_imag else None
    or_ref, oi_ref = refs[3 + has_imag:]
    yr, yi = _cmul_mat(fc[...].astype(BF16), fs[...].astype(BF16), _to_lanes(xr_ref[...]),
                       _to_lanes(xi_ref[...]) if has_imag else None, False)
    or_ref[...] = _from_lanes(yr, or_ref.shape[1])
    oi_ref[...] = _from_lanes(yi, oi_ref.shape[1])


def _slab3_fwd(x3, k_in, parts, n1):
    _, rows, d = x3.shape
    fc, fs = _dft_mats(n1, k_in, n1)
    const = lambda g, j: (0, 0)
    in_specs = [pl.BlockSpec((n1, k_in), const), pl.BlockSpec((n1, k_in), const)]
    args = [jnp.asarray(fc), jnp.asarray(fs)]
    for part in parts:
        if part is not None:
            in_specs.append(pl.BlockSpec((k_in, SLAB_ROWS, SLAB_COLS), lambda g, j, part=part: (part, g, j)))
            args.append(x3)
    ospec = pl.BlockSpec((n1, SLAB_ROWS, SLAB_COLS), lambda g, j: (0, g, j))
    osh = jax.ShapeDtypeStruct((n1, rows, d), F32)
    return pl.pallas_call(
        functools.partial(_slab3_fwd_kernel, has_imag=parts[1] is not None),
        grid=(rows // SLAB_ROWS, d // SLAB_COLS),
        in_specs=in_specs,
        out_specs=[ospec, ospec],
        out_shape=[osh, osh],
        compiler_params=_params(("arbitrary", "arbitrary")),
        name="dft_slab3_fwd",
    )(*args)


def _slab3_inv_kernel(fc, fs, pr_ref, pi_ref, u0_ref, u1_ref, a0_ref, a1_ref, skip_ref, o_ref, *, scale):
    yr, yi = _cmul_mat(fc[...].astype(BF16), fs[...].astype(BF16), _to_lanes(pr_ref[...]), _to_lanes(pi_ref[...]), True)
    r = u0_ref.shape[1]
    skip = skip_ref[...]
    o_ref[0] = ((_from_lanes(yr, r) * scale + u0_ref[...] * skip) * a0_ref[...]).astype(o_ref.dtype)
    o_ref[1] = ((_from_lanes(yi, r) * scale + u1_ref[...] * skip) * a1_ref[...]).astype(o_ref.dtype)


def _slab3_inv(p_re, p_im, u3, x03, k_out, parts, skip, scale):
    n1, rows, d = p_re.shape
    fc, fs = _dft_mats(k_out, n1, n1)
    const = lambda g, j: (0, 0)
    pspec = pl.BlockSpec((n1, SLAB_ROWS, SLAB_COLS), lambda g, j: (0, g, j))
    uspec = lambda part: pl.BlockSpec((k_out, SLAB_ROWS, SLAB_COLS), lambda g, j, part=part: (part, g, j))
    return pl.pallas_call(
        functools.partial(_slab3_inv_kernel, scale=scale),
        grid=(rows // SLAB_ROWS, d // SLAB_COLS),
        in_specs=[pl.BlockSpec((k_out, n1), const), pl.BlockSpec((k_out, n1), const),
                  pspec, pspec, uspec(parts[0]), uspec(parts[1]), uspec(parts[0]), uspec(parts[1]),
                  pl.BlockSpec((1, SLAB_COLS), lambda g, j: (0, j))],
        out_specs=pl.BlockSpec((2, k_out, SLAB_ROWS, SLAB_COLS), lambda g, j: (0, 0, g, j)),
        out_shape=jax.ShapeDtypeStruct((2, k_out, rows, d), BF16),
        compiler_params=_params(("arbitrary", "arbitrary")),
        name="dft_slab3_inv",
    )(jnp.asarray(fc), jnp.asarray(fs), p_re, p_im, u3, u3, x03, x03, skip)


def kernel(x, c, ctx, c_ctx, ada_w, ada_b, norm_mix_g, norm_ffn_g, final_norm_g, hg_w_in, hg_lb, hg_onorm_g, hg_w_out, hy_w_in, hy_b_in, hy_short_w, hy_short_b, hy_f_w1, hy_f_b1, hy_f_freq, hy_f_w2, hy_f_b2, hy_f_w3, hy_f_b3, hy_f_w4, hy_skip, hy_w_out, hy_b_out, moe_w_router, moe_b_router, moe_w_gu, moe_b_gu, moe_w_down, moe_b_down):
    bsz, seq, d = x.shape
    ctx_len = ctx.shape[1]
    depth = ada_w.shape[0]
    n_lat = bsz * seq
    n_all = n_lat + bsz * ctx_len
    assert bsz == 2 and seq % MOE_TILE == 0 and ctx_len == GLA_ROWS

    cond8 = jnp.zeros((8, d), F32).at[0:2].set(c).at[2].set(c_ctx)
    mods = _ada_all(cond8, ada_w, ada_b).reshape(depth, 8, N_MOD, d)[:, :3]
    mods = jnp.pad(mods, ((0, 0), (0, 0), (0, 8 - N_MOD), (0, 0)))

    lb_soft = jax.nn.softmax(hg_lb.astype(F32), axis=0)
    lower_bounds = jnp.cumsum(lb_soft, axis=0) - lb_soft[0]

    xa = jnp.concatenate([x.reshape(n_lat, d), ctx.reshape(bsz * ctx_len, d)], axis=0)
    zero_bias = jnp.zeros((1, d), F32)
    final_g = final_norm_g.reshape(1, d)
    wgu_bf, wdn_bf = moe_w_gu.astype(BF16), moe_w_down.astype(BF16)
    ctx_needed = [any(l % 2 == 0 for l in range(layer + 1, depth)) for layer in range(depth)]

    for layer in range(depth):
        j = layer // 2
        is_hgrn = layer % 2 == 0
        keep_ctx = ctx_needed[layer]
        n_out = n_all if keep_ctx else n_lat
        mod = mods[layer]
        ng1 = norm_mix_g[layer].reshape(1, d)
        ng2 = norm_ffn_g[layer].reshape(1, d)
        if is_hgrn:
            q, kf, lff, kb, lfb, v, g = _hg_in(xa, mod, ng1, lower_bounds[j].reshape(1, d),
                                               hg_w_in[j].astype(BF16), seq, n_all)
            o_fw = _gla_pass(q, kf, v, lff, seq, ctx_len, False)
            y = _gla_pass(q, kb, v, lfb, seq, ctx_len, True,
                          extra=(o_fw, g, hg_onorm_g[j].reshape(1, HG_DK)))
            w_out, b_out = hg_w_out[j].astype(BF16), zero_bias
        else:
            n_in = n_all if keep_ctx else n_lat
            u, x0 = _hy_in(xa, mod, ng1, hy_w_in[j].astype(BF16), hy_b_in[j], hy_short_w[j], hy_short_b[j],
                           seq, ctx_len, n_in)
            fpar = (hy_f_w1[j], hy_f_b1[j], hy_f_freq[j], hy_f_w2[j], hy_f_b2[j], hy_f_w3[j], hy_f_b3[j], hy_f_w4[j])
            skip = hy_skip[j].reshape(1, d)
            y = _long_conv(u, x0, 0, seq, *_hyena_filter(seq, *fpar), skip)
            if keep_ctx:
                yc = _long_conv(u, x0, n_lat, ctx_len, *_hyena_filter(ctx_len, *fpar), skip)
                y = jnp.concatenate([y, yc], axis=0)
            w_out, b_out = hy_w_out[j].astype(BF16), hy_b_out[j].reshape(1, d)
        wr = jnp.pad(moe_w_router[layer], ((0, 0), (0, LANES - N_EXPERTS)))
        br = jnp.pad(moe_b_router[layer], (0, LANES - N_EXPERTS)).reshape(1, LANES)
        xa, h2, logits = _out_proj(y, w_out, b_out, xa, mod, ng2, wr, br, seq, n_out)
        xa = _moe(h2, logits, xa, mod, final_g, layer, wgu_bf, moe_b_gu, wdn_bf, moe_b_down, seq, layer == depth - 1)
    return xa.reshape(bsz, seq, d)
```

```python
import functools
import math

import numpy as np
import jax
import jax.numpy as jnp
from jax import lax
from jax.experimental import pallas as pl
from jax.experimental.pallas import tpu as pltpu

F32 = jnp.float32
BF16 = jnp.bfloat16

EPS = 1e-6
LOG2_E = 1.0 / math.log(2.0)
N_MOD = 6
LANES = 128
GRID_W = 64
HG_DK = 128
GLA_CHUNK = 128
GLA_ROWS = 256
GLA_HEADS = 4
HY_BANDS = 16
HY_DECAY_PCT_MIN = 0.3
HY_DECAY_PCT_MAX = 1.5
HY_DECAY_TARGET = 1e-2
N_EXPERTS = 32
TOP_K = 4
SWIGLU_ALPHA = 1.702
SWIGLU_LIMIT = 7.0
MOE_TILE = 512
VMEM_LIMIT = 56 * 1024 * 1024


def _params(sem, vmem=VMEM_LIMIT):
    return pltpu.CompilerParams(dimension_semantics=sem, vmem_limit_bytes=vmem)


def _sigmoid(x):
    return 1.0 / (1.0 + jnp.exp(-x))


def _dot(a, b):
    return jnp.dot(a, b, preferred_element_type=F32)


def _dot_nt(a, b):
    return lax.dot_general(a, b, (((1,), (1,)), ((), ())), preferred_element_type=F32)


def _split(a):
    hi = a.astype(BF16)
    lo = (a - hi.astype(F32)).astype(BF16)
    return hi, lo


def _dot3(a, b):
    (ah, al), (bh, bl) = a, b
    return _dot(ah, bh) + _dot(ah, bl) + _dot(al, bh)


def _rms_mod(x, g, shift, scale):
    ms = jnp.mean(x * x, axis=-1, keepdims=True)
    return x * lax.rsqrt(ms + EPS) * g * (1.0 + scale) + shift


def _ada_kernel(c_ref, w_ref, b_ref, o_ref):
    c = c_ref[...]
    o_ref[0] = _dot(c * _sigmoid(c), w_ref[0]) + b_ref[0]


def _ada_all(cond8, ada_w, ada_b):
    depth, d, nd = ada_w.shape
    tn = 1024
    return pl.pallas_call(
        _ada_kernel,
        grid=(depth, nd // tn),
        in_specs=[pl.BlockSpec((8, d), lambda l, j: (0, 0)),
                  pl.BlockSpec((1, d, tn), lambda l, j: (l, 0, j)),
                  pl.BlockSpec((1, 1, tn), lambda l, j: (l, 0, j))],
        out_specs=pl.BlockSpec((1, 8, tn), lambda l, j: (l, 0, j)),
        out_shape=jax.ShapeDtypeStruct((depth, 8, nd), F32),
        compiler_params=_params(("arbitrary", "arbitrary")),
        name="ada_mod",
    )(cond8, ada_w, ada_b.reshape(depth, 1, nd))


def _hg_in_kernel(x_ref, mod_ref, ng_ref, lb_ref, wq, wf, wb, wi, wg,
                  q_o, kf_o, lff_o, kb_o, lfb_o, v_o, g_o, h_scr):
    @pl.when(pl.program_id(1) == 0)
    def _():
        m = mod_ref[0]
        h_scr[...] = _rms_mod(x_ref[...], ng_ref[...], m[0:1], m[1:2]).astype(BF16)

    h = h_scr[...]
    q = _dot(h, wq[...])
    q_o[...] = q * _sigmoid(q) * (HG_DK ** -0.5)
    lb = lb_ref[...]
    for w, k_o, lf_o in ((wf, kf_o, lff_o), (wb, kb_o, lfb_o)):
        f = lb + (1.0 - lb) * _sigmoid(_dot(h, w[...]))
        k_o[...] = 1.0 - f
        lf_o[...] = jnp.log(f) * LOG2_E
    v_o[...] = _dot(h, wi[...])
    g_o[...] = _dot(h, wg[...])


def _hg_in(x, mod, ng, lb, w_bf, seq, n_rows):
    d = x.shape[1]
    tm, tn = 512, 512
    nj = d // tn
    grp = lambda i, j: (jnp.minimum((i * tm) // seq, 2), 0, 0)
    wspec = lambda s: pl.BlockSpec((d, tn), lambda i, j, s=s: (0, s * nj + j))
    ospec = pl.BlockSpec((tm, tn), lambda i, j: (i, j))
    osh = jax.ShapeDtypeStruct((n_rows, d), F32)
    return pl.pallas_call(
        _hg_in_kernel,
        grid=(n_rows // tm, nj),
        in_specs=[pl.BlockSpec((tm, d), lambda i, j: (i, 0)),
                  pl.BlockSpec((1, 8, d), grp),
                  pl.BlockSpec((1, d), lambda i, j: (0, 0)),
                  pl.BlockSpec((1, tn), lambda i, j: (0, j))] + [wspec(s) for s in range(5)],
        out_specs=[ospec] * 7,
        out_shape=[osh] * 7,
        scratch_shapes=[pltpu.VMEM((tm, d), BF16)],
        compiler_params=_params(("arbitrary", "arbitrary")),
        name="hgrn_in_proj",
    )(x, mod, ng, lb, w_bf, w_bf, w_bf, w_bf, w_bf)


def _level_map(c, reverse):
    i = np.arange(c)[:, None]
    j = np.arange(c)[None, :]
    x = i ^ j
    lvl = np.where(x > 0, np.floor(np.log2(np.maximum(x, 1))), -1).astype(np.int32)
    side = (i < j) if reverse else (i > j)
    return np.where(side, lvl, -1).astype(np.int32)


def _tri_ones(c, reverse):
    i = np.arange(c)[:, None]
    m = np.arange(c)[None, :]
    return ((m >= i) if reverse else (m <= i)).astype(np.float32)


def _cumsum_rows(x, tri):
    hi = x.astype(BF16)
    r1 = x - hi.astype(F32)
    mid = r1.astype(BF16)
    lo = (r1 - mid.astype(F32)).astype(BF16)
    return _dot(tri, hi) + _dot(tri, mid) + _dot(tri, lo)


def _neg_abs(x):
    return pltpu.bitcast(pltpu.bitcast(x, jnp.uint32) | jnp.uint32(0x80000000), F32)


def _gla_chunk(q, k, v, lf, st, lvl, tri, row, reverse):
    c, wd = q.shape
    nh = wd // HG_DK
    b = _cumsum_rows(lf, tri)
    lf_up = pltpu.roll(lf, c - 1, axis=0)
    lf_dn = pltpu.roll(lf, 1, axis=0)
    att = [jnp.zeros((c, c), F32) for _ in range(nh)]
    p = 0
    while (1 << p) < c:
        s = 1 << p
        is_query = ((row & s) == 0) if reverse else ((row & s) != 0)
        if s == 1:
            a = jnp.where(is_query, lf, 0.0)
        elif s == 2:
            m = row & 3
            if reverse:
                a = jnp.where(m == 0, lf + lf_up, jnp.where(m == 1, lf, jnp.where(m == 2, 0.0, lf_dn)))
            else:
                a = jnp.where(m == 0, lf_up, jnp.where(m == 1, 0.0, jnp.where(m == 2, lf, lf + lf_dn)))
        else:
            nb = c // (2 * s)
            r = s if reverse else s - 1
            b3 = b.reshape(nb, 2 * s, wd)
            br = jnp.broadcast_to(b3[:, r:r + 1, :], (nb, 2 * s, wd)).reshape(c, wd)
            a = _neg_abs(b - br)
        z = (jnp.where(is_query, q, k) * jnp.exp2(a)).astype(BF16)
        for h in range(nh):
            zh = z[:, h * HG_DK:(h + 1) * HG_DK]
            att[h] = jnp.where(lvl == p, _dot_nt(zh, zh), att[h])
        p += 1
    tot = b[0:1] if reverse else b[c - 1:c]
    qe = (q * jnp.exp2(b)).astype(BF16)
    kd = (k * jnp.exp2(tot - b)).astype(BF16)
    dec = jnp.exp2(tot)
    outs, new_st = [], []
    for h in range(nh):
        cs = slice(h * HG_DK, (h + 1) * HG_DK)
        vh = v[:, cs]
        dsum = jnp.sum(q[:, cs] * k[:, cs], axis=-1, keepdims=True)
        o = _dot(att[h].astype(BF16), vh.astype(BF16)) + dsum * vh
        o = o + _dot_nt(qe[:, cs], st[h].astype(BF16))
        outs.append(o)
        new_st.append(st[h] * dec[:, cs] + _dot(vh.T.astype(BF16), kd[:, cs]))
    return jnp.concatenate(outs, axis=1), new_st


def _gla_kernel(*refs, reverse, gated):
    if gated:
        q_ref, k_ref, v_ref, lf_ref, lvl_ref, tri_ref, ofw_ref, g_ref, on_ref, y_ref, s_scr = refs
    else:
        q_ref, k_ref, v_ref, lf_ref, lvl_ref, tri_ref, y_ref, s_scr = refs
    c = GLA_CHUNK
    rows, wd = q_ref.shape
    nh = wd // HG_DK
    nchunk = rows // c

    @pl.when(pl.program_id(2) == 0)
    def _():
        s_scr[...] = jnp.zeros_like(s_scr)

    lvl = lvl_ref[...]
    tri = tri_ref[...]
    row = lax.broadcasted_iota(jnp.int32, (c, wd), 0)

    def body(i, carry):
        ci = (nchunk - 1 - i) if reverse else i
        rs = pl.ds(pl.multiple_of(ci * c, c), c)
        st = [s_scr[h] for h in range(nh)]
        o, st = _gla_chunk(q_ref[rs, :], k_ref[rs, :], v_ref[rs, :], lf_ref[rs, :], st, lvl, tri, row, reverse)
        for h in range(nh):
            s_scr[h] = st[h]
        if gated:
            o = o + ofw_ref[rs, :]
            g = g_ref[rs, :]
            on = on_ref[...]
            ys = []
            for h in range(nh):
                oh = o[:, h * HG_DK:(h + 1) * HG_DK]
                ms = jnp.mean(oh * oh, axis=-1, keepdims=True)
                ys.append(oh * lax.rsqrt(ms + EPS) * on)
            y = jnp.concatenate(ys, axis=1) * (g * _sigmoid(g))
            y_ref[rs, :] = y.astype(y_ref.dtype)
        else:
            y_ref[rs, :] = o
        return carry

    lax.fori_loop(0, nchunk, body, 0)


def _gla_pass(q, k, v, lf, seq, ctx_len, reverse, extra=None):
    n_all, d = q.shape
    rows, wd = GLA_ROWS, GLA_HEADS * HG_DK
    nlat, nctx = seq // rows, ctx_len // rows
    ctx_base = 2 * nlat

    def rmap(b, hh, t):
        tc = (nctx - 1 - t) if reverse else t
        tl = (nlat - 1 - (t - nctx)) if reverse else (t - nctx)
        return (jnp.where(t < nctx, ctx_base + b * nctx + tc, b * nlat + tl), hh)

    blk = pl.BlockSpec((rows, wd), rmap)
    lvl = jnp.asarray(_level_map(GLA_CHUNK, reverse))
    tri = jnp.asarray(_tri_ones(GLA_CHUNK, reverse), dtype=BF16)
    sq = pl.BlockSpec((GLA_CHUNK, GLA_CHUNK), lambda b, hh, t: (0, 0))
    in_specs = [blk, blk, blk, blk, sq, sq]
    args = [q, k, v, lf, lvl, tri]
    gated = extra is not None
    if gated:
        in_specs += [blk, blk, pl.BlockSpec((1, HG_DK), lambda b, hh, t: (0, 0))]
        args += list(extra)
    return pl.pallas_call(
        functools.partial(_gla_kernel, reverse=reverse, gated=gated),
        grid=(2, d // wd, nlat + nctx),
        in_specs=in_specs,
        out_specs=blk,
        out_shape=jax.ShapeDtypeStruct((n_all, d), BF16 if gated else F32),
        scratch_shapes=[pltpu.VMEM((GLA_HEADS, HG_DK, HG_DK), F32)],
        compiler_params=_params(("arbitrary", "arbitrary", "arbitrary")),
        name="gla_scan_bwd" if reverse else "gla_scan_fwd",
    )(*args)


def _tm_pitch(d):
    return d // LANES + 1


def _tm_store(ref, val):
    rows, d = val.shape
    pitch = _tm_pitch(d)
    for k in range(pitch - 1):
        ref[pl.ds(k, rows, stride=pitch), :] = val[:, k * LANES:(k + 1) * LANES]
    ref[pl.ds(pitch - 1, rows, stride=pitch), :] = jnp.zeros((rows, LANES), val.dtype)


def _tm_load(ref, rows):
    pitch = ref.shape[0] // rows
    return jnp.concatenate([ref[pl.ds(k, rows, stride=pitch), :] for k in range(pitch - 1)], axis=1)


def _out_kernel(y_ref, w_ref, b_ref, x_ref, mod_ref, ng_ref, wr_ref, br_ref, xo_ref, h2_ref, lg_ref):
    m = mod_ref[0]
    xn = x_ref[...] + m[2:3] * (_dot(y_ref[...], w_ref[...]) + b_ref[...])
    xo_ref[...] = xn
    h2 = _rms_mod(xn, ng_ref[...], m[3:4], m[4:5])
    _tm_store(h2_ref, h2)
    lg_ref[...] = _dot3(_split(h2), _split(wr_ref[...])) + br_ref[...]


def _out_proj(y, w_bf, bias, x, mod, ng, wr_pad, br_pad, seq, n_rows):
    d = x.shape[1]
    tm = 512
    row = lambda i: (i, 0)
    const = lambda i: (0, 0)
    return pl.pallas_call(
        _out_kernel,
        grid=(n_rows // tm,),
        in_specs=[pl.BlockSpec((tm, d), row),
                  pl.BlockSpec((d, d), const),
                  pl.BlockSpec((1, d), const),
                  pl.BlockSpec((tm, d), row),
                  pl.BlockSpec((1, 8, d), lambda i: (jnp.minimum((i * tm) // seq, 2), 0, 0)),
                  pl.BlockSpec((1, d), const),
                  pl.BlockSpec((d, LANES), const),
                  pl.BlockSpec((1, LANES), const)],
        out_specs=[pl.BlockSpec((tm, d), row), pl.BlockSpec((tm * _tm_pitch(d), LANES), row),
                   pl.BlockSpec((tm, LANES), row)],
        out_shape=[jax.ShapeDtypeStruct((n_rows, d), F32), jax.ShapeDtypeStruct((n_rows * _tm_pitch(d), LANES), F32),
                   jax.ShapeDtypeStruct((n_rows, LANES), F32)],
        compiler_params=_params(("arbitrary",)),
        name="out_proj",
    )(y, w_bf, bias, x, mod, ng, wr_pad, br_pad)


def _topk_kernel(lg_ref, idx_ref, w_ref):
    l = lg_ref[...]
    lane = lax.broadcasted_iota(jnp.int32, l.shape, 1).astype(F32)
    l = jnp.where(lane < N_EXPERTS, l, -jnp.inf)
    idx_out = jnp.zeros(l.shape, F32)
    vals = []
    for r in range(TOP_K):
        m = jnp.max(l, axis=-1, keepdims=True)
        sel = jnp.min(jnp.where(l == m, lane, float(LANES)), axis=-1, keepdims=True)
        idx_out = jnp.where(lane == r, sel, idx_out)
        l = jnp.where(lane == sel, -jnp.inf, l)
        vals.append(m)
    es = [jnp.exp(v - vals[0]) for v in vals]
    tot = es[0] + es[1] + es[2] + es[3]
    w_out = jnp.zeros(l.shape, F32)
    for r in range(TOP_K):
        w_out = jnp.where(lane == r, es[r] / tot, w_out)
    idx_ref[...] = idx_out.astype(jnp.int32)
    w_ref[...] = w_out


def _topk(logits):
    n = logits.shape[0]
    tm = 512
    spec = pl.BlockSpec((tm, LANES), lambda i: (i, 0))
    return pl.pallas_call(
        _topk_kernel,
        grid=(n // tm,),
        in_specs=[spec],
        out_specs=[spec, spec],
        out_shape=[jax.ShapeDtypeStruct((n, LANES), jnp.int32), jax.ShapeDtypeStruct((n, LANES), F32)],
        compiler_params=_params(("arbitrary",)),
        name="router_topk",
    )(logits)


def _rank_kernel(idx_ref, rank_ref, cnt_ref, run_scr):
    @pl.when(pl.program_id(0) == 0)
    def _():
        run_scr[...] = jnp.zeros_like(run_scr)

    idx = idx_ref[...]
    tm = idx.shape[0]
    lane = lax.broadcasted_iota(jnp.int32, idx.shape, 1)
    onehots = [jnp.where(lane == idx[:, j:j + 1], 1.0, 0.0) for j in range(TOP_K)]
    esum = onehots[0] + onehots[1] + onehots[2] + onehots[3]
    ri = lax.broadcasted_iota(jnp.int32, (tm, tm), 0)
    ci = lax.broadcasted_iota(jnp.int32, (tm, tm), 1)
    before = jnp.where(ci < ri, 1.0, 0.0).astype(BF16)
    base = run_scr[...] + _dot(before, esum.astype(BF16))
    rank = jnp.zeros(idx.shape, F32)
    for j in range(TOP_K):
        rank = jnp.where(lane == j, jnp.sum(onehots[j] * base, axis=-1, keepdims=True), rank)
    rank_ref[...] = rank.astype(jnp.int32)
    run_scr[...] = run_scr[...] + jnp.sum(esum, axis=0, keepdims=True)
    cnt_ref[...] = run_scr[...]


def _rank(idx):
    n = idx.shape[0]
    tm = 512
    spec = pl.BlockSpec((tm, LANES), lambda i: (i, 0))
    return pl.pallas_call(
        _rank_kernel,
        grid=(n // tm,),
        in_specs=[spec],
        out_specs=[spec, pl.BlockSpec((1, LANES), lambda i: (0, 0))],
        out_shape=[jax.ShapeDtypeStruct((n, LANES), jnp.int32), jax.ShapeDtypeStruct((1, LANES), F32)],
        scratch_shapes=[pltpu.VMEM((1, LANES), F32)],
        compiler_params=_params(("arbitrary",)),
        name="router_rank",
    )(idx)


COMBINE_ROWS = 256


def _token_copy(src_hbm, s_row, dst, r, sem, pitch):
    return pltpu.make_async_copy(src_hbm.at[pl.ds(s_row, pitch - 1)], dst.at[pl.ds(r * pitch, pitch - 1)], sem)


DMA_UNROLL = 16


def _tokens_in_flight(h_hbm, src_ref, buf, sem, n_tok, wait):
    pitch = buf.shape[0] // n_tok

    def body(i, c):
        for k in range(DMA_UNROLL):
            r = i * DMA_UNROLL + k
            cp = _token_copy(h_hbm, src_ref[r], buf, r, sem, pitch)
            cp.wait() if wait else cp.start(priority=k % 2)
        return c

    lax.fori_loop(0, n_tok // DMA_UNROLL, body, 0)


def _tokens_in_flight_inline(h_hbm, src_ref, buf, sem, n_tok, wait):
    pitch = buf.shape[0] // n_tok
    for r in range(n_tok):
        cp = _token_copy(h_hbm, src_ref[r], buf, r, sem, pitch)
        cp.wait() if wait else cp.start(priority=r % 2)


def _expert_kernel(te_ref, nv_ref, src_ref, nxt_ref, h_hbm, wgu_ref, bgu_ref, wdn_ref, bdn_ref, ys_ref,
                   xbuf0, xbuf1, sem):
    t = pl.program_id(0)
    nv = nv_ref[0]
    tm = MOE_TILE

    @pl.when(t == 0)
    def _():
        _tokens_in_flight(h_hbm, src_ref, xbuf0, sem.at[0], tm, False)

    for par, (cur, nxt) in enumerate(((xbuf0, xbuf1), (xbuf1, xbuf0))):
        @pl.when((t < nv) & (t % 2 == par))
        def _(par=par, cur=cur, nxt=nxt):
            _tokens_in_flight_inline(h_hbm, nxt_ref, nxt, sem.at[1 - par], tm, False)
            _tokens_in_flight_inline(h_hbm, src_ref, cur, sem.at[par], tm, True)
            f = wdn_ref.shape[2]
            gu = _dot(_tm_load(cur, tm).astype(BF16), wgu_ref[0, 0]) + bgu_ref[0, 0]
            gate = jnp.minimum(gu[:, :f], SWIGLU_LIMIT)
            up = jnp.clip(gu[:, f:], -SWIGLU_LIMIT, SWIGLU_LIMIT)
            act = (up + 1.0) * gate * _sigmoid(SWIGLU_ALPHA * gate)
            _tm_store(ys_ref, _dot(act.astype(BF16), wdn_ref[0, 0]) + bdn_ref[0, 0])

        @pl.when((t == nv) & (t % 2 == par))
        def _(par=par, cur=cur):
            _tokens_in_flight(h_hbm, src_ref, cur, sem.at[par], tm, True)

    @pl.when(t >= nv)
    def _():
        ys_ref[...] = jnp.zeros_like(ys_ref)


def _experts(h_tm, src, tile_expert, n_valid, layer, wgu_bf, bgu, wdn_bf, bdn):
    p_max = src.shape[0]
    _, ne, d, f2 = wgu_bf.shape
    f = f2 // 2
    sub = _tm_pitch(d)
    tm = MOE_TILE
    n_tiles = p_max // tm
    wmap = lambda t, te, nv: (layer, te[t], 0, 0)
    return pl.pallas_call(
        _expert_kernel,
        grid_spec=pltpu.PrefetchScalarGridSpec(
            num_scalar_prefetch=2,
            grid=(n_tiles,),
            in_specs=[pl.BlockSpec((tm,), lambda t, te, nv: (t,), memory_space=pltpu.SMEM),
                      pl.BlockSpec((tm,), lambda t, te, nv: (jnp.minimum(t + 1, n_tiles - 1),), memory_space=pltpu.SMEM),
                      pl.BlockSpec(memory_space=pl.ANY),
                      pl.BlockSpec((1, 1, d, f2), wmap),
                      pl.BlockSpec((1, 1, 1, f2), wmap),
                      pl.BlockSpec((1, 1, f, d), wmap),
                      pl.BlockSpec((1, 1, 1, d), wmap)],
            out_specs=pl.BlockSpec((tm * sub, LANES), lambda t, te, nv: (t, 0)),
            scratch_shapes=[pltpu.VMEM((tm * sub, LANES), F32), pltpu.VMEM((tm * sub, LANES), F32),
                            pltpu.SemaphoreType.DMA((2,))]),
        out_shape=jax.ShapeDtypeStruct((p_max * sub, LANES), F32),
        compiler_params=_params(("arbitrary",)),
        name="moe_experts",
    )(tile_expert, n_valid, src, src, h_tm, wgu_bf, bgu[:, :, None, :], wdn_bf, bdn[:, :, None, :])


def _combine_kernel(dest_ref, nxt_ref, x_ref, w_ref, mod_ref, fg_ref, ys_hbm, o_ref, buf0, buf1, sem, *, final):
    tm, d = x_ref.shape
    sub = _tm_pitch(d)
    t = pl.program_id(0)
    last = pl.num_programs(0) - 1

    def copy(idx_ref, buf, s, q):
        return _token_copy(ys_hbm, idx_ref[q], buf.at[q % TOP_K], q // TOP_K, sem.at[s], sub)

    def in_flight_loop(idx_ref, buf, s, wait):
        def body(i, c):
            for q in range(DMA_UNROLL):
                cp = copy(idx_ref, buf, s, i * DMA_UNROLL + q)
                cp.wait() if wait else cp.start(priority=q % 2)
            return c

        lax.fori_loop(0, tm * TOP_K // DMA_UNROLL, body, 0)

    @pl.when(t == 0)
    def _():
        in_flight_loop(dest_ref, buf0, 0, False)

    for par, (cur, nxt) in enumerate(((buf0, buf1), (buf1, buf0))):
        @pl.when(t % 2 == par)
        def _(par=par, cur=cur, nxt=nxt):
            for q in range(tm * TOP_K):
                copy(nxt_ref, nxt, 1 - par, q).start(priority=q % 2)
            for q in range(tm * TOP_K):
                copy(dest_ref, cur, par, q).wait()
            w = w_ref[...]
            pieces = []
            for k in range(sub - 1):
                piece = w[:, 0:1] * cur.at[0][pl.ds(k, tm, stride=sub), :]
                for j in range(1, TOP_K):
                    piece = piece + w[:, j:j + 1] * cur.at[j][pl.ds(k, tm, stride=sub), :]
                pieces.append(piece)
            out = x_ref[...] + mod_ref[0][5:6] * jnp.concatenate(pieces, axis=1)
            if final:
                ms = jnp.mean(out * out, axis=-1, keepdims=True)
                out = out * lax.rsqrt(ms + EPS) * fg_ref[...]
            o_ref[...] = out

        @pl.when((t == last) & (t % 2 == par))
        def _(par=par, nxt=nxt):
            in_flight_loop(nxt_ref, nxt, 1 - par, True)


def _combine(dest_flat, x, wts, mod, final_g, ys, seq, n_rows, final):
    d = x.shape[1]
    tm = COMBINE_ROWS
    n_steps = n_rows // tm
    row = lambda i: (i, 0)
    return pl.pallas_call(
        functools.partial(_combine_kernel, final=final),
        grid=(n_steps,),
        in_specs=[pl.BlockSpec((tm * TOP_K,), lambda i: (i,), memory_space=pltpu.SMEM),
                  pl.BlockSpec((tm * TOP_K,), lambda i: (jnp.minimum(i + 1, n_steps - 1),), memory_space=pltpu.SMEM),
                  pl.BlockSpec((tm, d), row),
                  pl.BlockSpec((tm, LANES), row),
                  pl.BlockSpec((1, 8, d), lambda i: (jnp.minimum((i * tm) // seq, 2), 0, 0)),
                  pl.BlockSpec((1, d), lambda i: (0, 0)),
                  pl.BlockSpec(memory_space=pl.ANY)],
        out_specs=pl.BlockSpec((tm, d), row),
        out_shape=jax.ShapeDtypeStruct((n_rows, d), F32),
        scratch_shapes=[pltpu.VMEM((TOP_K, tm * _tm_pitch(d), LANES), F32),
                        pltpu.VMEM((TOP_K, tm * _tm_pitch(d), LANES), F32), pltpu.SemaphoreType.DMA((2,))],
        compiler_params=_params(("arbitrary",)),
        name="moe_combine",
    )(dest_flat, dest_flat, x, wts, mod, final_g, ys)


def _moe(h2_tm, logits, x, mod, final_g, layer, wgu_bf, bgu, wdn_bf, bdn, seq, final):
    n = logits.shape[0]
    tm = MOE_TILE
    idx, wts = _topk(logits)
    rank, counts = _rank(idx)
    cnt = counts[0, :N_EXPERTS].astype(jnp.int32)
    tiles_e = (cnt + tm - 1) // tm
    tile_end = jnp.cumsum(tiles_e)
    offs = (tile_end - tiles_e) * tm
    idx4, rank4 = idx[:, :TOP_K], rank[:, :TOP_K]
    dest = (offs[idx4] + rank4).reshape(-1)
    p_max = (n * TOP_K // tm + N_EXPERTS) * tm
    n_tiles = p_max // tm
    n_valid = tile_end[-1:]
    tile_id = jnp.minimum(jnp.arange(n_tiles, dtype=jnp.int32), n_valid[0] - 1)
    tile_expert = jnp.sum((tile_end[None, :] <= tile_id[:, None]).astype(jnp.int32), axis=1)
    tile_expert = jnp.minimum(tile_expert, N_EXPERTS - 1)
    pitch = _tm_pitch(x.shape[1])
    tok_row = jnp.repeat(jnp.arange(n, dtype=jnp.int32) * pitch, TOP_K)
    src_row = jnp.zeros((p_max,), jnp.int32).at[dest].set(tok_row)
    ys = _experts(h2_tm, src_row, tile_expert, n_valid.astype(jnp.int32), layer, wgu_bf, bgu, wdn_bf, bdn)
    return _combine(dest * pitch, x, wts, mod, final_g, ys, seq, n, final)


def _hy_in_kernel(x_ref, mod_ref, ng_ref, w0, w1, w2, bin_ref, sw_ref, sb_ref, u_o, x0_o, h_scr, *, n_lat_tiles, ctx_len):
    i = pl.program_id(0)

    @pl.when(pl.program_id(1) == 0)
    def _():
        m = mod_ref[0]
        h_scr[...] = _rms_mod(x_ref[...], ng_ref[...], m[0:1], m[1:2]).astype(BF16)

    h = h_scr[...]
    tm, tn = u_o.shape
    row = lax.broadcasted_iota(jnp.int32, (tm, tn), 0)
    seg = jnp.where(i >= n_lat_tiles, ctx_len, GRID_W)
    pos = row & (seg - 1)
    first = pos == 0
    last = pos == seg - 1
    bin_ = bin_ref[...]
    sw = sw_ref[...]
    sb = sb_ref[...]
    outs = []
    for s, w in enumerate((w0, w1, w2)):
        z = _dot(h, w[...]) + bin_[s:s + 1]
        zp = jnp.where(first, 0.0, pltpu.roll(z, 1, axis=0))
        zn = jnp.where(last, 0.0, pltpu.roll(z, tm - 1, axis=0))
        outs.append(sw[0, s:s + 1] * zp + sw[1, s:s + 1] * z + sw[2, s:s + 1] * zn + sb[s:s + 1])
    x0_o[...] = outs[0]
    u_o[...] = outs[2] * outs[1]


def _hy_in(x, mod, ng, w_bf, b_in, short_w, short_b, seq, ctx_len, n_rows):
    d = x.shape[1]
    tm, tn = 512, 512
    nj = d // tn
    wspec = lambda s: pl.BlockSpec((d, tn), lambda i, j, s=s: (0, s * nj + j))
    ospec = pl.BlockSpec((tm, tn), lambda i, j: (i, j))
    osh = jax.ShapeDtypeStruct((n_rows, d), F32)
    return pl.pallas_call(
        functools.partial(_hy_in_kernel, n_lat_tiles=2 * seq // tm, ctx_len=ctx_len),
        grid=(n_rows // tm, nj),
        in_specs=[pl.BlockSpec((tm, d), lambda i, j: (i, 0)),
                  pl.BlockSpec((1, 8, d), lambda i, j: (jnp.minimum((i * tm) // seq, 2), 0, 0)),
                  pl.BlockSpec((1, d), lambda i, j: (0, 0)),
                  wspec(0), wspec(1), wspec(2),
                  pl.BlockSpec((3, tn), lambda i, j: (0, j)),
                  pl.BlockSpec((3, 3, tn), lambda i, j: (0, 0, j)),
                  pl.BlockSpec((3, tn), lambda i, j: (0, j))],
        out_specs=[ospec, ospec],
        out_shape=[osh, osh],
        scratch_shapes=[pltpu.VMEM((tm, d), BF16)],
        compiler_params=_params(("arbitrary", "arbitrary")),
        name="hyena_in_proj",
    )(x, mod, ng, w_bf, w_bf, w_bf, b_in.reshape(3, d), short_w.reshape(3, 3, d), short_b.reshape(3, d))


def _filter_kernel(z_ref, w1, b1, fr, w2, b2, w3, b3, w4, dl_ref, f_o, nrm_o, *, length):
    i = pl.program_id(0)
    z = z_ref[...]
    freq = fr[...]
    a = jnp.sin(freq * (_dot(z, w1[...]) + b1[...]))
    a = jnp.sin(freq * (_dot(a, w2[...]) + b2[...]))
    a = jnp.sin(freq * (_dot(a, w3[...]) + b3[...]))
    h = _dot(a, w4[...]) * jnp.exp(-z[:, 0:1] * dl_ref[...])
    row = lax.broadcasted_iota(jnp.int32, h.shape, 0) + i * h.shape[0]
    h = jnp.where(row == length, 0.0, h)
    f_o[...] = h

    @pl.when(i == 0)
    def _():
        nrm_o[...] = jnp.zeros_like(nrm_o)

    nrm_o[...] = nrm_o[...] + jnp.sum(jnp.abs(h), axis=0, keepdims=True)


def _pos_features(length):
    t = (np.arange(length, dtype=np.float64) / length)[:, None]
    bands = np.linspace(1e-4, HY_BANDS - 1, HY_BANDS)[None, :]
    ang = 2.0 * math.pi * t * bands
    return np.concatenate([t, np.cos(ang), -np.sin(ang)], axis=-1).astype(np.float32)


def _decay_rates(d):
    return np.abs(np.linspace(math.log(HY_DECAY_PCT_MIN) / HY_DECAY_TARGET,
                              math.log(HY_DECAY_PCT_MAX) / HY_DECAY_TARGET, d)).astype(np.float32)[None, :]


def _hyena_filter(length, w1, b1, freq, w2, b2, w3, b3, w4):
    d = w4.shape[1] // 2
    tr = min(length, 512)
    nf = length // tr
    feats = _pos_features(length)
    feats = np.concatenate([feats, feats[:1], feats[:0:-1]], axis=0)
    z = jnp.asarray(np.pad(feats, ((0, 0), (0, LANES - feats.shape[1]))))
    const = lambda i: (0, 0)
    full = lambda a: pl.BlockSpec(a.shape, const)
    mat = lambda a: jnp.pad(a, ((0, LANES - a.shape[0]), (0, LANES - a.shape[1])))
    vec = lambda a: jnp.pad(a, (0, LANES - a.shape[0])).reshape(1, LANES)
    w4p = jnp.pad(w4, ((0, LANES - w4.shape[0]), (0, 0)))
    args = [z, mat(w1), vec(b1), vec(freq), mat(w2), vec(b2), mat(w3), vec(b3), w4p, jnp.asarray(_decay_rates(d))]
    in_specs = [pl.BlockSpec((tr, LANES), lambda i: (i, 0))] + [full(a) for a in args[1:]]
    in_specs[8] = pl.BlockSpec((LANES, d), lambda i: (0, i // nf))
    return pl.pallas_call(
        functools.partial(_filter_kernel, length=length),
        grid=(2 * nf,),
        in_specs=in_specs,
        out_specs=[pl.BlockSpec((tr, d), lambda i: (i, 0)), pl.BlockSpec((1, d), const)],
        out_shape=[jax.ShapeDtypeStruct((2 * length, d), F32), jax.ShapeDtypeStruct((1, d), F32)],
        compiler_params=_params(("arbitrary",)),
        name="hyena_filter",
    )(*args)


def _dft_mats(m, k, n):
    r = np.arange(m, dtype=np.int64)[:, None]
    c = np.arange(k, dtype=np.int64)[None, :]
    ang = 2.0 * math.pi * ((r * c) % n).astype(np.float64) / n
    return np.cos(ang).astype(np.float32), np.sin(ang).astype(np.float32)


def _cmul_mat(c, s, xr, xi, conj):
    xr = xr.astype(BF16)
    cr, sr = _dot(c, xr), _dot(s, xr)
    if xi is None:
        return cr, (sr if conj else -sr)
    xi = xi.astype(BF16)
    ci, si = _dot(c, xi), _dot(s, xi)
    if conj:
        return cr - si, ci + sr
    return cr + si, ci - sr


def _slab_fwd_kernel(*refs, has_imag, has_filter):
    refs = list(refs)
    fc, fs, xr_ref = refs[:3]
    xi_ref = refs[3] if has_imag else None
    rest = refs[3 + has_imag:]
    if has_filter:
        gr_ref, gi_ref, gn_ref, or_ref, oi_ref = rest
    else:
        or_ref, oi_ref = rest
    yr, yi = _cmul_mat(fc[...].astype(BF16), fs[...].astype(BF16), xr_ref[...],
                       xi_ref[...] if has_imag else None, False)
    if has_filter:
        gn = 1.0 / gn_ref[...]
        gr, gi = gr_ref[...] * gn, gi_ref[...] * gn
        yr, yi = yr * gr - yi * gi, yr * gi + yi * gr
    or_ref[...] = yr
    oi_ref[...] = yi


def _slab_fwd(x2, k_in, parts, n_out, n, spectrum=None):
    cols = x2.shape[1]
    tc = min(cols, 2048 if n_out <= LANES else 512)
    fc, fs = _dft_mats(n_out, k_in, n)
    const = lambda c: (0, 0)
    in_specs = [pl.BlockSpec((n_out, k_in), const), pl.BlockSpec((n_out, k_in), const)]
    args = [jnp.asarray(fc), jnp.asarray(fs)]
    for part in parts:
        if part is not None:
            in_specs.append(pl.BlockSpec((k_in, tc), lambda c, part=part: (part, c)))
            args.append(x2)
    ospec = pl.BlockSpec((n_out, tc), lambda c: (0, c))
    if spectrum is not None:
        in_specs += [ospec, ospec, pl.BlockSpec((1, tc), lambda c: (0, c))]
        args += list(spectrum)
    osh = jax.ShapeDtypeStruct((n_out, cols), F32)
    return pl.pallas_call(
        functools.partial(_slab_fwd_kernel, has_imag=parts[1] is not None, has_filter=spectrum is not None),
        grid=(cols // tc,),
        in_specs=in_specs,
        out_specs=[ospec, ospec],
        out_shape=[osh, osh],
        compiler_params=_params(("arbitrary",)),
        name="dft_slab_fwd",
    )(*args)


def _mid_kernel(*refs, has_filter):
    if has_filter:
        fc, fs, twc_ref, tws_ref, ar_ref, ai_ref, gr_ref, gi_ref, gn_ref, or_ref, oi_ref = refs
    else:
        fc, fs, twc_ref, tws_ref, ar_ref, ai_ref, or_ref, oi_ref = refs
    td = ar_ref.shape[2]
    c, s = fc[...].astype(BF16), fs[...].astype(BF16)
    twc = jnp.concatenate([twc_ref[0]] * (td // LANES), axis=1)
    tws = jnp.concatenate([tws_ref[0]] * (td // LANES), axis=1)
    ar, ai = ar_ref[0], ai_ref[0]
    pr, pi_ = ar * twc + ai * tws, ai * twc - ar * tws
    br, bi = _cmul_mat(c, s, pr, pi_, False)
    if not has_filter:
        or_ref[0] = br
        oi_ref[0] = bi
        return
    gn = 1.0 / gn_ref[...]
    gr, gi = gr_ref[0] * gn, gi_ref[0] * gn
    yr, yi = br * gr - bi * gi, br * gi + bi * gr
    qr, qi = _cmul_mat(c, s, yr, yi, True)
    or_ref[0] = qr * twc - qi * tws
    oi_ref[0] = qr * tws + qi * twc


def _mid_stage(a_re, a_im, n, spectrum=None):
    n1, _, d = a_re.shape
    td = 2048
    fc, fs = _dft_mats(LANES, LANES, LANES)
    k1 = lax.broadcasted_iota(jnp.int32, (n1, LANES, LANES), 0)
    m2 = lax.broadcasted_iota(jnp.int32, (n1, LANES, LANES), 1)
    ang = ((k1 * m2) % n).astype(F32) * (2.0 * math.pi / n)
    twc, tws = jnp.cos(ang), jnp.sin(ang)
    const = lambda s, j: (0, 0)
    slab = pl.BlockSpec((1, LANES, td), lambda s, j: (s, 0, j))
    tw = pl.BlockSpec((1, LANES, LANES), lambda s, j: (s, 0, 0))
    in_specs = [pl.BlockSpec((LANES, LANES), const), pl.BlockSpec((LANES, LANES), const), tw, tw, slab, slab]
    args = [jnp.asarray(fc), jnp.asarray(fs), twc, tws, a_re, a_im]
    if spectrum is not None:
        in_specs += [slab, slab, pl.BlockSpec((1, td), lambda s, j: (0, j))]
        args += list(spectrum)
    osh = jax.ShapeDtypeStruct((n1, LANES, d), F32)
    return pl.pallas_call(
        functools.partial(_mid_kernel, has_filter=spectrum is not None),
        grid=(n1, d // td),
        in_specs=in_specs,
        out_specs=[slab, slab],
        out_shape=[osh, osh],
        compiler_params=_params(("arbitrary", "arbitrary")),
        name="dft_mid",
    )(*args)


def _slab_inv_kernel(fc, fs, pr_ref, pi_ref, u0_ref, u1_ref, a0_ref, a1_ref, skip_ref, o_ref, *, scale):
    yr, yi = _cmul_mat(fc[...].astype(BF16), fs[...].astype(BF16), pr_ref[...], pi_ref[...], True)
    skip = skip_ref[...]
    o_ref[0] = ((yr * scale + u0_ref[...] * skip) * a0_ref[...]).astype(o_ref.dtype)
    o_ref[1] = ((yi * scale + u1_ref[...] * skip) * a1_ref[...]).astype(o_ref.dtype)


def _slab_inv(p_re, p_im, u2, x02, k_out, parts, skip, n, scale):
    n_in, cols = p_re.shape
    d = skip.shape[1]
    tc = min(cols, d if n_in <= LANES else 512)
    fc, fs = _dft_mats(k_out, n_in, n)
    const = lambda c: (0, 0)
    pspec = pl.BlockSpec((n_in, tc), lambda c: (0, c))
    uspec = lambda part: pl.BlockSpec((k_out, tc), lambda c, part=part: (part, c))
    return pl.pallas_call(
        functools.partial(_slab_inv_kernel, scale=scale),
        grid=(cols // tc,),
        in_specs=[pl.BlockSpec((k_out, n_in), const), pl.BlockSpec((k_out, n_in), const),
                  pspec, pspec, uspec(parts[0]), uspec(parts[1]), uspec(parts[0]), uspec(parts[1]),
                  pl.BlockSpec((1, tc), lambda c: (0, c % (d // tc)))],
        out_specs=pl.BlockSpec((2, k_out, tc), lambda c: (0, 0, c)),
        out_shape=jax.ShapeDtypeStruct((2, k_out, cols), BF16),
        compiler_params=_params(("arbitrary",)),
        name="dft_slab_inv",
    )(jnp.asarray(fc), jnp.asarray(fs), p_re, p_im, u2, u2, x02, x02, skip)


def _long_conv(u, x0, row0, length, filt, nrm, skip):
    d = u.shape[1]
    n = 2 * length
    if n <= 4 * LANES:
        parts = (row0 // length, row0 // length + 1)
        g_re, g_im = _slab_fwd(filt, n, (0, None), n, n)
        y_re, y_im = _slab_fwd(u, length, parts, n, n, spectrum=(g_re, g_im, nrm))
        y = _slab_inv(y_re, y_im, u, x0, length, parts, skip, n, 1.0 / n)
        return y.reshape(2 * length, d)
    n1 = n // LANES
    k1 = n1 // 2
    parts = (row0 // length, row0 // length + 1)
    slabs = lambda a: a.reshape(a.shape[0] // LANES, LANES, d)
    f_re, f_im = _slab3_fwd(slabs(filt), n1, (0, None), n1)
    g_re, g_im = _mid_stage(f_re, f_im, n)
    a_re, a_im = _slab3_fwd(slabs(u), k1, parts, n1)
    p_re, p_im = _mid_stage(a_re, a_im, n, spectrum=(g_re, g_im, nrm))
    y = _slab3_inv(p_re, p_im, slabs(u), slabs(x0), k1, parts, skip, 1.0 / n)
    return y.reshape(2 * length, d)


SLAB_ROWS = 16
SLAB_COLS = 256


def _to_lanes(x3):
    xs = jnp.swapaxes(x3, 0, 1)
    return jnp.concatenate([xs[r] for r in range(xs.shape[0])], axis=1)


def _from_lanes(y2, r):
    c = y2.shape[1] // r
    return jnp.swapaxes(jnp.stack([y2[:, i * c:(i + 1) * c] for i in range(r)], axis=0), 0, 1)


def _slab3_fwd_kernel(*refs, has_imag):
    fc, fs, xr_ref = refs[:3]
    xi_ref = refs[3] if has_imag else None
    or_ref, oi_ref = refs[3 + has_imag:]
    yr, yi = _cmul_mat(fc[...].astype(BF16), fs[...].astype(BF16), _to_lanes(xr_ref[...]),
                       _to_lanes(xi_ref[...]) if has_imag else None, False)
    or_ref[...] = _from_lanes(yr, or_ref.shape[1])
    oi_ref[...] = _from_lanes(yi, oi_ref.shape[1])


def _slab3_fwd(x3, k_in, parts, n1):
    _, rows, d = x3.shape
    fc, fs = _dft_mats(n1, k_in, n1)
    const = lambda g, j: (0, 0)
    in_specs = [pl.BlockSpec((n1, k_in), const), pl.BlockSpec((n1, k_in), const)]
    args = [jnp.asarray(fc), jnp.asarray(fs)]
    for part in parts:
        if part is not None:
            in_specs.append(pl.BlockSpec((k_in, SLAB_ROWS, SLAB_COLS), lambda g, j, part=part: (part, g, j)))
            args.append(x3)
    ospec = pl.BlockSpec((n1, SLAB_ROWS, SLAB_COLS), lambda g, j: (0, g, j))
    osh = jax.ShapeDtypeStruct((n1, rows, d), F32)
    return pl.pallas_call(
        functools.partial(_slab3_fwd_kernel, has_imag=parts[1] is not None),
        grid=(rows // SLAB_ROWS, d // SLAB_COLS),
        in_specs=in_specs,
        out_specs=[ospec, ospec],
        out_shape=[osh, osh],
        compiler_params=_params(("arbitrary", "arbitrary")),
        name="dft_slab3_fwd",
    )(*args)


def _slab3_inv_kernel(fc, fs, pr_ref, pi_ref, u0_ref, u1_ref, a0_ref, a1_ref, skip_ref, o_ref, *, scale):
    yr, yi = _cmul_mat(fc[...].astype(BF16), fs[...].astype(BF16), _to_lanes(pr_ref[...]), _to_lanes(pi_ref[...]), True)
    r = u0_ref.shape[1]
    skip = skip_ref[...]
    o_ref[0] = ((_from_lanes(yr, r) * scale + u0_ref[...] * skip) * a0_ref[...]).astype(o_ref.dtype)
    o_ref[1] = ((_from_lanes(yi, r) * scale + u1_ref[...] * skip) * a1_ref[...]).astype(o_ref.dtype)


def _slab3_inv(p_re, p_im, u3, x03, k_out, parts, skip, scale):
    n1, rows, d = p_re.shape
    fc, fs = _dft_mats(k_out, n1, n1)
    const = lambda g, j: (0, 0)
    pspec = pl.BlockSpec((n1, SLAB_ROWS, SLAB_COLS), lambda g, j: (0, g, j))
    uspec = lambda part: pl.BlockSpec((k_out, SLAB_ROWS, SLAB_COLS), lambda g, j, part=part: (part, g, j))
    return pl.pallas_call(
        functools.partial(_slab3_inv_kernel, scale=scale),
        grid=(rows // SLAB_ROWS, d // SLAB_COLS),
        in_specs=[pl.BlockSpec((k_out, n1), const), pl.BlockSpec((k_out, n1), const),
                  pspec, pspec, uspec(parts[0]), uspec(parts[1]), uspec(parts[0]), uspec(parts[1]),
                  pl.BlockSpec((1, SLAB_COLS), lambda g, j: (0, j))],
        out_specs=pl.BlockSpec((2, k_out, SLAB_ROWS, SLAB_COLS), lambda g, j: (0, 0, g, j)),
        out_shape=jax.ShapeDtypeStruct((2, k_out, rows, d), BF16),
        compiler_params=_params(("arbitrary", "arbitrary")),
        name="dft_slab3_inv",
    )(jnp.asarray(fc), jnp.asarray(fs), p_re, p_im, u3, u3, x03, x03, skip)


def kernel(x, c, ctx, c_ctx, ada_w, ada_b, norm_mix_g, norm_ffn_g, final_norm_g, hg_w_in, hg_lb, hg_onorm_g, hg_w_out, hy_w_in, hy_b_in, hy_short_w, hy_short_b, hy_f_w1, hy_f_b1, hy_f_freq, hy_f_w2, hy_f_b2, hy_f_w3, hy_f_b3, hy_f_w4, hy_skip, hy_w_out, hy_b_out, moe_w_router, moe_b_router, moe_w_gu, moe_b_gu, moe_w_down, moe_b_down):
    bsz, seq, d = x.shape
    ctx_len = ctx.shape[1]
    depth = ada_w.shape[0]
    n_lat = bsz * seq
    n_all = n_lat + bsz * ctx_len
    assert bsz == 2 and seq % MOE_TILE == 0 and ctx_len == GLA_ROWS

    cond8 = jnp.zeros((8, d), F32).at[0:2].set(c).at[2].set(c_ctx)
    mods = _ada_all(cond8, ada_w, ada_b).reshape(depth, 8, N_MOD, d)[:, :3]
    mods = jnp.pad(mods, ((0, 0), (0, 0), (0, 8 - N_MOD), (0, 0)))

    lb_soft = jax.nn.softmax(hg_lb.astype(F32), axis=0)
    lower_bounds = jnp.cumsum(lb_soft, axis=0) - lb_soft[0]

    xa = jnp.concatenate([x.reshape(n_lat, d), ctx.reshape(bsz * ctx_len, d)], axis=0)
    zero_bias = jnp.zeros((1, d), F32)
    final_g = final_norm_g.reshape(1, d)
    wgu_bf, wdn_bf = moe_w_gu.astype(BF16), moe_w_down.astype(BF16)
    ctx_needed = [any(l % 2 == 0 for l in range(layer + 1, depth)) for layer in range(depth)]

    for layer in range(depth):
        j = layer // 2
        is_hgrn = layer % 2 == 0
        keep_ctx = ctx_needed[layer]
        n_out = n_all if keep_ctx else n_lat
        mod = mods[layer]
        ng1 = norm_mix_g[layer].reshape(1, d)
        ng2 = norm_ffn_g[layer].reshape(1, d)
        if is_hgrn:
            q, kf, lff, kb, lfb, v, g = _hg_in(xa, mod, ng1, lower_bounds[j].reshape(1, d),
                                               hg_w_in[j].astype(BF16), seq, n_all)
            o_fw = _gla_pass(q, kf, v, lff, seq, ctx_len, False)
            y = _gla_pass(q, kb, v, lfb, seq, ctx_len, True,
                          extra=(o_fw, g, hg_onorm_g[j].reshape(1, HG_DK)))
            w_out, b_out = hg_w_out[j].astype(BF16), zero_bias
        else:
            n_in = n_all if keep_ctx else n_lat
            u, x0 = _hy_in(xa, mod, ng1, hy_w_in[j].astype(BF16), hy_b_in[j], hy_short_w[j], hy_short_b[j],
                           seq, ctx_len, n_in)
            fpar = (hy_f_w1[j], hy_f_b1[j], hy_f_freq[j], hy_f_w2[j], hy_f_b2[j], hy_f_w3[j], hy_f_b3[j], hy_f_w4[j])
            skip = hy_skip[j].reshape(1, d)
            y = _long_conv(u, x0, 0, seq, *_hyena_filter(seq, *fpar), skip)
            if keep_ctx:
                yc = _long_conv(u, x0, n_lat, ctx_len, *_hyena_filter(ctx_len, *fpar), skip)
                y = jnp.concatenate([y, yc], axis=0)
            w_out, b_out = hy_w_out[j].astype(BF16), hy_b_out[j].reshape(1, d)
        wr = jnp.pad(moe_w_router[layer], ((0, 0), (0, LANES - N_EXPERTS)))
        br = jnp.pad(moe_b_router[layer], (0, LANES - N_EXPERTS)).reshape(1, LANES)
        xa, h2, logits = _out_proj(y, w_out, b_out, xa, mod, ng2, wr, br, seq, n_out)
        xa = _moe(h2, logits, xa, mod, final_g, layer, wgu_bf, moe_b_gu, wdn_bf, moe_b_down, seq, layer == depth - 1)
    return xa.reshape(bsz, seq, d)
```

```python
import functools
import math

import numpy as np
import jax
import jax.numpy as jnp
from jax import lax
from jax.experimental import pallas as pl
from jax.experimental.pallas import tpu as pltpu

F32 = jnp.float32
BF16 = jnp.bfloat16

EPS = 1e-6
LOG2_E = 1.0 / math.log(2.0)
N_MOD = 6
LANES = 128
GRID_W = 64
HG_DK = 128
GLA_CHUNK = 128
GLA_ROWS = 256
GLA_HEADS = 4
HY_BANDS = 16
HY_DECAY_PCT_MIN = 0.3
HY_DECAY_PCT_MAX = 1.5
HY_DECAY_TARGET = 1e-2
N_EXPERTS = 32
TOP_K = 4
SWIGLU_ALPHA = 1.702
SWIGLU_LIMIT = 7.0
MOE_TILE = 512
VMEM_LIMIT = 56 * 1024 * 1024


def _params(sem, vmem=VMEM_LIMIT):
    return pltpu.CompilerParams(dimension_semantics=sem, vmem_limit_bytes=vmem)


def _sigmoid(x):
    return 1.0 / (1.0 + jnp.exp(-x))


def _dot(a, b):
    return jnp.dot(a, b, preferred_element_type=F32)


def _dot_nt(a, b):
    return lax.dot_general(a, b, (((1,), (1,)), ((), ())), preferred_element_type=F32)


def _split(a):
    hi = a.astype(BF16)
    lo = (a - hi.astype(F32)).astype(BF16)
    return hi, lo


def _dot3(a, b):
    (ah, al), (bh, bl) = a, b
    return _dot(ah, bh) + _dot(ah, bl) + _dot(al, bh)


def _rms_mod(x, g, shift, scale):
    ms = jnp.mean(x * x, axis=-1, keepdims=True)
    return x * lax.rsqrt(ms + EPS) * g * (1.0 + scale) + shift


def _ada_kernel(c_ref, w_ref, b_ref, o_ref):
    c = c_ref[...]
    o_ref[0] = _dot(c * _sigmoid(c), w_ref[0]) + b_ref[0]


def _ada_all(cond8, ada_w, ada_b):
    depth, d, nd = ada_w.shape
    tn = 1024
    return pl.pallas_call(
        _ada_kernel,
        grid=(depth, nd // tn),
        in_specs=[pl.BlockSpec((8, d), lambda l, j: (0, 0)),
                  pl.BlockSpec((1, d, tn), lambda l, j: (l, 0, j)),
                  pl.BlockSpec((1, 1, tn), lambda l, j: (l, 0, j))],
        out_specs=pl.BlockSpec((1, 8, tn), lambda l, j: (l, 0, j)),
        out_shape=jax.ShapeDtypeStruct((depth, 8, nd), F32),
        compiler_params=_params(("arbitrary", "arbitrary")),
        name="ada_mod",
    )(cond8, ada_w, ada_b.reshape(depth, 1, nd))


def _hg_in_kernel(x_ref, mod_ref, ng_ref, lb_ref, wq, wf, wb, wi, wg,
                  q_o, kf_o, lff_o, kb_o, lfb_o, v_o, g_o, h_scr):
    @pl.when(pl.program_id(1) == 0)
    def _():
        m = mod_ref[0]
        h_scr[...] = _rms_mod(x_ref[...], ng_ref[...], m[0:1], m[1:2]).astype(BF16)

    h = h_scr[...]
    q = _dot(h, wq[...])
    q_o[...] = q * _sigmoid(q) * (HG_DK ** -0.5)
    lb = lb_ref[...]
    for w, k_o, lf_o in ((wf, kf_o, lff_o), (wb, kb_o, lfb_o)):
        f = lb + (1.0 - lb) * _sigmoid(_dot(h, w[...]))
        k_o[...] = 1.0 - f
        lf_o[...] = jnp.log(f) * LOG2_E
    v_o[...] = _dot(h, wi[...])
    g_o[...] = _dot(h, wg[...])


def _hg_in(x, mod, ng, lb, w_bf, seq, n_rows):
    d = x.shape[1]
    tm, tn = 512, 512
    nj = d // tn
    grp = lambda i, j: (jnp.minimum((i * tm) // seq, 2), 0, 0)
    wspec = lambda s: pl.BlockSpec((d, tn), lambda i, j, s=s: (0, s * nj + j))
    ospec = pl.BlockSpec((tm, tn), lambda i, j: (i, j))
    osh = jax.ShapeDtypeStruct((n_rows, d), F32)
    return pl.pallas_call(
        _hg_in_kernel,
        grid=(n_rows // tm, nj),
        in_specs=[pl.BlockSpec((tm, d), lambda i, j: (i, 0)),
                  pl.BlockSpec((1, 8, d), grp),
                  pl.BlockSpec((1, d), lambda i, j: (0, 0)),
                  pl.BlockSpec((1, tn), lambda i, j: (0, j))] + [wspec(s) for s in range(5)],
        out_specs=[ospec] * 7,
        out_shape=[osh] * 7,
        scratch_shapes=[pltpu.VMEM((tm, d), BF16)],
        compiler_params=_params(("arbitrary", "arbitrary")),
        name="hgrn_in_proj",
    )(x, mod, ng, lb, w_bf, w_bf, w_bf, w_bf, w_bf)


def _level_map(c, reverse):
    i = np.arange(c)[:, None]
    j = np.arange(c)[None, :]
    x = i ^ j
    lvl = np.where(x > 0, np.floor(np.log2(np.maximum(x, 1))), -1).astype(np.int32)
    side = (i < j) if reverse else (i > j)
    return np.where(side, lvl, -1).astype(np.int32)


def _tri_ones(c, reverse):
    i = np.arange(c)[:, None]
    m = np.arange(c)[None, :]
    return ((m >= i) if reverse else (m <= i)).astype(np.float32)


def _cumsum_rows(x, tri):
    hi = x.astype(BF16)
    r1 = x - hi.astype(F32)
    mid = r1.astype(BF16)
    lo = (r1 - mid.astype(F32)).astype(BF16)
    return _dot(tri, hi) + _dot(tri, mid) + _dot(tri, lo)


def _neg_abs(x):
    return pltpu.bitcast(pltpu.bitcast(x, jnp.uint32) | jnp.uint32(0x80000000), F32)


def _gla_chunk(q, k, v, lf, st, lvl, tri, row, reverse):
    c, wd = q.shape
    nh = wd // HG_DK
    b = _cumsum_rows(lf, tri)
    lf_up = pltpu.roll(lf, c - 1, axis=0)
    lf_dn = pltpu.roll(lf, 1, axis=0)
    att = [jnp.zeros((c, c), F32) for _ in range(nh)]
    p = 0
    while (1 << p) < c:
        s = 1 << p
        is_query = ((row & s) == 0) if reverse else ((row & s) != 0)
        if s == 1:
            a = jnp.where(is_query, lf, 0.0)
        elif s == 2:
            m = row & 3
            if reverse:
                a = jnp.where(m == 0, lf + lf_up, jnp.where(m == 1, lf, jnp.where(m == 2, 0.0, lf_dn)))
            else:
                a = jnp.where(m == 0, lf_up, jnp.where(m == 1, 0.0, jnp.where(m == 2, lf, lf + lf_dn)))
        else:
            nb = c // (2 * s)
            r = s if reverse else s - 1
            b3 = b.reshape(nb, 2 * s, wd)
            br = jnp.broadcast_to(b3[:, r:r + 1, :], (nb, 2 * s, wd)).reshape(c, wd)
            a = _neg_abs(b - br)
        z = (jnp.where(is_query, q, k) * jnp.exp2(a)).astype(BF16)
        for h in range(nh):
            zh = z[:, h * HG_DK:(h + 1) * HG_DK]
            att[h] = jnp.where(lvl == p, _dot_nt(zh, zh), att[h])
        p += 1
    tot = b[0:1] if reverse else b[c - 1:c]
    qe = (q * jnp.exp2(b)).astype(BF16)
    kd = (k * jnp.exp2(tot - b)).astype(BF16)
    dec = jnp.exp2(tot)
    outs, new_st = [], []
    for h in range(nh):
        cs = slice(h * HG_DK, (h + 1) * HG_DK)
        vh = v[:, cs]
        dsum = jnp.sum(q[:, cs] * k[:, cs], axis=-1, keepdims=True)
        o = _dot(att[h].astype(BF16), vh.astype(BF16)) + dsum * vh
        o = o + _dot_nt(qe[:, cs], st[h].astype(BF16))
        outs.append(o)
        new_st.append(st[h] * dec[:, cs] + _dot(vh.T.astype(BF16), kd[:, cs]))
    return jnp.concatenate(outs, axis=1), new_st


def _gla_kernel(*refs, reverse, gated):
    if gated:
        q_ref, k_ref, v_ref, lf_ref, lvl_ref, tri_ref, ofw_ref, g_ref, on_ref, y_ref, s_scr = refs
    else:
        q_ref, k_ref, v_ref, lf_ref, lvl_ref, tri_ref, y_ref, s_scr = refs
    c = GLA_CHUNK
    rows, wd = q_ref.shape
    nh = wd // HG_DK
    nchunk = rows // c

    @pl.when(pl.program_id(2) == 0)
    def _():
        s_scr[...] = jnp.zeros_like(s_scr)

    lvl = lvl_ref[...]
    tri = tri_ref[...]
    row = lax.broadcasted_iota(jnp.int32, (c, wd), 0)

    def body(i, carry):
        ci = (nchunk - 1 - i) if reverse else i
        rs = pl.ds(pl.multiple_of(ci * c, c), c)
        st = [s_scr[h] for h in range(nh)]
        o, st = _gla_chunk(q_ref[rs, :], k_ref[rs, :], v_ref[rs, :], lf_ref[rs, :], st, lvl, tri, row, reverse)
        for h in range(nh):
            s_scr[h] = st[h]
        if gated:
            o = o + ofw_ref[rs, :]
            g = g_ref[rs, :]
            on = on_ref[...]
            ys = []
            for h in range(nh):
                oh = o[:, h * HG_DK:(h + 1) * HG_DK]
                ms = jnp.mean(oh * oh, axis=-1, keepdims=True)
                ys.append(oh * lax.rsqrt(ms + EPS) * on)
            y = jnp.concatenate(ys, axis=1) * (g * _sigmoid(g))
            y_ref[rs, :] = y.astype(y_ref.dtype)
        else:
            y_ref[rs, :] = o
        return carry

    lax.fori_loop(0, nchunk, body, 0)


def _gla_pass(q, k, v, lf, seq, ctx_len, reverse, extra=None):
    n_all, d = q.shape
    rows, wd = GLA_ROWS, GLA_HEADS * HG_DK
    nlat, nctx = seq // rows, ctx_len // rows
    ctx_base = 2 * nlat

    def rmap(b, hh, t):
        tc = (nctx - 1 - t) if reverse else t
        tl = (nlat - 1 - (t - nctx)) if reverse else (t - nctx)
        return (jnp.where(t < nctx, ctx_base + b * nctx + tc, b * nlat + tl), hh)

    blk = pl.BlockSpec((rows, wd), rmap)
    lvl = jnp.asarray(_level_map(GLA_CHUNK, reverse))
    tri = jnp.asarray(_tri_ones(GLA_CHUNK, reverse), dtype=BF16)
    sq = pl.BlockSpec((GLA_CHUNK, GLA_CHUNK), lambda b, hh, t: (0, 0))
    in_specs = [blk, blk, blk, blk, sq, sq]
    args = [q, k, v, lf, lvl, tri]
    gated = extra is not None
    if gated:
        in_specs += [blk, blk, pl.BlockSpec((1, HG_DK), lambda b, hh, t: (0, 0))]
        args += list(extra)
    return pl.pallas_call(
        functools.partial(_gla_kernel, reverse=reverse, gated=gated),
        grid=(2, d // wd, nlat + nctx),
        in_specs=in_specs,
        out_specs=blk,
        out_shape=jax.ShapeDtypeStruct((n_all, d), BF16 if gated else F32),
        scratch_shapes=[pltpu.VMEM((GLA_HEADS, HG_DK, HG_DK), F32)],
        compiler_params=_params(("arbitrary", "arbitrary", "arbitrary")),
        name="gla_scan_bwd" if reverse else "gla_scan_fwd",
    )(*args)


def _tm_pitch(d):
    return d // LANES + 1


def _tm_store(ref, val):
    rows, d = val.shape
    pitch = _tm_pitch(d)
    for k in range(pitch - 1):
        ref[pl.ds(k, rows, stride=pitch), :] = val[:, k * LANES:(k + 1) * LANES]
    ref[pl.ds(pitch - 1, rows, stride=pitch), :] = jnp.zeros((rows, LANES), val.dtype)


def _tm_load(ref, rows):
    pitch = ref.shape[0] // rows
    return jnp.concatenate([ref[pl.ds(k, rows, stride=pitch), :] for k in range(pitch - 1)], axis=1)


def _out_kernel(y_ref, w_ref, b_ref, x_ref, mod_ref, ng_ref, wr_ref, br_ref, xo_ref, h2_ref, lg_ref):
    m = mod_ref[0]
    xn = x_ref[...] + m[2:3] * (_dot(y_ref[...], w_ref[...]) + b_ref[...])
    xo_ref[...] = xn
    h2 = _rms_mod(xn, ng_ref[...], m[3:4], m[4:5])
    _tm_store(h2_ref, h2)
    lg_ref[...] = _dot3(_split(h2), _split(wr_ref[...])) + br_ref[...]


def _out_proj(y, w_bf, bias, x, mod, ng, wr_pad, br_pad, seq, n_rows):
    d = x.shape[1]
    tm = 512
    row = lambda i: (i, 0)
    const = lambda i: (0, 0)
    return pl.pallas_call(
        _out_kernel,
        grid=(n_rows // tm,),
        in_specs=[pl.BlockSpec((tm, d), row),
                  pl.BlockSpec((d, d), const),
                  pl.BlockSpec((1, d), const),
                  pl.BlockSpec((tm, d), row),
                  pl.BlockSpec((1, 8, d), lambda i: (jnp.minimum((i * tm) // seq, 2), 0, 0)),
                  pl.BlockSpec((1, d), const),
                  pl.BlockSpec((d, LANES), const),
                  pl.BlockSpec((1, LANES), const)],
        out_specs=[pl.BlockSpec((tm, d), row), pl.BlockSpec((tm * _tm_pitch(d), LANES), row),
                   pl.BlockSpec((tm, LANES), row)],
        out_shape=[jax.ShapeDtypeStruct((n_rows, d), F32), jax.ShapeDtypeStruct((n_rows * _tm_pitch(d), LANES), F32),
                   jax.ShapeDtypeStruct((n_rows, LANES), F32)],
        compiler_params=_params(("arbitrary",)),
        name="out_proj",
    )(y, w_bf, bias, x, mod, ng, wr_pad, br_pad)


def _topk_kernel(lg_ref, idx_ref, w_ref):
    l = lg_ref[...]
    lane = lax.broadcasted_iota(jnp.int32, l.shape, 1).astype(F32)
    l = jnp.where(lane < N_EXPERTS, l, -jnp.inf)
    idx_out = jnp.zeros(l.shape, F32)
    vals = []
    for r in range(TOP_K):
        m = jnp.max(l, axis=-1, keepdims=True)
        sel = jnp.min(jnp.where(l == m, lane, float(LANES)), axis=-1, keepdims=True)
        idx_out = jnp.where(lane == r, sel, idx_out)
        l = jnp.where(lane == sel, -jnp.inf, l)
        vals.append(m)
    es = [jnp.exp(v - vals[0]) for v in vals]
    tot = es[0] + es[1] + es[2] + es[3]
    w_out = jnp.zeros(l.shape, F32)
    for r in range(TOP_K):
        w_out = jnp.where(lane == r, es[r] / tot, w_out)
    idx_ref[...] = idx_out.astype(jnp.int32)
    w_ref[...] = w_out


def _topk(logits):
    n = logits.shape[0]
    tm = 512
    spec = pl.BlockSpec((tm, LANES), lambda i: (i, 0))
    return pl.pallas_call(
        _topk_kernel,
        grid=(n // tm,),
        in_specs=[spec],
        out_specs=[spec, spec],
        out_shape=[jax.ShapeDtypeStruct((n, LANES), jnp.int32), jax.ShapeDtypeStruct((n, LANES), F32)],
        compiler_params=_params(("arbitrary",)),
        name="router_topk",
    )(logits)


def _rank_kernel(idx_ref, rank_ref, cnt_ref, run_scr):
    @pl.when(pl.program_id(0) == 0)
    def _():
        run_scr[...] = jnp.zeros_like(run_scr)

    idx = idx_ref[...]
    tm = idx.shape[0]
    lane = lax.broadcasted_iota(jnp.int32, idx.shape, 1)
    onehots = [jnp.where(lane == idx[:, j:j + 1], 1.0, 0.0) for j in range(TOP_K)]
    esum = onehots[0] + onehots[1] + onehots[2] + onehots[3]
    ri = lax.broadcasted_iota(jnp.int32, (tm, tm), 0)
    ci = lax.broadcasted_iota(jnp.int32, (tm, tm), 1)
    before = jnp.where(ci < ri, 1.0, 0.0).astype(BF16)
    base = run_scr[...] + _dot(before, esum.astype(BF16))
    rank = jnp.zeros(idx.shape, F32)
    for j in range(TOP_K):
        rank = jnp.where(lane == j, jnp.sum(onehots[j] * base, axis=-1, keepdims=True), rank)
    rank_ref[...] = rank.astype(jnp.int32)
    run_scr[...] = run_scr[...] + jnp.sum(esum, axis=0, keepdims=True)
    cnt_ref[...] = run_scr[...]


def _rank(idx):
    n = idx.shape[0]
    tm = 512
    spec = pl.BlockSpec((tm, LANES), lambda i: (i, 0))
    return pl.pallas_call(
        _rank_kernel,
        grid=(n // tm,),
        in_specs=[spec],
        out_specs=[spec, pl.BlockSpec((1, LANES), lambda i: (0, 0))],
        out_shape=[jax.ShapeDtypeStruct((n, LANES), jnp.int32), jax.ShapeDtypeStruct((1, LANES), F32)],
        scratch_shapes=[pltpu.VMEM((1, LANES), F32)],
        compiler_params=_params(("arbitrary",)),
        name="router_rank",
    )(idx)


COMBINE_ROWS = 256


def _token_copy(src_hbm, s_row, dst, r, sem, pitch):
    return pltpu.make_async_copy(src_hbm.at[pl.ds(s_row, pitch - 1)], dst.at[pl.ds(r * pitch, pitch - 1)], sem)


DMA_UNROLL = 16


def _tokens_in_flight(h_hbm, src_ref, buf, sem, n_tok, wait):
    pitch = buf.shape[0] // n_tok

    def body(i, c):
        for k in range(DMA_UNROLL):
            r = i * DMA_UNROLL + k
            cp = _token_copy(h_hbm, src_ref[r], buf, r, sem, pitch)
            cp.wait() if wait else cp.start(priority=k % 2)
        return c

    lax.fori_loop(0, n_tok // DMA_UNROLL, body, 0)


def _tokens_in_flight_inline(h_hbm, src_ref, buf, sem, n_tok, wait):
    pitch = buf.shape[0] // n_tok
    for r in range(n_tok):
        cp = _token_copy(h_hbm, src_ref[r], buf, r, sem, pitch)
        cp.wait() if wait else cp.start(priority=r % 2)


def _expert_kernel(te_ref, nv_ref, src_ref, nxt_ref, h_hbm, wgu_ref, bgu_ref, wdn_ref, bdn_ref, ys_ref,
                   xbuf0, xbuf1, sem):
    t = pl.program_id(0)
    nv = nv_ref[0]
    tm = MOE_TILE

    @pl.when(t == 0)
    def _():
        _tokens_in_flight(h_hbm, src_ref, xbuf0, sem.at[0], tm, False)

    for par, (cur, nxt) in enumerate(((xbuf0, xbuf1), (xbuf1, xbuf0))):
        @pl.when((t < nv) & (t % 2 == par))
        def _(par=par, cur=cur, nxt=nxt):
            _tokens_in_flight_inline(h_hbm, nxt_ref, nxt, sem.at[1 - par], tm, False)
            _tokens_in_flight_inline(h_hbm, src_ref, cur, sem.at[par], tm, True)
            f = wdn_ref.shape[2]
            gu = _dot(_tm_load(cur, tm).astype(BF16), wgu_ref[0, 0]) + bgu_ref[0, 0]
            gate = jnp.minimum(gu[:, :f], SWIGLU_LIMIT)
            up = jnp.clip(gu[:, f:], -SWIGLU_LIMIT, SWIGLU_LIMIT)
            act = (up + 1.0) * gate * _sigmoid(SWIGLU_ALPHA * gate)
            _tm_store(ys_ref, _dot(act.astype(BF16), wdn_ref[0, 0]) + bdn_ref[0, 0])

        @pl.when((t == nv) & (t % 2 == par))
        def _(par=par, cur=cur):
            _tokens_in_flight(h_hbm, src_ref, cur, sem.at[par], tm, True)

    @pl.when(t >= nv)
    def _():
        ys_ref[...] = jnp.zeros_like(ys_ref)


def _experts(h_tm, src, tile_expert, n_valid, layer, wgu_bf, bgu, wdn_bf, bdn):
    p_max = src.shape[0]
    _, ne, d, f2 = wgu_bf.shape
    f = f2 // 2
    sub = _tm_pitch(d)
    tm = MOE_TILE
    n_tiles = p_max // tm
    wmap = lambda t, te, nv: (layer, te[t], 0, 0)
    return pl.pallas_call(
        _expert_kernel,
        grid_spec=pltpu.PrefetchScalarGridSpec(
            num_scalar_prefetch=2,
            grid=(n_tiles,),
            in_specs=[pl.BlockSpec((tm,), lambda t, te, nv: (t,), memory_space=pltpu.SMEM),
                      pl.BlockSpec((tm,), lambda t, te, nv: (jnp.minimum(t + 1, n_tiles - 1),), memory_space=pltpu.SMEM),
                      pl.BlockSpec(memory_space=pl.ANY),
                      pl.BlockSpec((1, 1, d, f2), wmap),
                      pl.BlockSpec((1, 1, 1, f2), wmap),
                      pl.BlockSpec((1, 1, f, d), wmap),
                      pl.BlockSpec((1, 1, 1, d), wmap)],
            out_specs=pl.BlockSpec((tm * sub, LANES), lambda t, te, nv: (t, 0)),
            scratch_shapes=[pltpu.VMEM((tm * sub, LANES), F32), pltpu.VMEM((tm * sub, LANES), F32),
                            pltpu.SemaphoreType.DMA((2,))]),
        out_shape=jax.ShapeDtypeStruct((p_max * sub, LANES), F32),
        compiler_params=_params(("arbitrary",)),
        name="moe_experts",
    )(tile_expert, n_valid, src, src, h_tm, wgu_bf, bgu[:, :, None, :], wdn_bf, bdn[:, :, None, :])


def _combine_kernel(dest_ref, nxt_ref, x_ref, w_ref, mod_ref, fg_ref, ys_hbm, o_ref, buf0, buf1, sem, *, final):
    tm, d = x_ref.shape
    sub = _tm_pitch(d)
    t = pl.program_id(0)
    last = pl.num_programs(0) - 1

    def copy(idx_ref, buf, s, q):
        return _token_copy(ys_hbm, idx_ref[q], buf.at[q % TOP_K], q // TOP_K, sem.at[s], sub)

    def in_flight_loop(idx_ref, buf, s, wait):
        def body(i, c):
            for q in range(DMA_UNROLL):
                cp = copy(idx_ref, buf, s, i * DMA_UNROLL + q)
                cp.wait() if wait else cp.start(priority=q % 2)
            return c

        lax.fori_loop(0, tm * TOP_K // DMA_UNROLL, body, 0)

    @pl.when(t == 0)
    def _():
        in_flight_loop(dest_ref, buf0, 0, False)

    for par, (cur, nxt) in enumerate(((buf0, buf1), (buf1, buf0))):
        @pl.when(t % 2 == par)
        def _(par=par, cur=cur, nxt=nxt):
            for q in range(tm * TOP_K):
                copy(nxt_ref, nxt, 1 - par, q).start(priority=q % 2)
            for q in range(tm * TOP_K):
                copy(dest_ref, cur, par, q).wait()
            w = w_ref[...]
            pieces = []
            for k in range(sub - 1):
                piece = w[:, 0:1] * cur.at[0][pl.ds(k, tm, stride=sub), :]
                for j in range(1, TOP_K):
                    piece = piece + w[:, j:j + 1] * cur.at[j][pl.ds(k, tm, stride=sub), :]
                pieces.append(piece)
            out = x_ref[...] + mod_ref[0][5:6] * jnp.concatenate(pieces, axis=1)
            if final:
                ms = jnp.mean(out * out, axis=-1, keepdims=True)
                out = out * lax.rsqrt(ms + EPS) * fg_ref[...]
            o_ref[...] = out

        @pl.when((t == last) & (t % 2 == par))
        def _(par=par, nxt=nxt):
            in_flight_loop(nxt_ref, nxt, 1 - par, True)


def _combine(dest_flat, x, wts, mod, final_g, ys, seq, n_rows, final):
    d = x.shape[1]
    tm = COMBINE_ROWS
    n_steps = n_rows // tm
    row = lambda i: (i, 0)
    return pl.pallas_call(
        functools.partial(_combine_kernel, final=final),
        grid=(n_steps,),
        in_specs=[pl.BlockSpec((tm * TOP_K,), lambda i: (i,), memory_space=pltpu.SMEM),
                  pl.BlockSpec((tm * TOP_K,), lambda i: (jnp.minimum(i + 1, n_steps - 1),), memory_space=pltpu.SMEM),
                  pl.BlockSpec((tm, d), row),
                  pl.BlockSpec((tm, LANES), row),
                  pl.BlockSpec((1, 8, d), lambda i: (jnp.minimum((i * tm) // seq, 2), 0, 0)),
                  pl.BlockSpec((1, d), lambda i: (0, 0)),
                  pl.BlockSpec(memory_space=pl.ANY)],
        out_specs=pl.BlockSpec((tm, d), row),
        out_shape=jax.ShapeDtypeStruct((n_rows, d), F32),
        scratch_shapes=[pltpu.VMEM((TOP_K, tm * _tm_pitch(d), LANES), F32),
                        pltpu.VMEM((TOP_K, tm * _tm_pitch(d), LANES), F32), pltpu.SemaphoreType.DMA((2,))],
        compiler_params=_params(("arbitrary",)),
        name="moe_combine",
    )(dest_flat, dest_flat, x, wts, mod, final_g, ys)


def _moe(h2_tm, logits, x, mod, final_g, layer, wgu_bf, bgu, wdn_bf, bdn, seq, final):
    n = logits.shape[0]
    tm = MOE_TILE
    idx, wts = _topk(logits)
    rank, counts = _rank(idx)
    cnt = counts[0, :N_EXPERTS].astype(jnp.int32)
    tiles_e = (cnt + tm - 1) // tm
    tile_end = jnp.cumsum(tiles_e)
    offs = (tile_end - tiles_e) * tm
    idx4, rank4 = idx[:, :TOP_K], rank[:, :TOP_K]
    dest = (offs[idx4] + rank4).reshape(-1)
    p_max = (n * TOP_K // tm + N_EXPERTS) * tm
    n_tiles = p_max // tm
    n_valid = tile_end[-1:]
    tile_id = jnp.minimum(jnp.arange(n_tiles, dtype=jnp.int32), n_valid[0] - 1)
    tile_expert = jnp.sum((tile_end[None, :] <= tile_id[:, None]).astype(jnp.int32), axis=1)
    tile_expert = jnp.minimum(tile_expert, N_EXPERTS - 1)
    pitch = _tm_pitch(x.shape[1])
    tok_row = jnp.repeat(jnp.arange(n, dtype=jnp.int32) * pitch, TOP_K)
    src_row = jnp.zeros((p_max,), jnp.int32).at[dest].set(tok_row)
    ys = _experts(h2_tm, src_row, tile_expert, n_valid.astype(jnp.int32), layer, wgu_bf, bgu, wdn_bf, bdn)
    return _combine(dest * pitch, x, wts, mod, final_g, ys, seq, n, final)


def _hy_in_kernel(x_ref, mod_ref, ng_ref, w0, w1, w2, bin_ref, sw_ref, sb_ref, u_o, x0_o, h_scr, *, n_lat_tiles, ctx_len):
    i = pl.program_id(0)

    @pl.when(pl.program_id(1) == 0)
    def _():
        m = mod_ref[0]
        h_scr[...] = _rms_mod(x_ref[...], ng_ref[...], m[0:1], m[1:2]).astype(BF16)

    h = h_scr[...]
    tm, tn = u_o.shape
    row = lax.broadcasted_iota(jnp.int32, (tm, tn), 0)
    seg = jnp.where(i >= n_lat_tiles, ctx_len, GRID_W)
    pos = row & (seg - 1)
    first = pos == 0
    last = pos == seg - 1
    bin_ = bin_ref[...]
    sw = sw_ref[...]
    sb = sb_ref[...]
    outs = []
    for s, w in enumerate((w0, w1, w2)):
        z = _dot(h, w[...]) + bin_[s:s + 1]
        zp = jnp.where(first, 0.0, pltpu.roll(z, 1, axis=0))
        zn = jnp.where(last, 0.0, pltpu.roll(z, tm - 1, axis=0))
        outs.append(sw[0, s:s + 1] * zp + sw[1, s:s + 1] * z + sw[2, s:s + 1] * zn + sb[s:s + 1])
    x0_o[...] = outs[0]
    u_o[...] = outs[2] * outs[1]


def _hy_in(x, mod, ng, w_bf, b_in, short_w, short_b, seq, ctx_len, n_rows):
    d = x.shape[1]
    tm, tn = 512, 1024
    nj = d // tn
    wspec = lambda s: pl.BlockSpec((d, tn), lambda i, j, s=s: (0, s * nj + j))
    ospec = pl.BlockSpec((tm, tn), lambda i, j: (i, j))
    osh = jax.ShapeDtypeStruct((n_rows, d), F32)
    return pl.pallas_call(
        functools.partial(_hy_in_kernel, n_lat_tiles=2 * seq // tm, ctx_len=ctx_len),
        grid=(n_rows // tm, nj),
        in_specs=[pl.BlockSpec((tm, d), lambda i, j: (i, 0)),
                  pl.BlockSpec((1, 8, d), lambda i, j: (jnp.minimum((i * tm) // seq, 2), 0, 0)),
                  pl.BlockSpec((1, d), lambda i, j: (0, 0)),
                  wspec(0), wspec(1), wspec(2),
                  pl.BlockSpec((3, tn), lambda i, j: (0, j)),
                  pl.BlockSpec((3, 3, tn), lambda i, j: (0, 0, j)),
                  pl.BlockSpec((3, tn), lambda i, j: (0, j))],
        out_specs=[ospec, ospec],
        out_shape=[osh, osh],
        scratch_shapes=[pltpu.VMEM((tm, d), BF16)],
        compiler_params=_params(("arbitrary", "arbitrary")),
        name="hyena_in_proj",
    )(x, mod, ng, w_bf, w_bf, w_bf, b_in.reshape(3, d), short_w.reshape(3, 3, d), short_b.reshape(3, d))


def _filter_kernel(z_ref, w1, b1, fr, w2, b2, w3, b3, w4, dl_ref, f_o, nrm_o, *, length):
    i = pl.program_id(0)
    z = z_ref[...]
    freq = fr[...]
    a = jnp.sin(freq * (_dot(z, w1[...]) + b1[...]))
    a = jnp.sin(freq * (_dot(a, w2[...]) + b2[...]))
    a = jnp.sin(freq * (_dot(a, w3[...]) + b3[...]))
    h = _dot(a, w4[...]) * jnp.exp(-z[:, 0:1] * dl_ref[...])
    row = lax.broadcasted_iota(jnp.int32, h.shape, 0) + i * h.shape[0]
    h = jnp.where(row == length, 0.0, h)
    f_o[...] = h

    @pl.when(i == 0)
    def _():
        nrm_o[...] = jnp.zeros_like(nrm_o)

    nrm_o[...] = nrm_o[...] + jnp.sum(jnp.abs(h), axis=0, keepdims=True)


def _pos_features(length):
    t = (np.arange(length, dtype=np.float64) / length)[:, None]
    bands = np.linspace(1e-4, HY_BANDS - 1, HY_BANDS)[None, :]
    ang = 2.0 * math.pi * t * bands
    return np.concatenate([t, np.cos(ang), -np.sin(ang)], axis=-1).astype(np.float32)


def _decay_rates(d):
    return np.abs(np.linspace(math.log(HY_DECAY_PCT_MIN) / HY_DECAY_TARGET,
                              math.log(HY_DECAY_PCT_MAX) / HY_DECAY_TARGET, d)).astype(np.float32)[None, :]


def _hyena_filter(length, w1, b1, freq, w2, b2, w3, b3, w4):
    d = w4.shape[1] // 2
    tr = min(length, 512)
    nf = length // tr
    feats = _pos_features(length)
    feats = np.concatenate([feats, feats[:1], feats[:0:-1]], axis=0)
    z = jnp.asarray(np.pad(feats, ((0, 0), (0, LANES - feats.shape[1]))))
    const = lambda i: (0, 0)
    full = lambda a: pl.BlockSpec(a.shape, const)
    mat = lambda a: jnp.pad(a, ((0, LANES - a.shape[0]), (0, LANES - a.shape[1])))
    vec = lambda a: jnp.pad(a, (0, LANES - a.shape[0])).reshape(1, LANES)
    w4p = jnp.pad(w4, ((0, LANES - w4.shape[0]), (0, 0)))
    args = [z, mat(w1), vec(b1), vec(freq), mat(w2), vec(b2), mat(w3), vec(b3), w4p, jnp.asarray(_decay_rates(d))]
    in_specs = [pl.BlockSpec((tr, LANES), lambda i: (i, 0))] + [full(a) for a in args[1:]]
    in_specs[8] = pl.BlockSpec((LANES, d), lambda i: (0, i // nf))
    return pl.pallas_call(
        functools.partial(_filter_kernel, length=length),
        grid=(2 * nf,),
        in_specs=in_specs,
        out_specs=[pl.BlockSpec((tr, d), lambda i: (i, 0)), pl.BlockSpec((1, d), const)],
        out_shape=[jax.ShapeDtypeStruct((2 * length, d), F32), jax.ShapeDtypeStruct((1, d), F32)],
        compiler_params=_params(("arbitrary",)),
        name="hyena_filter",
    )(*args)


def _dft_mats(m, k, n):
    r = np.arange(m, dtype=np.int64)[:, None]
    c = np.arange(k, dtype=np.int64)[None, :]
    ang = 2.0 * math.pi * ((r * c) % n).astype(np.float64) / n
    return np.cos(ang).astype(np.float32), np.sin(ang).astype(np.float32)


def _cmul_mat(c, s, xr, xi, conj):
    xr = xr.astype(BF16)
    cr, sr = _dot(c, xr), _dot(s, xr)
    if xi is None:
        return cr, (sr if conj else -sr)
    xi = xi.astype(BF16)
    ci, si = _dot(c, xi), _dot(s, xi)
    if conj:
        return cr - si, ci + sr
    return cr + si, ci - sr


def _slab_fwd_kernel(*refs, has_imag, has_filter):
    refs = list(refs)
    fc, fs, xr_ref = refs[:3]
    xi_ref = refs[3] if has_imag else None
    rest = refs[3 + has_imag:]
    if has_filter:
        gr_ref, gi_ref, gn_ref, or_ref, oi_ref = rest
    else:
        or_ref, oi_ref = rest
    yr, yi = _cmul_mat(fc[...].astype(BF16), fs[...].astype(BF16), xr_ref[...],
                       xi_ref[...] if has_imag else None, False)
    if has_filter:
        gn = 1.0 / gn_ref[...]
        gr, gi = gr_ref[...] * gn, gi_ref[...] * gn
        yr, yi = yr * gr - yi * gi, yr * gi + yi * gr
    or_ref[...] = yr
    oi_ref[...] = yi


def _slab_fwd(x2, k_in, parts, n_out, n, spectrum=None):
    cols = x2.shape[1]
    tc = min(cols, 2048 if n_out <= LANES else 512)
    fc, fs = _dft_mats(n_out, k_in, n)
    const = lambda c: (0, 0)
    in_specs = [pl.BlockSpec((n_out, k_in), const), pl.BlockSpec((n_out, k_in), const)]
    args = [jnp.asarray(fc), jnp.asarray(fs)]
    for part in parts:
        if part is not None:
            in_specs.append(pl.BlockSpec((k_in, tc), lambda c, part=part: (part, c)))
            args.append(x2)
    ospec = pl.BlockSpec((n_out, tc), lambda c: (0, c))
    if spectrum is not None:
        in_specs += [ospec, ospec, pl.BlockSpec((1, tc), lambda c: (0, c))]
        args += list(spectrum)
    osh = jax.ShapeDtypeStruct((n_out, cols), F32)
    return pl.pallas_call(
        functools.partial(_slab_fwd_kernel, has_imag=parts[1] is not None, has_filter=spectrum is not None),
        grid=(cols // tc,),
        in_specs=in_specs,
        out_specs=[ospec, ospec],
        out_shape=[osh, osh],
        compiler_params=_params(("arbitrary",)),
        name="dft_slab_fwd",
    )(*args)


def _mid_kernel(*refs, has_filter):
    if has_filter:
        fc, fs, twc_ref, tws_ref, ar_ref, ai_ref, gr_ref, gi_ref, gn_ref, or_ref, oi_ref = refs
    else:
        fc, fs, twc_ref, tws_ref, ar_ref, ai_ref, or_ref, oi_ref = refs
    td = ar_ref.shape[2]
    c, s = fc[...].astype(BF16), fs[...].astype(BF16)
    twc = jnp.concatenate([twc_ref[0]] * (td // LANES), axis=1)
    tws = jnp.concatenate([tws_ref[0]] * (td // LANES), axis=1)
    ar, ai = ar_ref[0], ai_ref[0]
    pr, pi_ = ar * twc + ai * tws, ai * twc - ar * tws
    br, bi = _cmul_mat(c, s, pr, pi_, False)
    if not has_filter:
        or_ref[0] = br
        oi_ref[0] = bi
        return
    gn = 1.0 / gn_ref[...]
    gr, gi = gr_ref[0] * gn, gi_ref[0] * gn
    yr, yi = br * gr - bi * gi, br * gi + bi * gr
    qr, qi = _cmul_mat(c, s, yr, yi, True)
    or_ref[0] = qr * twc - qi * tws
    oi_ref[0] = qr * tws + qi * twc


def _mid_stage(a_re, a_im, n, spectrum=None):
    n1, _, d = a_re.shape
    td = 2048
    fc, fs = _dft_mats(LANES, LANES, LANES)
    k1 = lax.broadcasted_iota(jnp.int32, (n1, LANES, LANES), 0)
    m2 = lax.broadcasted_iota(jnp.int32, (n1, LANES, LANES), 1)
    ang = ((k1 * m2) % n).astype(F32) * (2.0 * math.pi / n)
    twc, tws = jnp.cos(ang), jnp.sin(ang)
    const = lambda s, j: (0, 0)
    slab = pl.BlockSpec((1, LANES, td), lambda s, j: (s, 0, j))
    tw = pl.BlockSpec((1, LANES, LANES), lambda s, j: (s, 0, 0))
    in_specs = [pl.BlockSpec((LANES, LANES), const), pl.BlockSpec((LANES, LANES), const), tw, tw, slab, slab]
    args = [jnp.asarray(fc), jnp.asarray(fs), twc, tws, a_re, a_im]
    if spectrum is not None:
        in_specs += [slab, slab, pl.BlockSpec((1, td), lambda s, j: (0, j))]
        args += list(spectrum)
    osh = jax.ShapeDtypeStruct((n1, LANES, d), F32)
    return pl.pallas_call(
        functools.partial(_mid_kernel, has_filter=spectrum is not None),
        grid=(n1, d // td),
        in_specs=in_specs,
        out_specs=[slab, slab],
        out_shape=[osh, osh],
        compiler_params=_params(("arbitrary", "arbitrary")),
        name="dft_mid",
    )(*args)


def _slab_inv_kernel(fc, fs, pr_ref, pi_ref, u0_ref, u1_ref, a0_ref, a1_ref, skip_ref, o_ref, *, scale):
    yr, yi = _cmul_mat(fc[...].astype(BF16), fs[...].astype(BF16), pr_ref[...], pi_ref[...], True)
    skip = skip_ref[...]
    o_ref[0] = ((yr * scale + u0_ref[...] * skip) * a0_ref[...]).astype(o_ref.dtype)
    o_ref[1] = ((yi * scale + u1_ref[...] * skip) * a1_ref[...]).astype(o_ref.dtype)


def _slab_inv(p_re, p_im, u2, x02, k_out, parts, skip, n, scale):
    n_in, cols = p_re.shape
    d = skip.shape[1]
    tc = min(cols, d if n_in <= LANES else 512)
    fc, fs = _dft_mats(k_out, n_in, n)
    const = lambda c: (0, 0)
    pspec = pl.BlockSpec((n_in, tc), lambda c: (0, c))
    uspec = lambda part: pl.BlockSpec((k_out, tc), lambda c, part=part: (part, c))
    return pl.pallas_call(
        functools.partial(_slab_inv_kernel, scale=scale),
        grid=(cols // tc,),
        in_specs=[pl.BlockSpec((k_out, n_in), const), pl.BlockSpec((k_out, n_in), const),
                  pspec, pspec, uspec(parts[0]), uspec(parts[1]), uspec(parts[0]), uspec(parts[1]),
                  pl.BlockSpec((1, tc), lambda c: (0, c % (d // tc)))],
        out_specs=pl.BlockSpec((2, k_out, tc), lambda c: (0, 0, c)),
        out_shape=jax.ShapeDtypeStruct((2, k_out, cols), BF16),
        compiler_params=_params(("arbitrary",)),
        name="dft_slab_inv",
    )(jnp.asarray(fc), jnp.asarray(fs), p_re, p_im, u2, u2, x02, x02, skip)


def _long_conv(u, x0, row0, length, filt, nrm, skip):
    d = u.shape[1]
    n = 2 * length
    if n <= 4 * LANES:
        parts = (row0 // length, row0 // length + 1)
        g_re, g_im = _slab_fwd(filt, n, (0, None), n, n)
        y_re, y_im = _slab_fwd(u, length, parts, n, n, spectrum=(g_re, g_im, nrm))
        y = _slab_inv(y_re, y_im, u, x0, length, parts, skip, n, 1.0 / n)
        return y.reshape(2 * length, d)
    n1 = n // LANES
    k1 = n1 // 2
    parts = (row0 // length, row0 // length + 1)
    slabs = lambda a: a.reshape(a.shape[0] // LANES, LANES, d)
    f_re, f_im = _slab3_fwd(slabs(filt), n1, (0, None), n1)
    g_re, g_im = _mid_stage(f_re, f_im, n)
    a_re, a_im = _slab3_fwd(slabs(u), k1, parts, n1)
    p_re, p_im = _mid_stage(a_re, a_im, n, spectrum=(g_re, g_im, nrm))
    y = _slab3_inv(p_re, p_im, slabs(u), slabs(x0), k1, parts, skip, 1.0 / n)
    return y.reshape(2 * length, d)


SLAB_ROWS = 16
SLAB_COLS = 512


def _to_lanes(x3):
    xs = jnp.swapaxes(x3, 0, 1)
    return jnp.concatenate([xs[r] for r in range(xs.shape[0])], axis=1)


def _from_lanes(y2, r):
    c = y2.shape[1] // r
    return jnp.swapaxes(jnp.stack([y2[:, i * c:(i + 1) * c] for i in range(r)], axis=0), 0, 1)


def _slab3_fwd_kernel(*refs, has_imag):
    fc, fs, xr_ref = refs[:3]
    xi_ref = refs[3] if has_imag else None
    or_ref, oi_ref = refs[3 + has_imag:]
    yr, yi = _cmul_mat(fc[...].astype(BF16), fs[...].astype(BF16), _to_lanes(xr_ref[...]),
                       _to_lanes(xi_ref[...]) if has_imag else None, False)
    or_ref[...] = _from_lanes(yr, or_ref.shape[1])
    oi_ref[...] = _from_lanes(yi, oi_ref.shape[1])


def _slab3_fwd(x3, k_in, parts, n1):
    _, rows, d = x3.shape
    fc, fs = _dft_mats(n1, k_in, n1)
    const = lambda g, j: (0, 0)
    in_specs = [pl.BlockSpec((n1, k_in), const), pl.BlockSpec((n1, k_in), const)]
    args = [jnp.asarray(fc), jnp.asarray(fs)]
    for part in parts:
        if part is not None:
            in_specs.append(pl.BlockSpec((k_in, SLAB_ROWS, SLAB_COLS), lambda g, j, part=part: (part, g, j)))
            args.append(x3)
    ospec = pl.BlockSpec((n1, SLAB_ROWS, SLAB_COLS), lambda g, j: (0, g, j))
    osh = jax.ShapeDtypeStruct((n1, rows, d), F32)
    return pl.pallas_call(
        functools.partial(_slab3_fwd_kernel, has_imag=parts[1] is not None),
        grid=(rows // SLAB_ROWS, d // SLAB_COLS),
        in_specs=in_specs,
        out_specs=[ospec, ospec],
        out_shape=[osh, osh],
        compiler_params=_params(("arbitrary", "arbitrary")),
        name="dft_slab3_fwd",
    )(*args)


def _slab3_inv_kernel(fc, fs, pr_ref, pi_ref, u0_ref, u1_ref, a0_ref, a1_ref, skip_ref, o_ref, *, scale):
    yr, yi = _cmul_mat(fc[...].astype(BF16), fs[...].astype(BF16), _to_lanes(pr_ref[...]), _to_lanes(pi_ref[...]), True)
    r = u0_ref.shape[1]
    skip = skip_ref[...]
    o_ref[0] = ((_from_lanes(yr, r) * scale + u0_ref[...] * skip) * a0_ref[...]).astype(o_ref.dtype)
    o_ref[1] = ((_from_lanes(yi, r) * scale + u1_ref[...] * skip) * a1_ref[...]).astype(o_ref.dtype)


def _slab3_inv(p_re, p_im, u3, x03, k_out, parts, skip, scale):
    n1, rows, d = p_re.shape
    fc, fs = _dft_mats(k_out, n1, n1)
    const = lambda g, j: (0, 0)
    pspec = pl.BlockSpec((n1, SLAB_ROWS, SLAB_COLS), lambda g, j: (0, g, j))
    uspec = lambda part: pl.BlockSpec((k_out, SLAB_ROWS, SLAB_COLS), lambda g, j, part=part: (part, g, j))
    return pl.pallas_call(
        functools.partial(_slab3_inv_kernel, scale=scale),
        grid=(rows // SLAB_ROWS, d // SLAB_COLS),
        in_specs=[pl.BlockSpec((k_out, n1), const), pl.BlockSpec((k_out, n1), const),
                  pspec, pspec, uspec(parts[0]), uspec(parts[1]), uspec(parts[0]), uspec(parts[1]),
                  pl.BlockSpec((1, SLAB_COLS), lambda g, j: (0, j))],
        out_specs=pl.BlockSpec((2, k_out, SLAB_ROWS, SLAB_COLS), lambda g, j: (0, 0, g, j)),
        out_shape=jax.ShapeDtypeStruct((2, k_out, rows, d), BF16),
        compiler_params=_params(("arbitrary", "arbitrary")),
        name="dft_slab3_inv",
    )(jnp.asarray(fc), jnp.asarray(fs), p_re, p_im, u3, u3, x03, x03, skip)


def kernel(x, c, ctx, c_ctx, ada_w, ada_b, norm_mix_g, norm_ffn_g, final_norm_g, hg_w_in, hg_lb, hg_onorm_g, hg_w_out, hy_w_in, hy_b_in, hy_short_w, hy_short_b, hy_f_w1, hy_f_b1, hy_f_freq, hy_f_w2, hy_f_b2, hy_f_w3, hy_f_b3, hy_f_w4, hy_skip, hy_w_out, hy_b_out, moe_w_router, moe_b_router, moe_w_gu, moe_b_gu, moe_w_down, moe_b_down):
    bsz, seq, d = x.shape
    ctx_len = ctx.shape[1]
    depth = ada_w.shape[0]
    n_lat = bsz * seq
    n_all = n_lat + bsz * ctx_len
    assert bsz == 2 and seq % MOE_TILE == 0 and ctx_len == GLA_ROWS

    cond8 = jnp.zeros((8, d), F32).at[0:2].set(c).at[2].set(c_ctx)
    mods = _ada_all(cond8, ada_w, ada_b).reshape(depth, 8, N_MOD, d)[:, :3]
    mods = jnp.pad(mods, ((0, 0), (0, 0), (0, 8 - N_MOD), (0, 0)))

    lb_soft = jax.nn.softmax(hg_lb.astype(F32), axis=0)
    lower_bounds = jnp.cumsum(lb_soft, axis=0) - lb_soft[0]

    xa = jnp.concatenate([x.reshape(n_lat, d), ctx.reshape(bsz * ctx_len, d)], axis=0)
    zero_bias = jnp.zeros((1, d), F32)
    final_g = final_norm_g.reshape(1, d)
    wgu_bf, wdn_bf = moe_w_gu.astype(BF16), moe_w_down.astype(BF16)
    ctx_needed = [any(l % 2 == 0 for l in range(layer + 1, depth)) for layer in range(depth)]

    for layer in range(depth):
        j = layer // 2
        is_hgrn = layer % 2 == 0
        keep_ctx = ctx_needed[layer]
        n_out = n_all if keep_ctx else n_lat
        mod = mods[layer]
        ng1 = norm_mix_g[layer].reshape(1, d)
        ng2 = norm_ffn_g[layer].reshape(1, d)
        if is_hgrn:
            q, kf, lff, kb, lfb, v, g = _hg_in(xa, mod, ng1, lower_bounds[j].reshape(1, d),
                                               hg_w_in[j].astype(BF16), seq, n_all)
            o_fw = _gla_pass(q, kf, v, lff, seq, ctx_len, False)
            y = _gla_pass(q, kb, v, lfb, seq, ctx_len, True,
                          extra=(o_fw, g, hg_onorm_g[j].reshape(1, HG_DK)))
            w_out, b_out = hg_w_out[j].astype(BF16), zero_bias
        else:
            n_in = n_all if keep_ctx else n_lat
            u, x0 = _hy_in(xa, mod, ng1, hy_w_in[j].astype(BF16), hy_b_in[j], hy_short_w[j], hy_short_b[j],
                           seq, ctx_len, n_in)
            fpar = (hy_f_w1[j], hy_f_b1[j], hy_f_freq[j], hy_f_w2[j], hy_f_b2[j], hy_f_w3[j], hy_f_b3[j], hy_f_w4[j])
            skip = hy_skip[j].reshape(1, d)
            y = _long_conv(u, x0, 0, seq, *_hyena_filter(seq, *fpar), skip)
            if keep_ctx:
                yc = _long_conv(u, x0, n_lat, ctx_len, *_hyena_filter(ctx_len, *fpar), skip)
                y = jnp.concatenate([y, yc], axis=0)
            w_out, b_out = hy_w_out[j].astype(BF16), hy_b_out[j].reshape(1, d)
        wr = jnp.pad(moe_w_router[layer], ((0, 0), (0, LANES - N_EXPERTS)))
        br = jnp.pad(moe_b_router[layer], (0, LANES - N_EXPERTS)).reshape(1, LANES)
        xa, h2, logits = _out_proj(y, w_out, b_out, xa, mod, ng2, wr, br, seq, n_out)
        xa = _moe(h2, logits, xa, mod, final_g, layer, wgu_bf, moe_b_gu, wdn_bf, moe_b_down, seq, layer == depth - 1)
    return xa.reshape(bsz, seq, d)
```
